```python
import math
import numpy as np
import jax
import jax.numpy as jnp
from jax import lax

D_MODEL = 2048
BATCH = 4
SEQ = 2048
DEPTH = 2

GRID_W = 64
CTX_LEN = 256
NORM_EPS = 1e-6
N_MOD = 9
FFN_HIDDEN = 5632
GDN_HEADS = 8
GDN_DK = 128
GDN_DV = 128
GDN_CONV = 5
GDN_CHUNK = 64
ATT_HEADS = 8
ATT_KV_HEADS = 2
ATT_HEAD_DIM = 128
ROPE_THETA = 10000.0
Q_BLOCK = 128
FOURIER_GROUPS = 4
W_QA = GDN_HEADS * GDN_DK
W_VA = GDN_HEADS * GDN_DV
W_QB = ATT_HEADS * ATT_HEAD_DIM
W_KB = ATT_KV_HEADS * ATT_HEAD_DIM
SPLITS = (W_QA, W_QA, W_VA, W_VA, 4 * GDN_HEADS, W_QB, W_KB, W_KB)
IN_WIDTH = sum(SPLITS)
MIX_WIDTH = W_VA + W_QB

kernel_name = 'hybrid_gdn_gqa_fourier_dit'


def rmsnorm(x, gain):
    xf = x.astype(jnp.float32)
    y = xf * lax.rsqrt(jnp.mean(xf * xf, axis=-1, keepdims=True) + NORM_EPS)
    return (y * gain.astype(jnp.float32)).astype(x.dtype)


def head_rms(x, gain):
    xf = x.astype(jnp.float32)
    return xf * lax.rsqrt(jnp.mean(xf * xf, axis=-1, keepdims=True) + NORM_EPS) * gain.astype(jnp.float32)


def l2norm(x):
    return x * lax.rsqrt(jnp.sum(x * x, axis=-1, keepdims=True) + NORM_EPS)


def modnorm(x, gain, shift, scale):
    return rmsnorm(x, gain) * (1.0 + scale) + shift


def swiglu(h, w_gu, w_down):
    gate, up = jnp.split(h @ w_gu, 2, axis=-1)
    return (jax.nn.silu(gate) * up) @ w_down


def ffn_sublayer(h, m, base, gain, w_gu, w_down):
    u = modnorm(h, gain, m[:, base], m[:, base + 1])
    return h + 0.5 * m[:, base + 2] * swiglu(u, w_gu, w_down)


def short_conv(x, w):
    y = lax.conv_general_dilated(x, w[:, None, :].astype(x.dtype), window_strides=(1,),
                                 padding=[(GDN_CONV // 2, GDN_CONV // 2)],
                                 dimension_numbers=('NWC', 'WIO', 'NWC'),
                                 feature_group_count=x.shape[-1])
    return jax.nn.silu(y)


def to_heads(x, n_heads):
    b, t, _ = x.shape
    return x.reshape(b, t, n_heads, -1).transpose(0, 2, 1, 3).astype(jnp.float32)


def chunk_gated_delta(q, k, v, g, beta, s0):
    b, h, t, dk = q.shape
    cs = GDN_CHUNK
    n = t // cs
    q = (q * dk ** -0.5).reshape(b, h, n, cs, dk)
    k = k.reshape(b, h, n, cs, dk)
    v = v.reshape(b, h, n, cs, -1)
    beta = beta.reshape(b, h, n, cs)
    g = jnp.cumsum(g.reshape(b, h, n, cs), axis=-1)
    incl = jnp.tril(jnp.ones((cs, cs), dtype=bool))
    strict = jnp.tril(jnp.ones((cs, cs), dtype=bool), -1)
    decay = jnp.exp(jnp.where(incl, g[..., :, None] - g[..., None, :], -jnp.inf))
    kk = jnp.einsum('bhnid,bhnjd->bhnij', k, k)
    lower = jnp.where(strict, beta[..., :, None] * kk * decay, 0.0)
    eye = jnp.eye(cs, dtype=jnp.float32)
    tmat = lax.linalg.triangular_solve(eye + lower, jnp.broadcast_to(eye, lower.shape),
                                       left_side=True, lower=True, unit_diagonal=True)
    u = jnp.einsum('bhnij,bhnjd->bhnid', tmat, v * beta[..., None])
    w = jnp.einsum('bhnij,bhnjd->bhnid', tmat, k * (beta * jnp.exp(g))[..., None])
    qk = jnp.einsum('bhnid,bhnjd->bhnij', q, k) * decay
    q_dec = q * jnp.exp(g)[..., None]
    k_dec = k * jnp.exp(g[..., -1:] - g)[..., None]
    g_tot = jnp.exp(g[..., -1])

    def step(s, inp):
        qd, kd, u_c, w_c, a_c, gt = inp
        v_new = u_c - jnp.einsum('bhcd,bhde->bhce', w_c, s)
        o = jnp.einsum('bhcd,bhde->bhce', qd, s) + jnp.einsum('bhij,bhje->bhie', a_c, v_new)
        s = s * gt[..., None, None] + jnp.einsum('bhcd,bhce->bhde', kd, v_new)
        return s, o

    xs = tuple(jnp.moveaxis(a, 2, 0) for a in (q_dec, k_dec, u, w, qk, g_tot))
    s_final, o = lax.scan(step, s0, xs)
    return jnp.moveaxis(o, 0, 2).reshape(b, h, t, -1), s_final


def gdn_inputs(qa, ka, va, gates, conv_w, a_log, dt_bias):
    qkv = short_conv(jnp.concatenate([qa, ka, va], axis=-1), conv_w)
    q, k, v = jnp.split(qkv, [W_QA, 2 * W_QA], axis=-1)
    q, k, v = l2norm(to_heads(q, GDN_HEADS)), l2norm(to_heads(k, GDN_HEADS)), to_heads(v, GDN_HEADS)
    b, t, _ = gates.shape
    gates = gates.astype(jnp.float32).reshape(b, t, 4, GDN_HEADS).transpose(2, 0, 3, 1)
    dt = jax.nn.softplus(gates[:2] + dt_bias.astype(jnp.float32)[:, None, :, None])
    g = -jnp.exp(a_log.astype(jnp.float32))[:, None, :, None] * dt
    beta = jax.nn.sigmoid(gates[2:])
    return q, k, v, g, beta


def gdn_bidir(q, k, v, g, beta, s_fwd, s_bwd):
    o_f, s_f = chunk_gated_delta(q, k, v, g[0], beta[0], s_fwd)
    flip = lambda a: jnp.flip(a, axis=2)
    o_b, s_b = chunk_gated_delta(flip(q), flip(k), flip(v), flip(g[1]), flip(beta[1]), s_bwd)
    return o_f + flip(o_b), s_f, s_b


def gdn_output(o, z, gain):
    b, h, t, dv = o.shape
    o = head_rms(o.transpose(0, 2, 1, 3), gain)
    return (o * jax.nn.silu(z.astype(jnp.float32).reshape(b, t, h, dv))).reshape(b, t, h * dv)


def rope_axis(x, pos):
    half = x.shape[-1] // 2
    inv = ROPE_THETA ** (-jnp.arange(half, dtype=jnp.float32) / half)
    ang = pos.astype(jnp.float32)[:, None] * inv
    ang = ang.reshape((ang.shape[0],) + (1,) * (x.ndim - 3) + (half,))
    cos, sin = jnp.cos(ang), jnp.sin(ang)
    x1, x2 = x[..., :half], x[..., half:]
    return jnp.concatenate([x1 * cos - x2 * sin, x1 * sin + x2 * cos], axis=-1)


def rope_2d(x, row, col):
    half = x.shape[-1] // 2
    return jnp.concatenate([rope_axis(x[..., :half], row), rope_axis(x[..., half:], col)], axis=-1)


def gqa_qkv(qb, kb, vb, q_gain, k_gain):
    b, t, _ = qb.shape
    q = head_rms(qb.reshape(b, t, ATT_KV_HEADS, ATT_HEADS // ATT_KV_HEADS, ATT_HEAD_DIM), q_gain)
    k = head_rms(kb.reshape(b, t, ATT_KV_HEADS, ATT_HEAD_DIM), k_gain)
    v = vb.reshape(b, t, ATT_KV_HEADS, ATT_HEAD_DIM).astype(jnp.float32)
    return q, k, v


def attend(q, k, v):
    s = jnp.einsum('bqkgd,bskd->bkgqs', q, k) * (ATT_HEAD_DIM ** -0.5)
    p = jax.nn.softmax(s, axis=-1)
    return jnp.einsum('bkgqs,bskd->bqkgd', p, v)


def hybrid_mixer(u_lat, u_ctx, w_in, conv_w, a_log, dt_bias, gdn_gain, q_gain, k_gain, w_out, row, col, ctx_out):
    offs = np.cumsum(SPLITS)[:-1].tolist()
    pl = jnp.split(u_lat @ w_in, offs, axis=-1)
    pc = jnp.split(u_ctx @ w_in, offs, axis=-1)
    qc, kc, vc, gc, bc = gdn_inputs(pc[0], pc[1], pc[2], pc[4], conv_w, a_log, dt_bias)
    s0 = jnp.zeros(qc.shape[:2] + (GDN_DK, GDN_DV), jnp.float32)
    oc_a, s_fwd, s_bwd = gdn_bidir(qc, kc, vc, gc, bc, s0, s0)
    ql, kl, vl, gl, bl = gdn_inputs(pl[0], pl[1], pl[2], pl[4], conv_w, a_log, dt_bias)
    ol_a, _, _ = gdn_bidir(ql, kl, vl, gl, bl, s_fwd, s_bwd)
    qcb, kcb, vcb = gqa_qkv(pc[5], pc[6], pc[7], q_gain, k_gain)
    qlb, klb, vlb = gqa_qkv(pl[5], pl[6], pl[7], q_gain, k_gain)
    qlb, klb = rope_2d(qlb, row, col), rope_2d(klb, row, col)
    k_all = jnp.concatenate([kcb, klb], axis=1)
    v_all = jnp.concatenate([vcb, vlb], axis=1)
    b, t = qlb.shape[:2]
    q_blocks = jnp.moveaxis(qlb.reshape((b, t // Q_BLOCK, Q_BLOCK) + qlb.shape[2:]), 1, 0)
    ol_b = lax.map(lambda qb: attend(qb, k_all, v_all), q_blocks)
    ol_b = jnp.moveaxis(ol_b, 0, 1).reshape(b, t, W_QB)
    y_lat = jnp.concatenate([gdn_output(ol_a, pl[3], gdn_gain), ol_b], axis=-1).astype(u_lat.dtype) @ w_out
    if not ctx_out:
        return y_lat, None
    oc_b = attend(qcb, kcb, vcb).reshape(b, -1, W_QB)
    y_ctx = jnp.concatenate([gdn_output(oc_a, pc[3], gdn_gain), oc_b], axis=-1).astype(u_ctx.dtype) @ w_out
    return y_lat, y_ctx


def fourier_mix(u, w_f):
    b, t, d = u.shape
    ug = u.astype(jnp.float32).reshape(b, t, FOURIER_GROUPS, d // FOURIER_GROUPS)
    y = jnp.fft.fft2(ug, axes=(1, 3), norm='ortho').real
    return y.reshape(b, t, d).astype(u.dtype) @ w_f


def setup_inputs(seed: int = 0) -> dict:
    key = jax.random.key(seed)
    ks = jax.random.split(key, 20)
    f32 = jnp.float32
    n_even = (DEPTH + 1) // 2
    n_odd = DEPTH // 2

    def normal(k, shape, scale):
        return jax.random.normal(k, shape, f32) * scale

    def gain(k, shape):
        return 1.0 + 0.02 * jax.random.normal(k, shape, f32)

    dt = jnp.exp(jax.random.uniform(ks[10], (n_even, 2, GDN_HEADS), f32, minval=math.log(1e-3), maxval=math.log(1e-1)))
    return {
        'x': normal(ks[0], (BATCH, SEQ, D_MODEL), 1.0),
        'c': normal(ks[1], (BATCH, D_MODEL), 1.0),
        'ctx': normal(ks[2], (BATCH, CTX_LEN, D_MODEL), 1.0),
        'c_ctx': normal(ks[3], (D_MODEL,), 1.0),
        'w_mod': normal(ks[4], (DEPTH, D_MODEL, N_MOD * D_MODEL), 0.5 * D_MODEL ** -0.5),
        'b_mod': normal(ks[5], (DEPTH, N_MOD * D_MODEL), 0.02),
        'norm_w': gain(ks[6], (DEPTH, 3, D_MODEL)),
        'ffn_w_gu': normal(ks[7], (DEPTH, 2, D_MODEL, 2 * FFN_HIDDEN), D_MODEL ** -0.5),
        'ffn_w_down': normal(ks[8], (DEPTH, 2, FFN_HIDDEN, D_MODEL), FFN_HIDDEN ** -0.5),
        'w_in': normal(ks[9], (n_even, D_MODEL, IN_WIDTH), D_MODEL ** -0.5),
        'conv_w': normal(ks[11], (n_even, GDN_CONV, 2 * W_QA + W_VA), GDN_CONV ** -0.5),
        'a_log': jnp.log(jax.random.uniform(ks[12], (n_even, 2, GDN_HEADS), f32, minval=1.0, maxval=16.0)),
        'dt_bias': dt + jnp.log(-jnp.expm1(-dt)),
        'gdn_norm': gain(ks[13], (n_even, GDN_DV)),
        'q_norm': gain(ks[14], (n_even, ATT_HEAD_DIM)),
        'k_norm': gain(ks[15], (n_even, ATT_HEAD_DIM)),
        'w_out': normal(ks[16], (n_even, MIX_WIDTH, D_MODEL), MIX_WIDTH ** -0.5),
        'w_fourier': normal(ks[17], (n_odd, D_MODEL, D_MODEL), D_MODEL ** -0.5),
        'final_norm': gain(ks[18], (D_MODEL,)),
    }


def reference(x, c, ctx, c_ctx, w_mod, b_mod, norm_w, ffn_w_gu, ffn_w_down, w_in, conv_w, a_log, dt_bias,
              gdn_norm, q_norm, k_norm, w_out, w_fourier, final_norm):
    b, t, d = x.shape
    rows = t // GRID_W
    row = jnp.repeat(jnp.arange(rows), GRID_W)
    col = jnp.tile(jnp.arange(GRID_W), rows)
    silu_c = jax.nn.silu(c)
    silu_cc = jax.nn.silu(c_ctx)[None, :]
    h_lat, h_ctx = x, ctx
    for layer in range(DEPTH):
        ctx_out = any(j % 2 == 0 for j in range(layer + 1, DEPTH))
        ctx_in = layer % 2 == 0 or ctx_out
        m_lat = (silu_c @ w_mod[layer] + b_mod[layer]).reshape(b, N_MOD, 1, d)
        m_ctx = (silu_cc @ w_mod[layer] + b_mod[layer]).reshape(1, N_MOD, 1, d)
        h_lat = ffn_sublayer(h_lat, m_lat, 0, norm_w[layer, 0], ffn_w_gu[layer, 0], ffn_w_down[layer, 0])
        if ctx_in:
            h_ctx = ffn_sublayer(h_ctx, m_ctx, 0, norm_w[layer, 0], ffn_w_gu[layer, 0], ffn_w_down[layer, 0])
        u_lat = modnorm(h_lat, norm_w[layer, 1], m_lat[:, 3], m_lat[:, 4])
        if layer % 2 == 0:
            e = layer // 2
            u_ctx = modnorm(h_ctx, norm_w[layer, 1], m_ctx[:, 3], m_ctx[:, 4])
            y_lat, y_ctx = hybrid_mixer(u_lat, u_ctx, w_in[e], conv_w[e], a_log[e], dt_bias[e], gdn_norm[e],
                                        q_norm[e], k_norm[e], w_out[e], row, col, ctx_out)
        else:
            o = layer // 2
            y_lat = fourier_mix(u_lat, w_fourier[o])
            if ctx_out:
                y_ctx = fourier_mix(modnorm(h_ctx, norm_w[layer, 1], m_ctx[:, 3], m_ctx[:, 4]), w_fourier[o])
        h_lat = h_lat + m_lat[:, 5] * y_lat
        h_lat = ffn_sublayer(h_lat, m_lat, 6, norm_w[layer, 2], ffn_w_gu[layer, 1], ffn_w_down[layer, 1])
        if ctx_out:
            h_ctx = h_ctx + m_ctx[:, 5] * y_ctx
            h_ctx = ffn_sublayer(h_ctx, m_ctx, 6, norm_w[layer, 2], ffn_w_gu[layer, 1], ffn_w_down[layer, 1])
    return rmsnorm(h_lat, final_norm)
```

```python
import functools
import math

import numpy as np
import jax
import jax.numpy as jnp
from jax import lax
from jax.experimental import pallas as pl
from jax.experimental.pallas import tpu as pltpu

D = 2048
NB = 4
T = 2048
GRID_W = 64
TC = 256
TT = T + TC
DEPTH = 2
EPS = 1e-6
N_MOD = 9
FH = 5632
NH = 8
HD = 128
KVH = 2
GQ = NH // KVH
CONV_K = 5
CS = 64
NCH = TT // CS
NCH_LAT = T // CS
ROPE_THETA = 10000.0
FG = 4
GC = D // FG
W_A = NH * HD
NP = 4 * W_A + W_A + 2 * KVH * HD + 128
COL_Z = 3 * W_A
COL_QB = 4 * W_A
COL_KB = COL_QB + W_A
COL_VB = COL_KB + KVH * HD
COL_G = COL_VB + KVH * HD

VMEM_LIMIT = 52 * 1024 * 1024
BF = jnp.bfloat16
F32 = jnp.float32


def _cp(sem):
    return pltpu.CompilerParams(dimension_semantics=sem, vmem_limit_bytes=VMEM_LIMIT)


def _dot(a, b):
    return jnp.dot(a, b, preferred_element_type=F32)


def _dot_nt(a, b):
    return lax.dot_general(a, b, (((1,), (1,)), ((), ())), preferred_element_type=F32)


def _dot_tn(a, b):
    return lax.dot_general(a, b, (((0,), (0,)), ((), ())), preferred_element_type=F32)


def _silu(x):
    return x * jax.nn.sigmoid(x)


def _modnorm(x, gain, shift, scale):
    y = x * lax.rsqrt(jnp.mean(x * x, axis=-1, keepdims=True) + EPS) * gain
    return y * (1.0 + scale) + shift


def _mod_kernel(c_ref, w_ref, b_ref, o_ref):
    a = _silu(c_ref[...]).astype(BF)
    o_ref[...] = _dot(a, w_ref[...].astype(BF)) + b_ref[...]


def _modulation(cvec, w_mod, b_mod):
    tn = 1024
    n = N_MOD * D
    return pl.pallas_call(
        _mod_kernel,
        grid=(DEPTH, n // tn),
        in_specs=[pl.BlockSpec((8, D), lambda l, j: (0, 0)),
                  pl.BlockSpec((None, D, tn), lambda l, j: (l, 0, j)),
                  pl.BlockSpec((None, 1, tn), lambda l, j: (l, 0, j))],
        out_specs=pl.BlockSpec((None, 8, tn), lambda l, j: (l, 0, j)),
        out_shape=jax.ShapeDtypeStruct((DEPTH, 8, n), F32),
        compiler_params=_cp(("parallel", "arbitrary")),
        name="modulation",
    )(cvec, w_mod, b_mod.reshape(DEPTH, 1, n))


def _ffn_kernel(h_ref, sh_ref, sc_ref, gt_ref, gain_ref, wg_ref, wu_ref, wd_ref, *rest, nj, final):
    if final:
        fin_ref, o_ref, u_ref = rest
    else:
        o_ref, u_ref = rest
    j = pl.program_id(2)

    @pl.when(j == 0)
    def _():
        u = _modnorm(h_ref[...], gain_ref[...], sh_ref[...], sc_ref[...])
        u_ref[...] = u.astype(BF)
        o_ref[...] = jnp.zeros_like(o_ref)

    u = u_ref[...]
    g = _dot(u, wg_ref[...])
    up = _dot(u, wu_ref[...])
    act = (_silu(g) * up).astype(BF)
    o_ref[...] += _dot(act, wd_ref[...])

    @pl.when(j == nj - 1)
    def _():
        r = h_ref[...] + 0.5 * gt_ref[...] * o_ref[...]
        if final:
            r = r * lax.rsqrt(jnp.mean(r * r, axis=-1, keepdims=True) + EPS) * fin_ref[...]
        o_ref[...] = r


def _ffn(h, mod, base, mod_row0, gain, w_gu, w_down, final_gain=None, tm=512, tf=512):
    nb, rows, _ = h.shape
    nj = FH // tf
    mspec = lambda k: pl.BlockSpec((None, None, 1, D), lambda b, i, j: (b + mod_row0, base + k, 0, 0))
    vec = pl.BlockSpec((1, D), lambda b, i, j: (0, 0))
    in_specs = [pl.BlockSpec((None, tm, D), lambda b, i, j: (b, i, 0)),
                mspec(0), mspec(1), mspec(2), vec,
                pl.BlockSpec((D, tf), lambda b, i, j: (0, j)),
                pl.BlockSpec((D, tf), lambda b, i, j: (0, j + nj)),
                pl.BlockSpec((tf, D), lambda b, i, j: (j, 0))]
    args = [h, mod, mod, mod, gain.reshape(1, D), w_gu, w_gu, w_down]
    if final_gain is not None:
        in_specs.append(vec)
        args.append(final_gain.reshape(1, D))
    return pl.pallas_call(
        functools.partial(_ffn_kernel, nj=nj, final=final_gain is not None),
        grid=(nb, rows // tm, nj),
        in_specs=in_specs,
        out_specs=pl.BlockSpec((None, tm, D), lambda b, i, j: (b, i, 0)),
        out_shape=jax.ShapeDtypeStruct(h.shape, F32),
        scratch_shapes=[pltpu.VMEM((tm, D), BF)],
        compiler_params=_cp(("parallel", "parallel", "arbitrary")),
        name="ffn",
    )(*args)


def _proj_kernel(h_ref, sh_ref, sc_ref, gain_ref, w_ref, *rest):
    o_ref, u_ref = rest[-2], rest[-1]

    @pl.when(pl.program_id(2) == 0)
    def _():
        u_ref[...] = _modnorm(h_ref[...], gain_ref[...], sh_ref[...], sc_ref[...]).astype(BF)

    o_ref[...] = _dot(u_ref[...], w_ref[...])


def _proj(h, mod, mod_row0, mod_row_step, gain, w, tm, row_block0, prev=None, tn=1152):
    nb, rows, _ = h.shape
    mspec = lambda k: pl.BlockSpec((None, None, 1, D),
                                   lambda b, i, j: (b * mod_row_step + mod_row0, 3 + k, 0, 0))
    in_specs = [pl.BlockSpec((None, tm, D), lambda b, i, j: (b, i, 0)),
                mspec(0), mspec(1),
                pl.BlockSpec((1, D), lambda b, i, j: (0, 0)),
                pl.BlockSpec((D, tn), lambda b, i, j: (0, j))]
    args = [h, mod, mod, gain.reshape(1, D), w]
    aliases = {}
    if prev is not None:
        in_specs.append(pl.BlockSpec(memory_space=pl.ANY))
        args.append(prev)
        aliases = {len(args) - 1: 0}
    return pl.pallas_call(
        _proj_kernel,
        grid=(nb, rows // tm, NP // tn),
        in_specs=in_specs,
        out_specs=pl.BlockSpec((None, tm, tn), lambda b, i, j: (b, i + row_block0, j)),
        out_shape=jax.ShapeDtypeStruct((NB, TT, NP), F32),
        scratch_shapes=[pltpu.VMEM((tm, D), BF)],
        input_output_aliases=aliases,
        compiler_params=_cp(("parallel", "parallel", "arbitrary")),
        name="in_proj",
    )(*args)


def _conv_kernel(x_ref, w_ref, o_ref, pad_ref):
    j = pl.program_id(1)
    half = CONV_K // 2
    for r0, n in ((0, T), (T, TC)):
        pad_ref[0:8, :] = jnp.zeros((8, HD), F32)
        pad_ref[8:8 + n, :] = x_ref[r0:r0 + n, :]
        pad_ref[8 + n:16 + n, :] = jnp.zeros((8, HD), F32)
        acc = jnp.zeros((n, HD), F32)
        for k in range(CONV_K):
            acc = acc + w_ref[k:k + 1, :] * pad_ref[8 + k - half:8 + k - half + n, :]
        y = _silu(acc)
        yn = y * lax.rsqrt(jnp.sum(y * y, axis=-1, keepdims=True) + EPS)
        o_ref[r0:r0 + n, :] = jnp.where(j < 2 * NH, yn, y)


def _gdn_conv(p, conv_w):
    return pl.pallas_call(
        _conv_kernel,
        grid=(NB, 3 * NH),
        in_specs=[pl.BlockSpec((None, TT, HD), lambda b, j: (b, 0, j)),
                  pl.BlockSpec((CONV_K, HD), lambda b, j: (0, j))],
        out_specs=pl.BlockSpec((None, TT, HD), lambda b, j: (b, 0, j)),
        out_shape=jax.ShapeDtypeStruct((NB, TT, 3 * W_A), F32),
        scratch_shapes=[pltpu.VMEM((T + 16, HD), F32)],
        compiler_params=_cp(("parallel", "parallel")),
        name="gdn_conv",
    )(p, conv_w)


def _gdn_kernel(q_ref, k_ref, v_ref, gt_ref, prm_ref, o_ref, s_ref):
    d = pl.program_id(1)
    s = pl.program_id(2)

    @pl.when(s == 0)
    def _():
        s_ref[...] = jnp.zeros_like(s_ref)

    sgn = 1 - 2 * d
    ri = lax.broadcasted_iota(jnp.int32, (CS, CS), 0)
    ci = lax.broadcasted_iota(jnp.int32, (CS, CS), 1)
    dm = (ri - ci) * sgn
    incl = dm >= 0
    incl_t = dm <= 0
    strict = dm > 0
    eye = ri == ci

    gts = gt_ref[...]
    prm = prm_ref[...]
    x = gts + prm[1:2, :]
    softplus = jnp.maximum(x, 0.0) + jnp.log(1.0 + jnp.exp(-jnp.abs(x)))
    glog = -jnp.exp(prm[0:1, :]) * softplus
    beta = jax.nn.sigmoid(gts)
    fwd = d == 0
    glog = jnp.where(fwd, glog, pltpu.roll(glog, 128 - NH, axis=1))
    beta = jnp.where(fwd, beta, pltpu.roll(beta, 128 - NH, axis=1))

    for h in range(NH):
        sl = slice(h * HD, (h + 1) * HD)
        qh = q_ref[:, sl] * (HD ** -0.5)
        kh = k_ref[:, sl]
        vh = v_ref[:, sl]
        g_c = glog[:, h:h + 1]
        b_c = beta[:, 2 * NH + h:2 * NH + h + 1]
        g_r = jnp.sum(jnp.where(eye, g_c, 0.0), axis=0, keepdims=True)
        gc_c = jnp.sum(jnp.where(incl, g_r, 0.0), axis=1, keepdims=True)
        gc_r = jnp.sum(jnp.where(incl_t, g_c, 0.0), axis=0, keepdims=True)
        g_last = jnp.sum(g_c, axis=0, keepdims=True)
        decay = jnp.exp(jnp.where(incl, gc_c - gc_r, -1e30))
        kb = kh.astype(BF)
        kk = _dot_nt(kb, kb)
        lower = jnp.where(strict, b_c * kk * decay, 0.0)
        tmat = jnp.where(eye, 1.0, 0.0) - jnp.where((ri // 2) == (ci // 2), lower, 0.0)
        for bs in (2, 4, 8, 16, 32):
            off = ((ri // (2 * bs)) == (ci // (2 * bs))) & ((ri // bs) != (ci // bs))
            m_off = jnp.where(off, lower, 0.0).astype(BF)
            tb = tmat.astype(BF)
            tmat = tmat - _dot(tb, _dot(m_off, tb).astype(BF))
        eg = jnp.exp(gc_c)
        rhs = jnp.concatenate([vh * b_c, kh * (b_c * eg)], axis=1).astype(BF)
        uw = _dot(tmat.astype(BF), rhs)
        u = uw[:, :HD]
        w = uw[:, HD:]
        qk = _dot_nt(qh.astype(BF), kb) * decay
        q_dec = (qh * eg).astype(BF)
        k_dec = (kh * jnp.exp(g_last - gc_c)).astype(BF)
        st = s_ref[h]
        stb = st.astype(BF)
        v_new = u - _dot(w.astype(BF), stb)
        vnb = v_new.astype(BF)
        o_ref[:, sl] = _dot(q_dec, stb) + _dot(qk.astype(BF), vnb)
        s_ref[h] = st * jnp.exp(g_last) + _dot_tn(k_dec, vnb)


def _gdn_scan(qkv, p, prm):
    def chunk(d, s):
        return jnp.where(d == 0, (s + NCH_LAT) % NCH, NCH - 1 - s)

    qkv_spec = lambda c: pl.BlockSpec((None, CS, W_A), lambda b, d, s: (b, chunk(d, s), c))
    return pl.pallas_call(
        _gdn_kernel,
        grid=(NB, 2, NCH),
        in_specs=[qkv_spec(0), qkv_spec(1), qkv_spec(2),
                  pl.BlockSpec((None, CS, 128), lambda b, d, s: (b, chunk(d, s), COL_G // 128)),
                  pl.BlockSpec((8, 128), lambda b, d, s: (0, 0))],
        out_specs=pl.BlockSpec((None, None, CS, W_A), lambda b, d, s: (b, d, chunk(d, s), 0)),
        out_shape=jax.ShapeDtypeStruct((NB, 2, TT, W_A), F32),
        scratch_shapes=[pltpu.VMEM((NH, HD, HD), F32)],
        compiler_params=_cp(("parallel", "parallel", "arbitrary")),
        name="gdn_scan",
    )(qkv, qkv, qkv, p, prm)


def _rope_kernel(x_ref, g_ref, cos_ref, sin_ref, o_ref):
    x = x_ref[...]
    y = x * lax.rsqrt(jnp.mean(x * x, axis=-1, keepdims=True) + EPS) * g_ref[...]
    lane = lax.broadcasted_iota(jnp.int32, y.shape, 1)
    partner = jnp.where((lane % 64) < 32, pltpu.roll(y, 128 - 32, axis=1), pltpu.roll(y, 32, axis=1))
    o_ref[...] = (y * cos_ref[...] + partner * sin_ref[...]).astype(o_ref.dtype)


def _rope_norm(p, gains, cos_t, sin_t):
    nh = NH + KVH
    return pl.pallas_call(
        _rope_kernel,
        grid=(NB, nh),
        in_specs=[pl.BlockSpec((None, TT, HD), lambda b, j: (b, 0, COL_QB // HD + j)),
                  pl.BlockSpec((None, 1, HD), lambda b, j: (j, 0, 0)),
                  pl.BlockSpec((TT, HD), lambda b, j: (0, 0)),
                  pl.BlockSpec((TT, HD), lambda b, j: (0, 0))],
        out_specs=pl.BlockSpec((None, TT, HD), lambda b, j: (b, 0, j)),
        out_shape=jax.ShapeDtypeStruct((NB, TT, nh * HD), BF),
        compiler_params=_cp(("parallel", "parallel")),
        name="rope_norm",
    )(p, gains, cos_t, sin_t)


def _attn_kernel(q_ref, k_ref, v_ref, o_ref):
    k = k_ref[...]
    v = v_ref[...].astype(BF)
    for g in range(GQ):
        sl = slice(g * HD, (g + 1) * HD)
        s = _dot_nt(q_ref[:, sl], k)
        pr = jnp.exp(s - jnp.max(s, axis=-1, keepdims=True))
        den = jnp.sum(pr, axis=-1, keepdims=True)
        o_ref[:, sl] = (_dot(pr.astype(BF), v) / den).astype(o_ref.dtype)


def _attention(qk, p, tq=512):
    return pl.pallas_call(
        _attn_kernel,
        grid=(NB, KVH, T // tq),
        in_specs=[pl.BlockSpec((None, tq, GQ * HD), lambda b, kv, i: (b, i, kv)),
                  pl.BlockSpec((None, TT, HD), lambda b, kv, i: (b, 0, NH + kv)),
                  pl.BlockSpec((None, TT, HD), lambda b, kv, i: (b, 0, COL_VB // HD + kv))],
        out_specs=pl.BlockSpec((None, tq, GQ * HD), lambda b, kv, i: (b, i, kv)),
        out_shape=jax.ShapeDtypeStruct((NB, T, W_A), BF),
        compiler_params=_cp(("parallel", "parallel", "arbitrary")),
        name="attention",
    )(qk, qk, p)


def _mixout_kernel(of_ref, ob_ref, z_ref, a_ref, h_ref, gt_ref, gain_ref, w_ref, o_ref):
    parts = []
    for hd in range(NH):
        sl = slice(hd * HD, (hd + 1) * HD)
        o = of_ref[:, sl] + ob_ref[:, sl]
        n = o * lax.rsqrt(jnp.mean(o * o, axis=-1, keepdims=True) + EPS) * gain_ref[...]
        parts.append((n * _silu(z_ref[:, sl])).astype(BF))
    gd = jnp.concatenate(parts, axis=1)
    y = _dot(gd, w_ref[0:W_A, :]) + _dot(a_ref[...], w_ref[W_A:2 * W_A, :])
    o_ref[...] = h_ref[...] + gt_ref[...] * y


def _mixout(o_gdn, p, attn, h, mod, gain, w_out, tm=256):
    return pl.pallas_call(
        _mixout_kernel,
        grid=(NB, T // tm),
        in_specs=[pl.BlockSpec((None, None, tm, W_A), lambda b, i: (b, 0, i, 0)),
                  pl.BlockSpec((None, None, tm, W_A), lambda b, i: (b, 1, i, 0)),
                  pl.BlockSpec((None, tm, W_A), lambda b, i: (b, i, COL_Z // W_A)),
                  pl.BlockSpec((None, tm, W_A), lambda b, i: (b, i, 0)),
                  pl.BlockSpec((None, tm, D), lambda b, i: (b, i, 0)),
                  pl.BlockSpec((None, None, 1, D), lambda b, i: (b, 5, 0, 0)),
                  pl.BlockSpec((1, HD), lambda b, i: (0, 0)),
                  pl.BlockSpec((D, D), lambda b, i: (0, 0))],
        out_specs=pl.BlockSpec((None, tm, D), lambda b, i: (b, i, 0)),
        out_shape=jax.ShapeDtypeStruct((NB, T, D), F32),
        compiler_params=_cp(("parallel", "parallel")),
        name="mix_out",
    )(o_gdn, o_gdn, p, attn, h, mod, gain.reshape(1, HD), w_out)


def _dft_tables():
    def cs(n):
        idx = np.arange(n, dtype=np.int64)
        ang = 2.0 * np.pi * ((idx[:, None] * idx[None, :]) % n).astype(np.float64) / n
        return np.cos(ang) / math.sqrt(n), np.sin(ang) / math.sqrt(n)

    cc, sc = cs(GC)
    ct, st = cs(T)
    ch = np.concatenate([cc, sc], axis=1).astype(np.float32)
    tm = np.concatenate([ct, -st], axis=1).astype(np.float32)
    return jnp.asarray(ch, dtype=BF), jnp.asarray(tm, dtype=BF)


def _fchan_kernel(h_ref, sh_ref, sc_ref, gain_ref, w_ref, o_ref):
    u = _modnorm(h_ref[...], gain_ref[...], sh_ref[...], sc_ref[...]).astype(BF)
    w = w_ref[...]
    for g in range(FG):
        sl = slice(g * GC, (g + 1) * GC)
        ab = _dot(u[:, sl], w)
        o_ref[0, :, sl] = ab[:, :GC].astype(o_ref.dtype)
        o_ref[1, :, sl] = ab[:, GC:].astype(o_ref.dtype)


def _fourier_channels(h, mod, gain, ch_tab, tm=512):
    return pl.pallas_call(
        _fchan_kernel,
        grid=(NB, T // tm),
        in_specs=[pl.BlockSpec((None, tm, D), lambda b, i: (b, i, 0)),
                  pl.BlockSpec((None, None, 1, D), lambda b, i: (b, 3, 0, 0)),
                  pl.BlockSpec((None, None, 1, D), lambda b, i: (b, 4, 0, 0)),
                  pl.BlockSpec((1, D), lambda b, i: (0, 0)),
                  pl.BlockSpec((GC, 2 * GC), lambda b, i: (0, 0))],
        out_specs=pl.BlockSpec((None, 2, tm, D), lambda b, i: (b, 0, i, 0)),
        out_shape=jax.ShapeDtypeStruct((NB, 2, T, D), BF),
        compiler_params=_cp(("parallel", "parallel")),
        name="fourier_channels",
    )(h, mod, mod, gain.reshape(1, D), ch_tab)


def _mm_kernel(a_ref, b_ref, o_ref):
    o_ref[...] = _dot(a_ref[...], b_ref[...]).astype(o_ref.dtype)


def _fourier_time(tm_tab, ab, tm=1024, tn=1024):
    return pl.pallas_call(
        _mm_kernel,
        grid=(NB, D // tn, T // tm),
        in_specs=[pl.BlockSpec((tm, 2 * T), lambda b, j, i: (i, 0)),
                  pl.BlockSpec((None, 2 * T, tn), lambda b, j, i: (b, 0, j))],
        out_specs=pl.BlockSpec((None, tm, tn), lambda b, j, i: (b, i, j)),
        out_shape=jax.ShapeDtypeStruct((NB, T, D), BF),
        compiler_params=_cp(("parallel", "parallel", "arbitrary")),
        name="fourier_time",
    )(tm_tab, ab)


def _resmm_kernel(y_ref, w_ref, h_ref, gt_ref, o_ref):
    o_ref[...] = h_ref[...] + gt_ref[...] * _dot(y_ref[...], w_ref[...])


def _fourier_out(y, w_f, h, mod, tm=512):
    return pl.pallas_call(
        _resmm_kernel,
        grid=(NB, T // tm),
        in_specs=[pl.BlockSpec((None, tm, D), lambda b, i: (b, i, 0)),
                  pl.BlockSpec((D, D), lambda b, i: (0, 0)),
                  pl.BlockSpec((None, tm, D), lambda b, i: (b, i, 0)),
                  pl.BlockSpec((None, None, 1, D), lambda b, i: (b, 5, 0, 0))],
        out_specs=pl.BlockSpec((None, tm, D), lambda b, i: (b, i, 0)),
        out_shape=jax.ShapeDtypeStruct((NB, T, D), F32),
        compiler_params=_cp(("parallel", "parallel")),
        name="fourier_out",
    )(y, w_f, h, mod)


def _rope_tables():
    half = HD // 4
    inv = ROPE_THETA ** (-np.arange(half, dtype=np.float64) / half)
    t = np.arange(T)
    ang_r = (t // GRID_W)[:, None] * inv[None, :]
    ang_c = (t % GRID_W)[:, None] * inv[None, :]
    cos = np.concatenate([np.cos(ang_r)] * 2 + [np.cos(ang_c)] * 2, axis=1)
    sin = np.concatenate([-np.sin(ang_r), np.sin(ang_r), -np.sin(ang_c), np.sin(ang_c)], axis=1)
    cos = np.concatenate([cos, np.ones((TC, HD))], axis=0).astype(np.float32)
    sin = np.concatenate([sin, np.zeros((TC, HD))], axis=0).astype(np.float32)
    return jnp.asarray(cos), jnp.asarray(sin)


def kernel(x, c, ctx, c_ctx, w_mod, b_mod, norm_w, ffn_w_gu, ffn_w_down, w_in, conv_w, a_log, dt_bias,
           gdn_norm, q_norm, k_norm, w_out, w_fourier, final_norm):
    cvec = jnp.concatenate([c, c_ctx[None, :], jnp.zeros((8 - NB - 1, D), F32)], axis=0)
    mod = _modulation(cvec, w_mod, b_mod).reshape(DEPTH, 8, N_MOD, 1, D)
    w_gu = ffn_w_gu.astype(BF)
    w_dn = ffn_w_down.astype(BF)

    m0 = mod[0]
    h = _ffn(x, m0, 0, 0, norm_w[0, 0], w_gu[0, 0], w_dn[0, 0])
    h_ctx = _ffn(ctx.reshape(1, NB * TC, D), m0, 0, NB, norm_w[0, 0], w_gu[0, 0], w_dn[0, 0])
    h_ctx = h_ctx.reshape(NB, TC, D)

    wi = w_in[0]
    w_p = jnp.concatenate([wi[:, :4 * W_A], wi[:, 4 * W_A + 4 * NH:], wi[:, 4 * W_A:4 * W_A + 4 * NH],
                           jnp.zeros((D, 128 - 4 * NH), F32)], axis=1).astype(BF)
    p = _proj(h, m0, 0, 1, norm_w[0, 1], w_p, tm=512, row_block0=0)
    p = _proj(h_ctx, m0, NB, 0, norm_w[0, 1], w_p, tm=TC, row_block0=T // TC, prev=p)

    qkv = _gdn_conv(p, conv_w[0])
    prm = jnp.zeros((8, 128), F32)
    prm = prm.at[0, :2 * NH].set(a_log[0].reshape(-1)).at[1, :2 * NH].set(dt_bias[0].reshape(-1))
    o_gdn = _gdn_scan(qkv, p, prm)

    gains = jnp.concatenate([jnp.broadcast_to(q_norm[0] * (HD ** -0.5), (NH, HD)),
                             jnp.broadcast_to(k_norm[0], (KVH, HD))], axis=0).reshape(NH + KVH, 1, HD)
    cos_t, sin_t = _rope_tables()
    qk = _rope_norm(p, gains, cos_t, sin_t)
    attn = _attention(qk, p)

    h = _mixout(o_gdn, p, attn, h, m0, gdn_norm[0], w_out[0].astype(BF))
    h = _ffn(h, m0, 6, 0, norm_w[0, 2], w_gu[0, 1], w_dn[0, 1])

    m1 = mod[1]
    h = _ffn(h, m1, 0, 0, norm_w[1, 0], w_gu[1, 0], w_dn[1, 0])
    ch_tab, tm_tab = _dft_tables()
    ab = _fourier_channels(h, m1, norm_w[1, 1], ch_tab)
    y = _fourier_time(tm_tab, ab.reshape(NB, 2 * T, D))
    h = _fourier_out(y, w_fourier[0].astype(BF), h, m1)
    h = _ffn(h, m1, 6, 0, norm_w[1, 2], w_gu[1, 1], w_dn[1, 1], final_gain=final_norm)
    return h
```

```python
import functools
import math

import numpy as np
import jax
import jax.numpy as jnp
from jax import lax
from jax.experimental import pallas as pl
from jax.experimental.pallas import tpu as pltpu

D = 2048
NB = 4
T = 2048
GRID_W = 64
TC = 256
TT = T + TC
DEPTH = 2
EPS = 1e-6
N_MOD = 9
FH = 5632
NH = 8
HD = 128
KVH = 2
GQ = NH // KVH
CONV_K = 5
CS = 64
NCH = TT // CS
NCH_LAT = T // CS
ROPE_THETA = 10000.0
FG = 4
GC = D // FG
W_A = NH * HD
NP = 4 * W_A + W_A + 2 * KVH * HD + 128
COL_Z = 3 * W_A
COL_QB = 4 * W_A
COL_KB = COL_QB + W_A
COL_VB = COL_KB + KVH * HD
COL_G = COL_VB + KVH * HD

VMEM_LIMIT = 52 * 1024 * 1024
BF = jnp.bfloat16
F32 = jnp.float32


def _cp(sem):
    return pltpu.CompilerParams(dimension_semantics=sem, vmem_limit_bytes=VMEM_LIMIT)


def _dot(a, b):
    return jnp.dot(a, b, preferred_element_type=F32)


def _dot_nt(a, b):
    return lax.dot_general(a, b, (((1,), (1,)), ((), ())), preferred_element_type=F32)


def _dot_tn(a, b):
    return lax.dot_general(a, b, (((0,), (0,)), ((), ())), preferred_element_type=F32)


def _silu(x):
    return x * jax.nn.sigmoid(x)


def _modnorm(x, gain, shift, scale):
    y = x * lax.rsqrt(jnp.mean(x * x, axis=-1, keepdims=True) + EPS) * gain
    return y * (1.0 + scale) + shift


def _mod_kernel(c_ref, w_ref, b_ref, o_ref):
    a = _silu(c_ref[...]).astype(BF)
    o_ref[...] = _dot(a, w_ref[...].astype(BF)) + b_ref[...]


def _modulation(cvec, w_mod, b_mod):
    tn = 1024
    n = N_MOD * D
    return pl.pallas_call(
        _mod_kernel,
        grid=(DEPTH, n // tn),
        in_specs=[pl.BlockSpec((8, D), lambda l, j: (0, 0)),
                  pl.BlockSpec((None, D, tn), lambda l, j: (l, 0, j)),
                  pl.BlockSpec((None, 1, tn), lambda l, j: (l, 0, j))],
        out_specs=pl.BlockSpec((None, 8, tn), lambda l, j: (l, 0, j)),
        out_shape=jax.ShapeDtypeStruct((DEPTH, 8, n), F32),
        compiler_params=_cp(("parallel", "arbitrary")),
        name="modulation",
    )(cvec, w_mod, b_mod.reshape(DEPTH, 1, n))


def _ffn_kernel(h_ref, sh_ref, sc_ref, gt_ref, gain_ref, wg_ref, wu_ref, wd_ref, *rest, nj, final):
    if final:
        fin_ref, o_ref, u_ref = rest
    else:
        o_ref, u_ref = rest
    j = pl.program_id(2)

    @pl.when(j == 0)
    def _():
        u = _modnorm(h_ref[...], gain_ref[...], sh_ref[...], sc_ref[...])
        u_ref[...] = u.astype(BF)
        o_ref[...] = jnp.zeros_like(o_ref)

    u = u_ref[...]
    g = _dot(u, wg_ref[...])
    up = _dot(u, wu_ref[...])
    act = (_silu(g) * up).astype(BF)
    o_ref[...] += _dot(act, wd_ref[...])

    @pl.when(j == nj - 1)
    def _():
        r = h_ref[...] + 0.5 * gt_ref[...] * o_ref[...]
        if final:
            r = r * lax.rsqrt(jnp.mean(r * r, axis=-1, keepdims=True) + EPS) * fin_ref[...]
        o_ref[...] = r


def _ffn(h, mod, base, mod_row0, gain, w_gu, w_down, layer, which, final_gain=None, tm=512, tf=512):
    nb, rows, _ = h.shape
    nj = FH // tf
    mspec = lambda k: pl.BlockSpec((None, None, 1, D), lambda b, i, j: (b + mod_row0, base + k, 0, 0))
    vec = pl.BlockSpec((1, D), lambda b, i, j: (0, 0))
    in_specs = [pl.BlockSpec((None, tm, D), lambda b, i, j: (b, i, 0)),
                mspec(0), mspec(1), mspec(2), vec,
                pl.BlockSpec((None, None, D, tf), lambda b, i, j: (layer, which, 0, j)),
                pl.BlockSpec((None, None, D, tf), lambda b, i, j: (layer, which, 0, j + nj)),
                pl.BlockSpec((None, None, tf, D), lambda b, i, j: (layer, which, j, 0))]
    args = [h, mod, mod, mod, gain.reshape(1, D), w_gu, w_gu, w_down]
    if final_gain is not None:
        in_specs.append(vec)
        args.append(final_gain.reshape(1, D))
    return pl.pallas_call(
        functools.partial(_ffn_kernel, nj=nj, final=final_gain is not None),
        grid=(nb, rows // tm, nj),
        in_specs=in_specs,
        out_specs=pl.BlockSpec((None, tm, D), lambda b, i, j: (b, i, 0)),
        out_shape=jax.ShapeDtypeStruct(h.shape, F32),
        scratch_shapes=[pltpu.VMEM((tm, D), BF)],
        compiler_params=_cp(("parallel", "parallel", "arbitrary")),
        name="ffn",
    )(*args)


def _proj_kernel(h_ref, sh_ref, sc_ref, gain_ref, w_ref, *rest):
    o_ref, u_ref = rest[-2], rest[-1]

    @pl.when(pl.program_id(2) == 0)
    def _():
        u_ref[...] = _modnorm(h_ref[...], gain_ref[...], sh_ref[...], sc_ref[...]).astype(BF)

    o_ref[...] = _dot(u_ref[...], w_ref[...])


def _proj(h, mod, mod_row0, mod_row_step, gain, w, tm, row_block0, prev=None, tn=1152):
    nb, rows, _ = h.shape
    mspec = lambda k: pl.BlockSpec((None, None, 1, D),
                                   lambda b, i, j: (b * mod_row_step + mod_row0, 3 + k, 0, 0))
    in_specs = [pl.BlockSpec((None, tm, D), lambda b, i, j: (b, i, 0)),
                mspec(0), mspec(1),
                pl.BlockSpec((1, D), lambda b, i, j: (0, 0)),
                pl.BlockSpec((D, tn), lambda b, i, j: (0, j))]
    args = [h, mod, mod, gain.reshape(1, D), w]
    aliases = {}
    if prev is not None:
        in_specs.append(pl.BlockSpec(memory_space=pl.ANY))
        args.append(prev)
        aliases = {len(args) - 1: 0}
    return pl.pallas_call(
        _proj_kernel,
        grid=(nb, rows // tm, NP // tn),
        in_specs=in_specs,
        out_specs=pl.BlockSpec((None, tm, tn), lambda b, i, j: (b, i + row_block0, j)),
        out_shape=jax.ShapeDtypeStruct((NB, TT, NP), F32),
        scratch_shapes=[pltpu.VMEM((tm, D), BF)],
        input_output_aliases=aliases,
        compiler_params=_cp(("parallel", "parallel", "arbitrary")),
        name="in_proj",
    )(*args)


def _conv_kernel(x_ref, w_ref, o_ref, pad_ref):
    j = pl.program_id(1)
    half = CONV_K // 2
    for r0, n in ((0, T), (T, TC)):
        pad_ref[0:8, :] = jnp.zeros((8, HD), F32)
        pad_ref[8:8 + n, :] = x_ref[r0:r0 + n, :]
        pad_ref[8 + n:16 + n, :] = jnp.zeros((8, HD), F32)
        acc = jnp.zeros((n, HD), F32)
        for k in range(CONV_K):
            acc = acc + w_ref[k:k + 1, :] * pad_ref[8 + k - half:8 + k - half + n, :]
        y = _silu(acc)
        yn = y * lax.rsqrt(jnp.sum(y * y, axis=-1, keepdims=True) + EPS)
        o_ref[r0:r0 + n, :] = jnp.where(j < 2 * NH, yn, y)


def _gdn_conv(p, conv_w):
    return pl.pallas_call(
        _conv_kernel,
        grid=(NB, 3 * NH),
        in_specs=[pl.BlockSpec((None, TT, HD), lambda b, j: (b, 0, j)),
                  pl.BlockSpec((CONV_K, HD), lambda b, j: (0, j))],
        out_specs=pl.BlockSpec((None, TT, HD), lambda b, j: (b, 0, j)),
        out_shape=jax.ShapeDtypeStruct((NB, TT, 3 * W_A), F32),
        scratch_shapes=[pltpu.VMEM((T + 16, HD), F32)],
        compiler_params=_cp(("parallel", "parallel")),
        name="gdn_conv",
    )(p, conv_w)


def _split3(x):
    a = x.astype(BF)
    r = x - a.astype(F32)
    b = r.astype(BF)
    return a, b, (r - b.astype(F32)).astype(BF)


def _gdn_kernel(qf_ref, kf_ref, vf_ref, gf_ref, qb_ref, kb_ref, vb_ref, gb_ref, prm_ref, of_ref, ob_ref, s_ref):
    @pl.when(pl.program_id(1) == 0)
    def _():
        s_ref[...] = jnp.zeros_like(s_ref)

    ri = lax.broadcasted_iota(jnp.int32, (CS, CS), 0)
    ci = lax.broadcasted_iota(jnp.int32, (CS, CS), 1)
    eye_f = jnp.where(ri == ci, 1.0, 0.0)
    pair = (ri // 2) == (ci // 2)
    levels = [((ri // (2 * bs)) == (ci // (2 * bs))) & ((ri // bs) != (ci // bs)) for bs in (2, 4, 8, 16, 32)]
    prm = prm_ref[...]
    a_neg = -jnp.exp(prm[0:1, :])
    dt_b = prm[1:2, :]

    ch = []
    for d, refs in enumerate(((qf_ref, kf_ref, vf_ref, gf_ref, of_ref), (qb_ref, kb_ref, vb_ref, gb_ref, ob_ref))):
        q_ref, k_ref, v_ref, g_ref, o_ref = refs
        dm = (ri - ci) if d == 0 else (ci - ri)
        incl = dm >= 0
        strict = dm > 0
        tri = jnp.where(incl, 1.0, 0.0).astype(BF)
        tri_t = jnp.where(dm <= 0, 1.0, 0.0).astype(BF)
        gts = g_ref[...]
        x = gts + dt_b
        glog = a_neg * (jnp.maximum(x, 0.0) + jnp.log(1.0 + jnp.exp(-jnp.abs(x))))
        beta = jax.nn.sigmoid(gts)
        parts = _split3(glog)
        gc = sum(_dot(tri, g) for g in parts)
        gc_t = sum(_dot_tn(g, tri_t) for g in parts)
        last = CS - 1 if d == 0 else 0
        g_last = gc[last:last + 1, :]
        eg = jnp.exp(gc)
        ek = jnp.exp(g_last - gc)
        g_tot = jnp.exp(g_last)
        for h in range(NH):
            c = d * NH + h
            ch.append(dict(sl=slice(h * HD, (h + 1) * HD), c=c, incl=incl, strict=strict, o_ref=o_ref,
                           q_ref=q_ref, k_ref=k_ref, v_ref=v_ref,
                           b_c=beta[:, 2 * NH + c:2 * NH + c + 1], gc_c=gc[:, c:c + 1], gc_r=gc_t[c:c + 1, :],
                           eg=eg[:, c:c + 1], ek=ek[:, c:c + 1], g_tot=g_tot[:, c:c + 1]))

    for x in ch:
        kh = x["k_ref"][:, x["sl"]]
        qh = x["q_ref"][:, x["sl"]] * (HD ** -0.5)
        x["kh"], x["qh"] = kh, qh
        kb = kh.astype(BF)
        x["kk"] = _dot_nt(kb, kb)
        x["qk"] = _dot_nt(qh.astype(BF), kb)
    for x in ch:
        decay = jnp.exp(jnp.where(x["incl"], x["gc_c"] - x["gc_r"], -1e30))
        lower = jnp.where(x["strict"], x["b_c"] * x["kk"] * decay, 0.0)
        x["qk"] = (x["qk"] * decay).astype(BF)
        x["lower"] = lower.astype(BF)
        x["t"] = eye_f - jnp.where(pair, lower, 0.0)
    for off in levels:
        for x in ch:
            x["tb"] = x["t"].astype(BF)
            x["mt"] = _dot(jnp.where(off, x["lower"], jnp.zeros_like(x["lower"])), x["tb"]).astype(BF)
        for x in ch:
            x["t"] = x["t"] - _dot(x["tb"], x["mt"])
    for x in ch:
        vh = x["v_ref"][:, x["sl"]]
        rhs = jnp.concatenate([vh * x["b_c"], x["kh"] * (x["b_c"] * x["eg"])], axis=1).astype(BF)
        x["uw"] = _dot(x["t"].astype(BF), rhs)
        x["q_dec"] = (x["qh"] * x["eg"]).astype(BF)
        x["k_dec"] = (x["kh"] * x["ek"]).astype(BF)
    for x in ch:
        st = s_ref[x["c"]]
        x["st"] = st
        stb = st.astype(BF)
        x["vn"] = (x["uw"][:, :HD] - _dot(x["uw"][:, HD:].astype(BF), stb)).astype(BF)
        x["qs"] = _dot(x["q_dec"], stb)
    for x in ch:
        x["o_ref"][:, x["sl"]] = x["qs"] + _dot(x["qk"], x["vn"])
        s_ref[x["c"]] = x["st"] * x["g_tot"] + _dot_tn(x["k_dec"], x["vn"])


def _gdn_scan(qkv, p, prm):
    chunk_f = lambda s: (s + NCH_LAT) % NCH
    chunk_b = lambda s: NCH - 1 - s

    def specs(chunk):
        qkv_spec = lambda c: pl.BlockSpec((None, CS, W_A), lambda b, s: (b, chunk(s), c))
        return [qkv_spec(0), qkv_spec(1), qkv_spec(2),
                pl.BlockSpec((None, CS, 128), lambda b, s: (b, chunk(s), COL_G // 128))]

    return pl.pallas_call(
        _gdn_kernel,
        grid=(NB, NCH),
        in_specs=specs(chunk_f) + specs(chunk_b) + [pl.BlockSpec((8, 128), lambda b, s: (0, 0))],
        out_specs=[pl.BlockSpec((None, CS, W_A), lambda b, s: (b, chunk_f(s), 0)),
                   pl.BlockSpec((None, CS, W_A), lambda b, s: (b, chunk_b(s), 0))],
        out_shape=[jax.ShapeDtypeStruct((NB, TT, W_A), F32)] * 2,
        scratch_shapes=[pltpu.VMEM((2 * NH, HD, HD), F32)],
        compiler_params=_cp(("parallel", "arbitrary")),
        name="gdn_scan",
    )(qkv, qkv, qkv, p, qkv, qkv, qkv, p, prm)


def _rope_kernel(x_ref, g_ref, cos_ref, sin_ref, o_ref):
    x = x_ref[...]
    y = x * lax.rsqrt(jnp.mean(x * x, axis=-1, keepdims=True) + EPS) * g_ref[...]
    lane = lax.broadcasted_iota(jnp.int32, y.shape, 1)
    partner = jnp.where((lane % 64) < 32, pltpu.roll(y, 128 - 32, axis=1), pltpu.roll(y, 32, axis=1))
    o_ref[...] = (y * cos_ref[...] + partner * sin_ref[...]).astype(o_ref.dtype)


def _rope_norm(p, gains, cos_t, sin_t):
    nh = NH + KVH
    return pl.pallas_call(
        _rope_kernel,
        grid=(NB, nh),
        in_specs=[pl.BlockSpec((None, TT, HD), lambda b, j: (b, 0, COL_QB // HD + j)),
                  pl.BlockSpec((None, 1, HD), lambda b, j: (j, 0, 0)),
                  pl.BlockSpec((TT, HD), lambda b, j: (0, 0)),
                  pl.BlockSpec((TT, HD), lambda b, j: (0, 0))],
        out_specs=pl.BlockSpec((None, TT, HD), lambda b, j: (b, 0, j)),
        out_shape=jax.ShapeDtypeStruct((NB, TT, nh * HD), BF),
        compiler_params=_cp(("parallel", "parallel")),
        name="rope_norm",
    )(p, gains, cos_t, sin_t)


def _attn_kernel(q_ref, k_ref, v_ref, o_ref):
    k = k_ref[...]
    v = v_ref[...].astype(BF)
    for g in range(GQ):
        sl = slice(g * HD, (g + 1) * HD)
        s = _dot_nt(q_ref[:, sl], k)
        pr = jnp.exp(s - jnp.max(s, axis=-1, keepdims=True))
        den = jnp.sum(pr, axis=-1, keepdims=True)
        o_ref[:, sl] = (_dot(pr.astype(BF), v) / den).astype(o_ref.dtype)


def _attention(qk, p, tq=512):
    return pl.pallas_call(
        _attn_kernel,
        grid=(NB, KVH, T // tq),
        in_specs=[pl.BlockSpec((None, tq, GQ * HD), lambda b, kv, i: (b, i, kv)),
                  pl.BlockSpec((None, TT, HD), lambda b, kv, i: (b, 0, NH + kv)),
                  pl.BlockSpec((None, TT, HD), lambda b, kv, i: (b, 0, COL_VB // HD + kv))],
        out_specs=pl.BlockSpec((None, tq, GQ * HD), lambda b, kv, i: (b, i, kv)),
        out_shape=jax.ShapeDtypeStruct((NB, T, W_A), BF),
        compiler_params=_cp(("parallel", "parallel", "arbitrary")),
        name="attention",
    )(qk, qk, p)


def _mixout_kernel(of_ref, ob_ref, z_ref, a_ref, h_ref, gt_ref, gain_ref, w_ref, o_ref):
    parts = []
    for hd in range(NH):
        sl = slice(hd * HD, (hd + 1) * HD)
        o = of_ref[:, sl] + ob_ref[:, sl]
        n = o * lax.rsqrt(jnp.mean(o * o, axis=-1, keepdims=True) + EPS) * gain_ref[...]
        parts.append((n * _silu(z_ref[:, sl])).astype(BF))
    gd = jnp.concatenate(parts, axis=1)
    y = _dot(gd, w_ref[0:W_A, :]) + _dot(a_ref[...], w_ref[W_A:2 * W_A, :])
    o_ref[...] = h_ref[...] + gt_ref[...] * y


def _mixout(o_f, o_b, p, attn, h, mod, gain, w_out, tm=256):
    return pl.pallas_call(
        _mixout_kernel,
        grid=(NB, T // tm),
        in_specs=[pl.BlockSpec((None, tm, W_A), lambda b, i: (b, i, 0)),
                  pl.BlockSpec((None, tm, W_A), lambda b, i: (b, i, 0)),
                  pl.BlockSpec((None, tm, W_A), lambda b, i: (b, i, COL_Z // W_A)),
                  pl.BlockSpec((None, tm, W_A), lambda b, i: (b, i, 0)),
                  pl.BlockSpec((None, tm, D), lambda b, i: (b, i, 0)),
                  pl.BlockSpec((None, None, 1, D), lambda b, i: (b, 5, 0, 0)),
                  pl.BlockSpec((1, HD), lambda b, i: (0, 0)),
                  pl.BlockSpec((D, D), lambda b, i: (0, 0))],
        out_specs=pl.BlockSpec((None, tm, D), lambda b, i: (b, i, 0)),
        out_shape=jax.ShapeDtypeStruct((NB, T, D), F32),
        compiler_params=_cp(("parallel", "parallel")),
        name="mix_out",
    )(o_f, o_b, p, attn, h, mod, gain.reshape(1, HD), w_out)


def _dft_tables():
    def cs(n):
        idx = np.arange(n, dtype=np.int64)
        ang = 2.0 * np.pi * ((idx[:, None] * idx[None, :]) % n).astype(np.float64) / n
        return np.cos(ang) / math.sqrt(n), np.sin(ang) / math.sqrt(n)

    cc, sc = cs(GC)
    ct, st = cs(T)
    ch = np.concatenate([cc, sc], axis=1).astype(np.float32)
    tm = np.concatenate([ct, -st], axis=1).astype(np.float32)
    return jnp.asarray(ch, dtype=BF), jnp.asarray(tm, dtype=BF)


def _fchan_kernel(h_ref, sh_ref, sc_ref, gain_ref, w_ref, o_ref):
    u = _modnorm(h_ref[...], gain_ref[...], sh_ref[...], sc_ref[...]).astype(BF)
    w = w_ref[...]
    for g in range(FG):
        sl = slice(g * GC, (g + 1) * GC)
        ab = _dot(u[:, sl], w)
        o_ref[0, :, sl] = ab[:, :GC].astype(o_ref.dtype)
        o_ref[1, :, sl] = ab[:, GC:].astype(o_ref.dtype)


def _fourier_channels(h, mod, gain, ch_tab, tm=512):
    return pl.pallas_call(
        _fchan_kernel,
        grid=(NB, T // tm),
        in_specs=[pl.BlockSpec((None, tm, D), lambda b, i: (b, i, 0)),
                  pl.BlockSpec((None, None, 1, D), lambda b, i: (b, 3, 0, 0)),
                  pl.BlockSpec((None, None, 1, D), lambda b, i: (b, 4, 0, 0)),
                  pl.BlockSpec((1, D), lambda b, i: (0, 0)),
                  pl.BlockSpec((GC, 2 * GC), lambda b, i: (0, 0))],
        out_specs=pl.BlockSpec((None, 2, tm, D), lambda b, i: (b, 0, i, 0)),
        out_shape=jax.ShapeDtypeStruct((NB, 2, T, D), BF),
        compiler_params=_cp(("parallel", "parallel")),
        name="fourier_channels",
    )(h, mod, mod, gain.reshape(1, D), ch_tab)


def _mm_kernel(a_ref, b_ref, o_ref):
    o_ref[...] = _dot(a_ref[...], b_ref[...]).astype(o_ref.dtype)


def _fourier_time(tm_tab, ab, tm=1024, tn=1024):
    return pl.pallas_call(
        _mm_kernel,
        grid=(NB, D // tn, T // tm),
        in_specs=[pl.BlockSpec((tm, 2 * T), lambda b, j, i: (i, 0)),
                  pl.BlockSpec((None, 2 * T, tn), lambda b, j, i: (b, 0, j))],
        out_specs=pl.BlockSpec((None, tm, tn), lambda b, j, i: (b, i, j)),
        out_shape=jax.ShapeDtypeStruct((NB, T, D), BF),
        compiler_params=_cp(("parallel", "parallel", "arbitrary")),
        name="fourier_time",
    )(tm_tab, ab)


def _resmm_kernel(y_ref, w_ref, h_ref, gt_ref, o_ref):
    o_ref[...] = h_ref[...] + gt_ref[...] * _dot(y_ref[...], w_ref[...])


def _fourier_out(y, w_f, h, mod, tm=512):
    return pl.pallas_call(
        _resmm_kernel,
        grid=(NB, T // tm),
        in_specs=[pl.BlockSpec((None, tm, D), lambda b, i: (b, i, 0)),
                  pl.BlockSpec((D, D), lambda b, i: (0, 0)),
                  pl.BlockSpec((None, tm, D), lambda b, i: (b, i, 0)),
                  pl.BlockSpec((None, None, 1, D), lambda b, i: (b, 5, 0, 0))],
        out_specs=pl.BlockSpec((None, tm, D), lambda b, i: (b, i, 0)),
        out_shape=jax.ShapeDtypeStruct((NB, T, D), F32),
        compiler_params=_cp(("parallel", "parallel")),
        name="fourier_out",
    )(y, w_f, h, mod)


def _rope_tables():
    half = HD // 4
    inv = ROPE_THETA ** (-np.arange(half, dtype=np.float64) / half)
    t = np.arange(T)
    ang_r = (t // GRID_W)[:, None] * inv[None, :]
    ang_c = (t % GRID_W)[:, None] * inv[None, :]
    cos = np.concatenate([np.cos(ang_r)] * 2 + [np.cos(ang_c)] * 2, axis=1)
    sin = np.concatenate([-np.sin(ang_r), np.sin(ang_r), -np.sin(ang_c), np.sin(ang_c)], axis=1)
    cos = np.concatenate([cos, np.ones((TC, HD))], axis=0).astype(np.float32)
    sin = np.concatenate([sin, np.zeros((TC, HD))], axis=0).astype(np.float32)
    return jnp.asarray(cos), jnp.asarray(sin)


def kernel(x, c, ctx, c_ctx, w_mod, b_mod, norm_w, ffn_w_gu, ffn_w_down, w_in, conv_w, a_log, dt_bias,
           gdn_norm, q_norm, k_norm, w_out, w_fourier, final_norm):
    cvec = jnp.concatenate([c, c_ctx[None, :], jnp.zeros((8 - NB - 1, D), F32)], axis=0)
    mod = _modulation(cvec, w_mod, b_mod).reshape(DEPTH, 8, N_MOD, 1, D)
    w_gu = ffn_w_gu.astype(BF)
    w_dn = ffn_w_down.astype(BF)

    m0 = mod[0]
    h = _ffn(x, m0, 0, 0, norm_w[0, 0], w_gu, w_dn, 0, 0)
    h_ctx = _ffn(ctx.reshape(1, NB * TC, D), m0, 0, NB, norm_w[0, 0], w_gu, w_dn, 0, 0)
    h_ctx = h_ctx.reshape(NB, TC, D)

    wi = w_in[0]
    w_p = jnp.concatenate([wi[:, :4 * W_A], wi[:, 4 * W_A + 4 * NH:], wi[:, 4 * W_A:4 * W_A + 4 * NH],
                           jnp.zeros((D, 128 - 4 * NH), F32)], axis=1).astype(BF)
    p = _proj(h, m0, 0, 1, norm_w[0, 1], w_p, tm=512, row_block0=0)
    p = _proj(h_ctx, m0, NB, 0, norm_w[0, 1], w_p, tm=TC, row_block0=T // TC, prev=p)

    qkv = _gdn_conv(p, conv_w[0])
    prm = jnp.zeros((8, 128), F32)
    prm = prm.at[0, :2 * NH].set(a_log[0].reshape(-1)).at[1, :2 * NH].set(dt_bias[0].reshape(-1))
    o_f, o_b = _gdn_scan(qkv, p, prm)

    gains = jnp.concatenate([jnp.broadcast_to(q_norm[0] * (HD ** -0.5), (NH, HD)),
                             jnp.broadcast_to(k_norm[0], (KVH, HD))], axis=0).reshape(NH + KVH, 1, HD)
    cos_t, sin_t = _rope_tables()
    qk = _rope_norm(p, gains, cos_t, sin_t)
    attn = _attention(qk, p)

    h = _mixout(o_f, o_b, p, attn, h, m0, gdn_norm[0], w_out[0].astype(BF))
    h = _ffn(h, m0, 6, 0, norm_w[0, 2], w_gu, w_dn, 0, 1)

    m1 = mod[1]
    h = _ffn(h, m1, 0, 0, norm_w[1, 0], w_gu, w_dn, 1, 0)
    ch_tab, tm_tab = _dft_tables()
    ab = _fourier_channels(h, m1, norm_w[1, 1], ch_tab)
    y = _fourier_time(tm_tab, ab.reshape(NB, 2 * T, D))
    h = _fourier_out(y, w_fourier[0].astype(BF), h, m1)
    h = _ffn(h, m1, 6, 0, norm_w[1, 2], w_gu, w_dn, 1, 1, final_gain=final_norm)
    return h
```

```python
import functools
import math

import numpy as np
import jax
import jax.numpy as jnp
from jax import lax
from jax.experimental import pallas as pl
from jax.experimental.pallas import tpu as pltpu

D = 2048
NB = 4
T = 2048
GRID_W = 64
TC = 256
TT = T + TC
DEPTH = 2
EPS = 1e-6
N_MOD = 9
FH = 5632
NH = 8
HD = 128
KVH = 2
GQ = NH // KVH
CONV_K = 5
CS = 64
NCH = TT // CS
NCH_LAT = T // CS
ROPE_THETA = 10000.0
FG = 4
GC = D // FG
W_A = NH * HD
NP = 4 * W_A + W_A + 2 * KVH * HD + 128
COL_Z = 3 * W_A
COL_QB = 4 * W_A
COL_KB = COL_QB + W_A
COL_VB = COL_KB + KVH * HD
COL_G = COL_VB + KVH * HD

VMEM_LIMIT = 52 * 1024 * 1024
BF = jnp.bfloat16
F32 = jnp.float32


def _cp(sem):
    return pltpu.CompilerParams(dimension_semantics=sem, vmem_limit_bytes=VMEM_LIMIT)


def _dot(a, b):
    return jnp.dot(a, b, preferred_element_type=F32)


def _dot_nt(a, b):
    return lax.dot_general(a, b, (((1,), (1,)), ((), ())), preferred_element_type=F32)


def _dot_tn(a, b):
    return lax.dot_general(a, b, (((0,), (0,)), ((), ())), preferred_element_type=F32)


def _silu(x):
    return x * jax.nn.sigmoid(x)


def _modnorm(x, gain, shift, scale):
    return x * lax.rsqrt(jnp.mean(x * x, axis=-1, keepdims=True) + EPS) * (gain * (1.0 + scale)) + shift


def _mod_kernel(c_ref, w_ref, b_ref, o_ref):
    a = _silu(c_ref[...]).astype(BF)
    o_ref[...] = _dot(a, w_ref[...].astype(BF)) + b_ref[...]


def _modulation(cvec, w_mod, b_mod):
    tn = 1024
    n = N_MOD * D
    return pl.pallas_call(
        _mod_kernel,
        grid=(DEPTH, n // tn),
        in_specs=[pl.BlockSpec((8, D), lambda l, j: (0, 0)),
                  pl.BlockSpec((None, D, tn), lambda l, j: (l, 0, j)),
                  pl.BlockSpec((None, 1, tn), lambda l, j: (l, 0, j))],
        out_specs=pl.BlockSpec((None, 8, tn), lambda l, j: (l, 0, j)),
        out_shape=jax.ShapeDtypeStruct((DEPTH, 8, n), F32),
        compiler_params=_cp(("parallel", "arbitrary")),
        name="modulation",
    )(cvec, w_mod, b_mod.reshape(DEPTH, 1, n))


def _ffn_kernel(h_ref, sh_ref, sc_ref, gt_ref, gain_ref, wg_ref, wu_ref, wd_ref, *rest, nj, final, convert):
    rest = list(rest)
    fin_ref = rest.pop(0) if final else None
    if convert:
        ngu_ref, ndn_ref = rest.pop(0), rest.pop(0)
    o_ref = rest.pop(0)
    if convert:
        ogu_ref, odn_ref = rest.pop(0), rest.pop(0)
    u_ref = rest.pop(0)
    j = pl.program_id(2)

    @pl.when(j == 0)
    def _():
        u = _modnorm(h_ref[...], gain_ref[...], sh_ref[...], sc_ref[...])
        u_ref[...] = u.astype(BF)
        o_ref[...] = jnp.zeros_like(o_ref)

    u = u_ref[...]
    g = _dot(u, wg_ref[...])
    up = _dot(u, wu_ref[...])
    act = (_silu(g) * up).astype(BF)
    o_ref[...] += _dot(act, wd_ref[...])
    if convert:
        ogu_ref[...] = ngu_ref[...].astype(BF)
        odn_ref[...] = ndn_ref[...].astype(BF)

    @pl.when(j == nj - 1)
    def _():
        r = h_ref[...] + 0.5 * gt_ref[...] * o_ref[...]
        if final:
            r = r * lax.rsqrt(jnp.mean(r * r, axis=-1, keepdims=True) + EPS) * fin_ref[...]
        o_ref[...] = r


def _ffn(h, mod, base, mod_row0, gain, w_gu, w_down, final_gain=None, nxt=None, tm=512, tf=512):
    nb, rows, _ = h.shape
    nj = FH // tf
    ni = rows // tm
    mspec = lambda k: pl.BlockSpec((None, None, 1, D), lambda b, i, j: (b + mod_row0, base + k, 0, 0))
    vec = pl.BlockSpec((1, D), lambda b, i, j: (0, 0))
    in_specs = [pl.BlockSpec((None, tm, D), lambda b, i, j: (b, i, 0)),
                mspec(0), mspec(1), mspec(2), vec,
                pl.BlockSpec((D, tf), lambda b, i, j: (0, j)),
                pl.BlockSpec((D, tf), lambda b, i, j: (0, j + nj)),
                pl.BlockSpec((tf, D), lambda b, i, j: (j, 0))]
    args = [h, mod, mod, mod, gain.reshape(1, D), w_gu, w_gu, w_down]
    out_specs = [pl.BlockSpec((None, tm, D), lambda b, i, j: (b, i, 0))]
    out_shape = [jax.ShapeDtypeStruct(h.shape, F32)]
    if final_gain is not None:
        in_specs.append(vec)
        args.append(final_gain.reshape(1, D))
    if nxt is not None:
        n_gu, n_dn, layer, which = nxt
        nr = nb * ni
        gu_blk = (D // nr, 2 * FH // nj)
        dn_blk = (FH // (nr * nj), D)
        gu_idx = lambda b, i, j: (b * ni + i, j)
        dn_idx = lambda b, i, j: ((b * ni + i) * nj + j, 0)
        in_specs += [pl.BlockSpec((None, None) + gu_blk, lambda b, i, j: (layer, which) + gu_idx(b, i, j)),
                     pl.BlockSpec((None, None) + dn_blk, lambda b, i, j: (layer, which) + dn_idx(b, i, j))]
        args += [n_gu, n_dn]
        out_specs += [pl.BlockSpec(gu_blk, gu_idx), pl.BlockSpec(dn_blk, dn_idx)]
        out_shape += [jax.ShapeDtypeStruct((D, 2 * FH), BF), jax.ShapeDtypeStruct((FH, D), BF)]
    out = pl.pallas_call(
        functools.partial(_ffn_kernel, nj=nj, final=final_gain is not None, convert=nxt is not None),
        grid=(nb, ni, nj),
        in_specs=in_specs,
        out_specs=out_specs,
        out_shape=out_shape,
        scratch_shapes=[pltpu.VMEM((tm, D), BF)],
        compiler_params=_cp(("parallel", "parallel", "arbitrary")),
        name="ffn",
    )(*args)
    return out if nxt is not None else out[0]


def _prenorm_kernel(hl_ref, hc_ref, shl_ref, scl_ref, shc_ref, scc_ref, gain_ref, o_ref):
    i = pl.program_id(1)

    @pl.when(i < T // TC)
    def _():
        o_ref[...] = _modnorm(hl_ref[...], gain_ref[...], shl_ref[...], scl_ref[...]).astype(BF)

    @pl.when(i >= T // TC)
    def _():
        o_ref[...] = _modnorm(hc_ref[...], gain_ref[...], shc_ref[...], scc_ref[...]).astype(BF)


def _prenorm(h, h_ctx, mod, gain):
    nlat = T // TC
    mspec = lambda row, k: pl.BlockSpec((None, None, 1, D), lambda b, i: (row(b), 3 + k, 0, 0))
    lat = lambda b: b
    cx = lambda b: NB
    return pl.pallas_call(
        _prenorm_kernel,
        grid=(NB, TT // TC),
        in_specs=[pl.BlockSpec((None, TC, D), lambda b, i: (b, jnp.minimum(i, nlat - 1), 0)),
                  pl.BlockSpec((None, TC, D), lambda b, i: (b, 0, 0)),
                  mspec(lat, 0), mspec(lat, 1), mspec(cx, 0), mspec(cx, 1),
                  pl.BlockSpec((1, D), lambda b, i: (0, 0))],
        out_specs=pl.BlockSpec((None, TC, D), lambda b, i: (b, i, 0)),
        out_shape=jax.ShapeDtypeStruct((NB, TT, D), BF),
        compiler_params=_cp(("parallel", "arbitrary")),
        name="prenorm",
    )(h, h_ctx, mod, mod, mod, mod, gain.reshape(1, D))


def _mm_kernel(a_ref, b_ref, o_ref):
    o_ref[...] = _dot(a_ref[...], b_ref[...]).astype(o_ref.dtype)


def _matmul(a, w, tm, tn, out_dtype, name):
    m, k = a.shape
    n = w.shape[1]
    return pl.pallas_call(
        _mm_kernel,
        grid=(m // tm, n // tn),
        in_specs=[pl.BlockSpec((tm, k), lambda i, j: (i, 0)),
                  pl.BlockSpec((k, tn), lambda i, j: (0, j))],
        out_specs=pl.BlockSpec((tm, tn), lambda i, j: (i, j)),
        out_shape=jax.ShapeDtypeStruct((m, n), out_dtype),
        compiler_params=_cp(("parallel", "arbitrary")),
        name=name,
    )(a, w)


def _conv_kernel(x_ref, w_ref, o_ref, pad_ref):
    j = pl.program_id(1)
    half = CONV_K // 2
    for r0, n in ((0, T), (T, TC)):
        pad_ref[0:8, :] = jnp.zeros((8, HD), F32)
        pad_ref[8:8 + n, :] = x_ref[r0:r0 + n, :]
        pad_ref[8 + n:16 + n, :] = jnp.zeros((8, HD), F32)
        acc = jnp.zeros((n, HD), F32)
        for k in range(CONV_K):
            acc = acc + w_ref[k:k + 1, :] * pad_ref[8 + k - half:8 + k - half + n, :]
        y = _silu(acc)
        yn = y * lax.rsqrt(jnp.sum(y * y, axis=-1, keepdims=True) + EPS)
        o_ref[r0:r0 + n, :] = jnp.where(j < 2 * NH, yn, y)


def _gdn_conv(p, conv_w):
    return pl.pallas_call(
        _conv_kernel,
        grid=(NB, 3 * NH),
        in_specs=[pl.BlockSpec((None, TT, HD), lambda b, j: (b, 0, j)),
                  pl.BlockSpec((CONV_K, HD), lambda b, j: (0, j))],
        out_specs=pl.BlockSpec((None, TT, HD), lambda b, j: (b, 0, j)),
        out_shape=jax.ShapeDtypeStruct((NB, TT, 3 * W_A), F32),
        scratch_shapes=[pltpu.VMEM((T + 16, HD), F32)],
        compiler_params=_cp(("parallel", "parallel")),
        name="gdn_conv",
    )(p, conv_w)


def _split3(x):
    a = x.astype(BF)
    r = x - a.astype(F32)
    b = r.astype(BF)
    return a, b, (r - b.astype(F32)).astype(BF)


def _gdn_kernel(qf_ref, kf_ref, vf_ref, gf_ref, qb_ref, kb_ref, vb_ref, gb_ref, prm_ref, of_ref, ob_ref, s_ref):
    @pl.when(pl.program_id(1) == 0)
    def _():
        s_ref[...] = jnp.zeros_like(s_ref)

    ri = lax.broadcasted_iota(jnp.int32, (CS, CS), 0)
    ci = lax.broadcasted_iota(jnp.int32, (CS, CS), 1)
    eye_f = jnp.where(ri == ci, 1.0, 0.0)
    pair = (ri // 2) == (ci // 2)
    levels = [((ri // (2 * bs)) == (ci // (2 * bs))) & ((ri // bs) != (ci // bs)) for bs in (2, 4, 8, 16, 32)]
    prm = prm_ref[...]
    a_neg = -jnp.exp(prm[0:1, :])
    dt_b = prm[1:2, :]

    ch = []
    for d, refs in enumerate(((qf_ref, kf_ref, vf_ref, gf_ref, of_ref), (qb_ref, kb_ref, vb_ref, gb_ref, ob_ref))):
        q_ref, k_ref, v_ref, g_ref, o_ref = refs
        dm = (ri - ci) if d == 0 else (ci - ri)
        incl = dm >= 0
        strict = dm > 0
        tri = jnp.where(incl, 1.0, 0.0).astype(BF)
        tri_t = jnp.where(dm <= 0, 1.0, 0.0).astype(BF)
        gts = g_ref[...]
        x = gts + dt_b
        glog = a_neg * (jnp.maximum(x, 0.0) + jnp.log(1.0 + jnp.exp(-jnp.abs(x))))
        beta = jax.nn.sigmoid(gts)
        parts = _split3(glog)
        gc = sum(_dot(tri, g) for g in parts)
        gc_t = sum(_dot_tn(g, tri_t) for g in parts)
        last = CS - 1 if d == 0 else 0
        g_last = gc[last:last + 1, :]
        eg = jnp.exp(gc)
        ek = jnp.exp(g_last - gc)
        g_tot = jnp.exp(g_last)
        for h in range(NH):
            c = d * NH + h
            ch.append(dict(sl=slice(h * HD, (h + 1) * HD), c=c, incl=incl, strict=strict, o_ref=o_ref,
                           q_ref=q_ref, k_ref=k_ref, v_ref=v_ref,
                           b_c=beta[:, 2 * NH + c:2 * NH + c + 1], gc_c=gc[:, c:c + 1], gc_r=gc_t[c:c + 1, :],
                           eg=eg[:, c:c + 1], ek=ek[:, c:c + 1], g_tot=g_tot[:, c:c + 1]))

    for x in ch:
        kh = x["k_ref"][:, x["sl"]]
        qh = x["q_ref"][:, x["sl"]] * (HD ** -0.5)
        x["kh"], x["qh"] = kh, qh
        kb = kh.astype(BF)
        x["kk"] = _dot_nt(kb, kb)
        x["qk"] = _dot_nt(qh.astype(BF), kb)
    for x in ch:
        decay = jnp.exp(jnp.where(x["incl"], x["gc_c"] - x["gc_r"], -1e30))
        lower = jnp.where(x["strict"], x["b_c"] * x["kk"] * decay, 0.0)
        x["qk"] = (x["qk"] * decay).astype(BF)
        x["lower"] = lower.astype(BF)
        x["t"] = eye_f - jnp.where(pair, lower, 0.0)
    for off in levels:
        for x in ch:
            x["tb"] = x["t"].astype(BF)
            x["mt"] = _dot(jnp.where(off, x["lower"], jnp.zeros_like(x["lower"])), x["tb"]).astype(BF)
        for x in ch:
            x["t"] = x["t"] - _dot(x["tb"], x["mt"])
    for x in ch:
        vh = x["v_ref"][:, x["sl"]]
        rhs = jnp.concatenate([vh * x["b_c"], x["kh"] * (x["b_c"] * x["eg"])], axis=1).astype(BF)
        x["uw"] = _dot(x["t"].astype(BF), rhs)
        x["q_dec"] = (x["qh"] * x["eg"]).astype(BF)
        x["k_dec"] = (x["kh"] * x["ek"]).astype(BF)
    for x in ch:
        st = s_ref[x["c"]]
        x["st"] = st
        stb = st.astype(BF)
        x["vn"] = (x["uw"][:, :HD] - _dot(x["uw"][:, HD:].astype(BF), stb)).astype(BF)
        x["qs"] = _dot(x["q_dec"], stb)
    for x in ch:
        x["o_ref"][:, x["sl"]] = x["qs"] + _dot(x["qk"], x["vn"])
        s_ref[x["c"]] = x["st"] * x["g_tot"] + _dot_tn(x["k_dec"], x["vn"])


def _gdn_scan(qkv, p, prm):
    chunk_f = lambda s: (s + NCH_LAT) % NCH
    chunk_b = lambda s: NCH - 1 - s

    def specs(chunk):
        qkv_spec = lambda c: pl.BlockSpec((None, CS, W_A), lambda b, s: (b, chunk(s), c))
        return [qkv_spec(0), qkv_spec(1), qkv_spec(2),
                pl.BlockSpec((None, CS, 128), lambda b, s: (b, chunk(s), COL_G // 128))]

    return pl.pallas_call(
        _gdn_kernel,
        grid=(NB, NCH),
        in_specs=specs(chunk_f) + specs(chunk_b) + [pl.BlockSpec((8, 128), lambda b, s: (0, 0))],
        out_specs=[pl.BlockSpec((None, CS, W_A), lambda b, s: (b, chunk_f(s), 0)),
                   pl.BlockSpec((None, CS, W_A), lambda b, s: (b, chunk_b(s), 0))],
        out_shape=[jax.ShapeDtypeStruct((NB, TT, W_A), F32)] * 2,
        scratch_shapes=[pltpu.VMEM((2 * NH, HD, HD), F32)],
        compiler_params=_cp(("parallel", "arbitrary")),
        name="gdn_scan",
    )(qkv, qkv, qkv, p, qkv, qkv, qkv, p, prm)


def _rope_kernel(x_ref, g_ref, cos_ref, sin_ref, o_ref):
    x = x_ref[...]
    y = x * lax.rsqrt(jnp.mean(x * x, axis=-1, keepdims=True) + EPS) * g_ref[...]
    lane = lax.broadcasted_iota(jnp.int32, y.shape, 1)
    partner = jnp.where((lane % 64) < 32, pltpu.roll(y, 128 - 32, axis=1), pltpu.roll(y, 32, axis=1))
    o_ref[...] = (y * cos_ref[...] + partner * sin_ref[...]).astype(o_ref.dtype)


def _rope_norm(p, gains, cos_t, sin_t):
    nh = NH + KVH
    return pl.pallas_call(
        _rope_kernel,
        grid=(NB, nh),
        in_specs=[pl.BlockSpec((None, TT, HD), lambda b, j: (b, 0, COL_QB // HD + j)),
                  pl.BlockSpec((None, 1, HD), lambda b, j: (j, 0, 0)),
                  pl.BlockSpec((TT, HD), lambda b, j: (0, 0)),
                  pl.BlockSpec((TT, HD), lambda b, j: (0, 0))],
        out_specs=pl.BlockSpec((None, TT, HD), lambda b, j: (b, 0, j)),
        out_shape=jax.ShapeDtypeStruct((NB, TT, nh * HD), BF),
        compiler_params=_cp(("parallel", "parallel")),
        name="rope_norm",
    )(p, gains, cos_t, sin_t)


def _attn_kernel(q_ref, k_ref, v_ref, o_ref):
    k = k_ref[...]
    v = v_ref[...].astype(BF)
    for g in range(GQ):
        sl = slice(g * HD, (g + 1) * HD)
        s = _dot_nt(q_ref[:, sl], k)
        pr = jnp.exp(s - jnp.max(s, axis=-1, keepdims=True))
        den = jnp.sum(pr, axis=-1, keepdims=True)
        o_ref[:, sl] = (_dot(pr.astype(BF), v) / den).astype(o_ref.dtype)


def _attention(qk, p, tq=512):
    return pl.pallas_call(
        _attn_kernel,
        grid=(NB, KVH, T // tq),
        in_specs=[pl.BlockSpec((None, tq, GQ * HD), lambda b, kv, i: (b, i, kv)),
                  pl.BlockSpec((None, TT, HD), lambda b, kv, i: (b, 0, NH + kv)),
                  pl.BlockSpec((None, TT, HD), lambda b, kv, i: (b, 0, COL_VB // HD + kv))],
        out_specs=pl.BlockSpec((None, tq, GQ * HD), lambda b, kv, i: (b, i, kv)),
        out_shape=jax.ShapeDtypeStruct((NB, T, W_A), BF),
        compiler_params=_cp(("parallel", "parallel", "arbitrary")),
        name="attention",
    )(qk, qk, p)


def _mixout_kernel(of_ref, ob_ref, z_ref, a_ref, h_ref, gt_ref, gain_ref, w_ref, o_ref):
    parts = []
    for hd in range(NH):
        sl = slice(hd * HD, (hd + 1) * HD)
        o = of_ref[:, sl] + ob_ref[:, sl]
        n = o * lax.rsqrt(jnp.mean(o * o, axis=-1, keepdims=True) + EPS) * gain_ref[...]
        parts.append((n * _silu(z_ref[:, sl])).astype(BF))
    gd = jnp.concatenate(parts, axis=1)
    y = _dot(gd, w_ref[0:W_A, :]) + _dot(a_ref[...], w_ref[W_A:2 * W_A, :])
    o_ref[...] = h_ref[...] + gt_ref[...] * y


def _mixout(o_f, o_b, p, attn, h, mod, gain, w_out, tm=256):
    return pl.pallas_call(
        _mixout_kernel,
        grid=(NB, T // tm),
        in_specs=[pl.BlockSpec((None, tm, W_A), lambda b, i: (b, i, 0)),
                  pl.BlockSpec((None, tm, W_A), lambda b, i: (b, i, 0)),
                  pl.BlockSpec((None, tm, W_A), lambda b, i: (b, i, COL_Z // W_A)),
                  pl.BlockSpec((None, tm, W_A), lambda b, i: (b, i, 0)),
                  pl.BlockSpec((None, tm, D), lambda b, i: (b, i, 0)),
                  pl.BlockSpec((None, None, 1, D), lambda b, i: (b, 5, 0, 0)),
                  pl.BlockSpec((1, HD), lambda b, i: (0, 0)),
                  pl.BlockSpec((D, D), lambda b, i: (0, 0))],
        out_specs=pl.BlockSpec((None, tm, D), lambda b, i: (b, i, 0)),
        out_shape=jax.ShapeDtypeStruct((NB, T, D), F32),
        compiler_params=_cp(("parallel", "parallel")),
        name="mix_out",
    )(o_f, o_b, p, attn, h, mod, gain.reshape(1, HD), w_out)


def _dft_tables():
    def cs(n):
        idx = np.arange(n, dtype=np.int64)
        ang = 2.0 * np.pi * ((idx[:, None] * idx[None, :]) % n).astype(np.float64) / n
        return np.cos(ang) / math.sqrt(n), np.sin(ang) / math.sqrt(n)

    cc, sc = cs(GC)
    ct, st = cs(T)
    ch = np.concatenate([cc, sc], axis=1).astype(np.float32)
    tm = np.concatenate([ct, -st], axis=1).astype(np.float32)
    return jnp.asarray(ch, dtype=BF), jnp.asarray(tm, dtype=BF)


def _fchan_kernel(h_ref, sh_ref, sc_ref, gain_ref, w_ref, o_ref):
    u = _modnorm(h_ref[...], gain_ref[...], sh_ref[...], sc_ref[...]).astype(BF)
    w = w_ref[...]
    for g in range(FG):
        sl = slice(g * GC, (g + 1) * GC)
        ab = _dot(u[:, sl], w)
        o_ref[0, :, sl] = ab[:, :GC].astype(o_ref.dtype)
        o_ref[1, :, sl] = ab[:, GC:].astype(o_ref.dtype)


def _fourier_channels(h, mod, gain, ch_tab, tm=512):
    return pl.pallas_call(
        _fchan_kernel,
        grid=(NB, T // tm),
        in_specs=[pl.BlockSpec((None, tm, D), lambda b, i: (b, i, 0)),
                  pl.BlockSpec((None, None, 1, D), lambda b, i: (b, 3, 0, 0)),
                  pl.BlockSpec((None, None, 1, D), lambda b, i: (b, 4, 0, 0)),
                  pl.BlockSpec((1, D), lambda b, i: (0, 0)),
                  pl.BlockSpec((GC, 2 * GC), lambda b, i: (0, 0))],
        out_specs=pl.BlockSpec((None, 2, tm, D), lambda b, i: (b, 0, i, 0)),
        out_shape=jax.ShapeDtypeStruct((NB, 2, T, D), BF),
        compiler_params=_cp(("parallel", "parallel")),
        name="fourier_channels",
    )(h, mod, mod, gain.reshape(1, D), ch_tab)


def _fourier_time(tm_tab, ab, tm=1024, tn=1024):
    return pl.pallas_call(
        _mm_kernel,
        grid=(NB, D // tn, T // tm),
        in_specs=[pl.BlockSpec((tm, 2 * T), lambda b, j, i: (i, 0)),
                  pl.BlockSpec((None, 2 * T, tn), lambda b, j, i: (b, 0, j))],
        out_specs=pl.BlockSpec((None, tm, tn), lambda b, j, i: (b, i, j)),
        out_shape=jax.ShapeDtypeStruct((NB, T, D), BF),
        compiler_params=_cp(("parallel", "parallel", "arbitrary")),
        name="fourier_time",
    )(tm_tab, ab)


def _resmm_kernel(y_ref, w_ref, h_ref, gt_ref, o_ref):
    o_ref[...] = h_ref[...] + gt_ref[...] * _dot(y_ref[...], w_ref[...])


def _fourier_out(y, w_f, h, mod, tm=512):
    return pl.pallas_call(
        _resmm_kernel,
        grid=(NB, T // tm),
        in_specs=[pl.BlockSpec((None, tm, D), lambda b, i: (b, i, 0)),
                  pl.BlockSpec((D, D), lambda b, i: (0, 0)),
                  pl.BlockSpec((None, tm, D), lambda b, i: (b, i, 0)),
                  pl.BlockSpec((None, None, 1, D), lambda b, i: (b, 5, 0, 0))],
        out_specs=pl.BlockSpec((None, tm, D), lambda b, i: (b, i, 0)),
        out_shape=jax.ShapeDtypeStruct((NB, T, D), F32),
        compiler_params=_cp(("parallel", "parallel")),
        name="fourier_out",
    )(y, w_f, h, mod)


def _rope_tables():
    half = HD // 4
    inv = ROPE_THETA ** (-np.arange(half, dtype=np.float64) / half)
    t = np.arange(T)
    ang_r = (t // GRID_W)[:, None] * inv[None, :]
    ang_c = (t % GRID_W)[:, None] * inv[None, :]
    cos = np.concatenate([np.cos(ang_r)] * 2 + [np.cos(ang_c)] * 2, axis=1)
    sin = np.concatenate([-np.sin(ang_r), np.sin(ang_r), -np.sin(ang_c), np.sin(ang_c)], axis=1)
    cos = np.concatenate([cos, np.ones((TC, HD))], axis=0).astype(np.float32)
    sin = np.concatenate([sin, np.zeros((TC, HD))], axis=0).astype(np.float32)
    return jnp.asarray(cos), jnp.asarray(sin)


def kernel(x, c, ctx, c_ctx, w_mod, b_mod, norm_w, ffn_w_gu, ffn_w_down, w_in, conv_w, a_log, dt_bias,
           gdn_norm, q_norm, k_norm, w_out, w_fourier, final_norm):
    cvec = jnp.concatenate([c, c_ctx[None, :], jnp.zeros((8 - NB - 1, D), F32)], axis=0)
    mod = _modulation(cvec, w_mod, b_mod).reshape(DEPTH, 8, N_MOD, 1, D)
    w_gu = ffn_w_gu[0, 0].astype(BF)
    w_dn = ffn_w_down[0, 0].astype(BF)

    m0 = mod[0]
    h_ctx = _ffn(ctx.reshape(1, NB * TC, D), m0, 0, NB, norm_w[0, 0], w_gu, w_dn).reshape(NB, TC, D)
    h, w_gu, w_dn = _ffn(x, m0, 0, 0, norm_w[0, 0], w_gu, w_dn, nxt=(ffn_w_gu, ffn_w_down, 0, 1))

    wi = w_in[0]
    w_p = jnp.concatenate([wi[:, :4 * W_A], wi[:, 4 * W_A + 4 * NH:], wi[:, 4 * W_A:4 * W_A + 4 * NH],
                           jnp.zeros((D, 128 - 4 * NH), F32)], axis=1).astype(BF)
    u = _prenorm(h, h_ctx, m0, norm_w[0, 1])
    p = _matmul(u.reshape(NB * TT, D), w_p, 1024, 1920, F32, "in_proj").reshape(NB, TT, NP)

    qkv = _gdn_conv(p, conv_w[0])
    prm = jnp.zeros((8, 128), F32)
    prm = prm.at[0, :2 * NH].set(a_log[0].reshape(-1)).at[1, :2 * NH].set(dt_bias[0].reshape(-1))
    o_f, o_b = _gdn_scan(qkv, p, prm)

    gains = jnp.concatenate([jnp.broadcast_to(q_norm[0] * (HD ** -0.5), (NH, HD)),
                             jnp.broadcast_to(k_norm[0], (KVH, HD))], axis=0).reshape(NH + KVH, 1, HD)
    cos_t, sin_t = _rope_tables()
    qk = _rope_norm(p, gains, cos_t, sin_t)
    attn = _attention(qk, p)

    h = _mixout(o_f, o_b, p, attn, h, m0, gdn_norm[0], w_out[0].astype(BF))
    h, w_gu, w_dn = _ffn(h, m0, 6, 0, norm_w[0, 2], w_gu, w_dn, nxt=(ffn_w_gu, ffn_w_down, 1, 0))

    m1 = mod[1]
    h, w_gu, w_dn = _ffn(h, m1, 0, 0, norm_w[1, 0], w_gu, w_dn, nxt=(ffn_w_gu, ffn_w_down, 1, 1))
    ch_tab, tm_tab = _dft_tables()
    ab = _fourier_channels(h, m1, norm_w[1, 1], ch_tab)
    y = _fourier_time(tm_tab, ab.reshape(NB, 2 * T, D))
    h = _fourier_out(y, w_fourier[0].astype(BF), h, m1)
    return _ffn(h, m1, 6, 0, norm_w[1, 2], w_gu, w_dn, final_gain=final_norm)
```

```python
import functools
import math

import numpy as np
import jax
import jax.numpy as jnp
from jax import lax
from jax.experimental import pallas as pl
from jax.experimental.pallas import tpu as pltpu

D = 2048
NB = 4
T = 2048
GRID_W = 64
TC = 256
TT = T + TC
DEPTH = 2
EPS = 1e-6
N_MOD = 9
FH = 5632
NH = 8
HD = 128
KVH = 2
GQ = NH // KVH
CONV_K = 5
CS = 64
NCH = TT // CS
NCH_LAT = T // CS
ROPE_THETA = 10000.0
FG = 4
GC = D // FG
W_A = NH * HD
NPA = 4 * W_A
COL_Z = 3 * W_A
NPB = W_A + 2 * KVH * HD
COL_KB = W_A
COL_VB = COL_KB + KVH * HD

VMEM_LIMIT = 58 * 1024 * 1024
BF = jnp.bfloat16
F32 = jnp.float32


def _cp(sem):
    return pltpu.CompilerParams(dimension_semantics=sem, vmem_limit_bytes=VMEM_LIMIT)


def _dot(a, b):
    return jnp.dot(a, b, preferred_element_type=F32)


def _dot_nt(a, b):
    return lax.dot_general(a, b, (((1,), (1,)), ((), ())), preferred_element_type=F32)


def _dot_tn(a, b):
    return lax.dot_general(a, b, (((0,), (0,)), ((), ())), preferred_element_type=F32)


def _silu(x):
    return x * jax.nn.sigmoid(x)


def _modnorm(x, gain, shift, scale):
    return x * lax.rsqrt(jnp.mean(x * x, axis=-1, keepdims=True) + EPS) * (gain * (1.0 + scale)) + shift


def _mod_kernel(c_ref, w_ref, b_ref, o_ref):
    a = _silu(c_ref[...]).astype(BF)
    o_ref[...] = _dot(a, w_ref[...].astype(BF)) + b_ref[...]


def _modulation(cvec, w_mod, b_mod):
    tn = 1024
    n = N_MOD * D
    return pl.pallas_call(
        _mod_kernel,
        grid=(DEPTH, n // tn),
        in_specs=[pl.BlockSpec((8, D), lambda l, j: (0, 0)),
                  pl.BlockSpec((None, D, tn), lambda l, j: (l, 0, j)),
                  pl.BlockSpec((None, 1, tn), lambda l, j: (l, 0, j))],
        out_specs=pl.BlockSpec((None, 8, tn), lambda l, j: (l, 0, j)),
        out_shape=jax.ShapeDtypeStruct((DEPTH, 8, n), F32),
        compiler_params=_cp(("parallel", "arbitrary")),
        name="modulation",
    )(cvec, w_mod, b_mod.reshape(DEPTH, 1, n))


def _ffn_kernel(h_ref, sh_ref, sc_ref, gt_ref, gain_ref, wg_ref, wu_ref, wd_ref, *rest, nj, final, convert):
    rest = list(rest)
    fin_ref = rest.pop(0) if final else None
    if convert:
        ngu_ref, ndn_ref = rest.pop(0), rest.pop(0)
    o_ref = rest.pop(0)
    if convert:
        ogu_ref, odn_ref = rest.pop(0), rest.pop(0)
    u_ref = rest.pop(0)
    j = pl.program_id(2)

    @pl.when(j == 0)
    def _():
        u = _modnorm(h_ref[...], gain_ref[...], sh_ref[...], sc_ref[...])
        u_ref[...] = u.astype(BF)
        o_ref[...] = jnp.zeros_like(o_ref)

    u = u_ref[...]
    g = _dot(u, wg_ref[...])
    up = _dot(u, wu_ref[...])
    act = (_silu(g) * up).astype(BF)
    o_ref[...] += _dot(act, wd_ref[...])
    if convert:
        ogu_ref[...] = ngu_ref[...].astype(BF)
        odn_ref[...] = ndn_ref[...].astype(BF)

    @pl.when(j == nj - 1)
    def _():
        r = h_ref[...] + 0.5 * gt_ref[...] * o_ref[...]
        if final:
            r = r * lax.rsqrt(jnp.mean(r * r, axis=-1, keepdims=True) + EPS) * fin_ref[...]
        o_ref[...] = r


def _ffn(h, mod, base, mod_row0, gain, w_gu, w_down, final_gain=None, nxt=None, tm=1024, tf=256):
    nb, rows, _ = h.shape
    nj = FH // tf
    ni = rows // tm
    mspec = lambda k: pl.BlockSpec((None, None, 1, D), lambda b, i, j: (b + mod_row0, base + k, 0, 0))
    vec = pl.BlockSpec((1, D), lambda b, i, j: (0, 0))
    in_specs = [pl.BlockSpec((None, tm, D), lambda b, i, j: (b, i, 0)),
                mspec(0), mspec(1), mspec(2), vec,
                pl.BlockSpec((D, tf), lambda b, i, j: (0, j)),
                pl.BlockSpec((D, tf), lambda b, i, j: (0, j + nj)),
                pl.BlockSpec((tf, D), lambda b, i, j: (j, 0))]
    args = [h, mod, mod, mod, gain.reshape(1, D), w_gu, w_gu, w_down]
    out_specs = [pl.BlockSpec((None, tm, D), lambda b, i, j: (b, i, 0))]
    out_shape = [jax.ShapeDtypeStruct(h.shape, F32)]
    if final_gain is not None:
        in_specs.append(vec)
        args.append(final_gain.reshape(1, D))
    if nxt is not None:
        n_gu, n_dn, layer, which = nxt
        nr = nb * ni
        gu_blk = (D // nr, 2 * FH // nj)
        dn_blk = (FH // (nr * nj), D)
        gu_idx = lambda b, i, j: (b * ni + i, j)
        dn_idx = lambda b, i, j: ((b * ni + i) * nj + j, 0)
        in_specs += [pl.BlockSpec((None, None) + gu_blk, lambda b, i, j: (layer, which) + gu_idx(b, i, j)),
                     pl.BlockSpec((None, None) + dn_blk, lambda b, i, j: (layer, which) + dn_idx(b, i, j))]
        args += [n_gu, n_dn]
        out_specs += [pl.BlockSpec(gu_blk, gu_idx), pl.BlockSpec(dn_blk, dn_idx)]
        out_shape += [jax.ShapeDtypeStruct((D, 2 * FH), BF), jax.ShapeDtypeStruct((FH, D), BF)]
    out = pl.pallas_call(
        functools.partial(_ffn_kernel, nj=nj, final=final_gain is not None, convert=nxt is not None),
        grid=(nb, ni, nj),
        in_specs=in_specs,
        out_specs=out_specs,
        out_shape=out_shape,
        scratch_shapes=[pltpu.VMEM((tm, D), BF)],
        compiler_params=_cp(("parallel", "parallel", "arbitrary")),
        name="ffn",
    )(*args)
    return out if nxt is not None else out[0]


def _prenorm_kernel(hl_ref, hc_ref, shl_ref, scl_ref, shc_ref, scc_ref, gain_ref, o_ref):
    i = pl.program_id(1)

    @pl.when(i < T // TC)
    def _():
        o_ref[...] = _modnorm(hl_ref[...], gain_ref[...], shl_ref[...], scl_ref[...]).astype(BF)

    @pl.when(i >= T // TC)
    def _():
        o_ref[...] = _modnorm(hc_ref[...], gain_ref[...], shc_ref[...], scc_ref[...]).astype(BF)


def _prenorm(h, h_ctx, mod, gain):
    nlat = T // TC
    mspec = lambda row, k: pl.BlockSpec((None, None, 1, D), lambda b, i: (row(b), 3 + k, 0, 0))
    lat = lambda b: b
    cx = lambda b: NB
    return pl.pallas_call(
        _prenorm_kernel,
        grid=(NB, TT // TC),
        in_specs=[pl.BlockSpec((None, TC, D), lambda b, i: (b, jnp.minimum(i, nlat - 1), 0)),
                  pl.BlockSpec((None, TC, D), lambda b, i: (b, 0, 0)),
                  mspec(lat, 0), mspec(lat, 1), mspec(cx, 0), mspec(cx, 1),
                  pl.BlockSpec((1, D), lambda b, i: (0, 0))],
        out_specs=pl.BlockSpec((None, TC, D), lambda b, i: (b, i, 0)),
        out_shape=jax.ShapeDtypeStruct((NB, TT, D), BF),
        compiler_params=_cp(("parallel", "arbitrary")),
        name="prenorm",
    )(h, h_ctx, mod, mod, mod, mod, gain.reshape(1, D))


def _mm_kernel(a_ref, b_ref, o_ref):
    o_ref[...] = _dot(a_ref[...], b_ref[...]).astype(o_ref.dtype)


def _mm_split_kernel(a_ref, b_ref, o1_ref, o2_ref):
    r = _dot(a_ref[...], b_ref[...])
    n1 = o1_ref.shape[1]
    o1_ref[...] = r[:, :n1].astype(o1_ref.dtype)
    o2_ref[...] = r[:, n1:]


def _matmul_split(a, w, tm, n1, name):
    m, k = a.shape
    n = w.shape[1]
    return pl.pallas_call(
        _mm_split_kernel,
        grid=(m // tm,),
        in_specs=[pl.BlockSpec((tm, k), lambda i: (i, 0)),
                  pl.BlockSpec((k, n), lambda i: (0, 0))],
        out_specs=[pl.BlockSpec((tm, n1), lambda i: (i, 0)),
                   pl.BlockSpec((tm, n - n1), lambda i: (i, 0))],
        out_shape=[jax.ShapeDtypeStruct((m, n1), BF), jax.ShapeDtypeStruct((m, n - n1), F32)],
        compiler_params=_cp(("parallel",)),
        name=name,
    )(a, w)


def _matmul(a, w, tm, tn, out_dtype, name):
    m, k = a.shape
    n = w.shape[1]
    return pl.pallas_call(
        _mm_kernel,
        grid=(m // tm, n // tn),
        in_specs=[pl.BlockSpec((tm, k), lambda i, j: (i, 0)),
                  pl.BlockSpec((k, tn), lambda i, j: (0, j))],
        out_specs=pl.BlockSpec((tm, tn), lambda i, j: (i, j)),
        out_shape=jax.ShapeDtypeStruct((m, n), out_dtype),
        compiler_params=_cp(("parallel", "arbitrary")),
        name=name,
    )(a, w)


def _conv_kernel(x_ref, w_ref, o_ref, pad_ref):
    j = pl.program_id(1)
    half = CONV_K // 2
    for r0, n in ((0, T), (T, TC)):
        pad_ref[0:8, :] = jnp.zeros((8, HD), F32)
        pad_ref[8:8 + n, :] = x_ref[r0:r0 + n, :].astype(F32)
        pad_ref[8 + n:16 + n, :] = jnp.zeros((8, HD), F32)
        acc = jnp.zeros((n, HD), F32)
        for k in range(CONV_K):
            acc = acc + w_ref[k:k + 1, :] * pad_ref[8 + k - half:8 + k - half + n, :]
        y = _silu(acc)
        yn = y * lax.rsqrt(jnp.sum(y * y, axis=-1, keepdims=True) + EPS)
        o_ref[r0:r0 + n, :] = jnp.where(j < 2 * NH, yn, y).astype(o_ref.dtype)


def _gdn_conv(p, conv_w):
    return pl.pallas_call(
        _conv_kernel,
        grid=(NB, 3 * NH),
        in_specs=[pl.BlockSpec((None, TT, HD), lambda b, j: (b, 0, j)),
                  pl.BlockSpec((CONV_K, HD), lambda b, j: (0, j))],
        out_specs=pl.BlockSpec((None, TT, HD), lambda b, j: (b, 0, j)),
        out_shape=jax.ShapeDtypeStruct((NB, TT, 3 * W_A), BF),
        scratch_shapes=[pltpu.VMEM((T + 16, HD), F32)],
        compiler_params=_cp(("parallel", "parallel")),
        name="gdn_conv",
    )(p, conv_w)


def _split3(x):
    a = x.astype(BF)
    r = x - a.astype(F32)
    b = r.astype(BF)
    return a, b, (r - b.astype(F32)).astype(BF)


def _gdn_kernel(qf_ref, kf_ref, vf_ref, gf_ref, qb_ref, kb_ref, vb_ref, gb_ref, prm_ref, of_ref, ob_ref, s_ref):
    @pl.when(pl.program_id(1) == 0)
    def _():
        s_ref[...] = jnp.zeros_like(s_ref)

    ri = lax.broadcasted_iota(jnp.int32, (CS, CS), 0)
    ci = lax.broadcasted_iota(jnp.int32, (CS, CS), 1)
    eye_f = jnp.where(ri == ci, 1.0, 0.0)
    pair = (ri // 2) == (ci // 2)
    levels = [((ri // (2 * bs)) == (ci // (2 * bs))) & ((ri // bs) != (ci // bs)) for bs in (2, 4, 8, 16, 32)]
    prm = prm_ref[...]
    a_neg = -jnp.exp(prm[0:1, :])
    dt_b = prm[1:2, :]

    ch = []
    for d, refs in enumerate(((qf_ref, kf_ref, vf_ref, gf_ref, of_ref), (qb_ref, kb_ref, vb_ref, gb_ref, ob_ref))):
        q_ref, k_ref, v_ref, g_ref, o_ref = refs
        dm = (ri - ci) if d == 0 else (ci - ri)
        incl = dm >= 0
        strict = dm > 0
        tri = jnp.where(incl, 1.0, 0.0).astype(BF)
        tri_t = jnp.where(dm <= 0, 1.0, 0.0).astype(BF)
        gts = g_ref[...]
        x = gts + dt_b
        glog = a_neg * (jnp.maximum(x, 0.0) + jnp.log(1.0 + jnp.exp(-jnp.abs(x))))
        beta = jax.nn.sigmoid(gts)
        parts = _split3(glog)
        gc = sum(_dot(tri, g) for g in parts)
        gc_t = sum(_dot_tn(g, tri_t) for g in parts)
        last = CS - 1 if d == 0 else 0
        g_last = gc[last:last + 1, :]
        eg = jnp.exp(gc)
        ek = jnp.exp(g_last - gc)
        g_tot = jnp.exp(g_last)
        for h in range(NH):
            c = d * NH + h
            ch.append(dict(sl=slice(h * HD, (h + 1) * HD), c=c, incl=incl, strict=strict, o_ref=o_ref,
                           q_ref=q_ref, k_ref=k_ref, v_ref=v_ref,
                           b_c=beta[:, 2 * NH + c:2 * NH + c + 1], gc_c=gc[:, c:c + 1], gc_r=gc_t[c:c + 1, :],
                           eg=eg[:, c:c + 1], ek=ek[:, c:c + 1], g_tot=g_tot[:, c:c + 1]))

    for x in ch:
        kb = x["k_ref"][:, x["sl"]]
        qh = x["q_ref"][:, x["sl"]].astype(F32) * (HD ** -0.5)
        x["kh"], x["qh"] = kb.astype(F32), qh
        kq = _dot_nt(jnp.concatenate([kb, qh.astype(BF)], axis=0), kb)
        x["kk"], x["qk"] = kq[:CS], kq[CS:]
    for x in ch:
        decay = jnp.exp(jnp.where(x["incl"], x["gc_c"] - x["gc_r"], -1e30))
        lower = jnp.where(x["strict"], x["b_c"] * x["kk"] * decay, 0.0)
        x["qk"] = (x["qk"] * decay).astype(BF)
        x["lower"] = lower.astype(BF)
        x["t"] = eye_f - jnp.where(pair, lower, 0.0)
    for off in levels:
        for x in ch:
            x["tb"] = x["t"].astype(BF)
            x["mt"] = _dot(jnp.where(off, x["lower"], jnp.zeros_like(x["lower"])), x["tb"]).astype(BF)
        for x in ch:
            x["t"] = x["t"] - _dot(x["tb"], x["mt"])
    for x in ch:
        vh = x["v_ref"][:, x["sl"]].astype(F32)
        rhs = jnp.concatenate([vh * x["b_c"], x["kh"] * (x["b_c"] * x["eg"])], axis=1).astype(BF)
        x["uw"] = _dot(x["t"].astype(BF), rhs)
        x["q_dec"] = (x["qh"] * x["eg"]).astype(BF)
        x["k_dec"] = (x["kh"] * x["ek"]).astype(BF)
    for x in ch:
        st = s_ref[x["c"]]
        x["st"] = st
        stb = st.astype(BF)
        ws = _dot(jnp.concatenate([x["uw"][:, HD:].astype(BF), x["q_dec"]], axis=0), stb)
        x["vn"] = (x["uw"][:, :HD] - ws[:CS]).astype(BF)
        x["qs"] = ws[CS:]
    for x in ch:
        x["o_ref"][:, x["sl"]] = x["qs"] + _dot(x["qk"], x["vn"])
        s_ref[x["c"]] = x["st"] * x["g_tot"] + _dot_tn(x["k_dec"], x["vn"])


def _gdn_scan(qkv, gates, prm):
    chunk_f = lambda s: (s + NCH_LAT) % NCH
    chunk_b = lambda s: NCH - 1 - s

    def specs(chunk):
        qkv_spec = lambda c: pl.BlockSpec((None, CS, W_A), lambda b, s: (b, chunk(s), c))
        return [qkv_spec(0), qkv_spec(1), qkv_spec(2),
                pl.BlockSpec((None, CS, 128), lambda b, s: (b, chunk(s), 0))]

    return pl.pallas_call(
        _gdn_kernel,
        grid=(NB, NCH),
        in_specs=specs(chunk_f) + specs(chunk_b) + [pl.BlockSpec((8, 128), lambda b, s: (0, 0))],
        out_specs=[pl.BlockSpec((None, CS, W_A), lambda b, s: (b, chunk_f(s), 0)),
                   pl.BlockSpec((None, CS, W_A), lambda b, s: (b, chunk_b(s), 0))],
        out_shape=[jax.ShapeDtypeStruct((NB, TT, W_A), F32)] * 2,
        scratch_shapes=[pltpu.VMEM((2 * NH, HD, HD), F32)],
        compiler_params=_cp(("parallel", "arbitrary")),
        name="gdn_scan",
    )(qkv, qkv, qkv, gates, qkv, qkv, qkv, gates, prm)


def _rope_kernel(x_ref, g_ref, cos_ref, sin_ref, o_ref):
    x = x_ref[...].astype(F32)
    y = x * lax.rsqrt(jnp.mean(x * x, axis=-1, keepdims=True) + EPS) * g_ref[...]
    lane = lax.broadcasted_iota(jnp.int32, y.shape, 1)
    partner = jnp.where((lane % 64) < 32, pltpu.roll(y, 128 - 32, axis=1), pltpu.roll(y, 32, axis=1))
    o_ref[...] = (y * cos_ref[...] + partner * sin_ref[...]).astype(o_ref.dtype)


def _rope_norm(pb, gains, cos_t, sin_t):
    nh = NH + KVH
    return pl.pallas_call(
        _rope_kernel,
        grid=(NB, nh),
        in_specs=[pl.BlockSpec((None, TT, HD), lambda b, j: (b, 0, j)),
                  pl.BlockSpec((None, 1, HD), lambda b, j: (j, 0, 0)),
                  pl.BlockSpec((TT, HD), lambda b, j: (0, 0)),
                  pl.BlockSpec((TT, HD), lambda b, j: (0, 0))],
        out_specs=pl.BlockSpec((None, TT, HD), lambda b, j: (b, 0, j)),
        out_shape=jax.ShapeDtypeStruct((NB, TT, nh * HD), BF),
        compiler_params=_cp(("parallel", "parallel")),
        name="rope_norm",
    )(pb, gains, cos_t, sin_t)


def _attn_kernel(q_ref, k_ref, v_ref, o_ref):
    k = k_ref[...]
    v = v_ref[...]
    for g in range(GQ):
        sl = slice(g * HD, (g + 1) * HD)
        s = _dot_nt(q_ref[:, sl], k)
        pr = jnp.exp(s - jnp.max(s, axis=-1, keepdims=True))
        den = jnp.sum(pr, axis=-1, keepdims=True)
        o_ref[:, sl] = (_dot(pr.astype(BF), v) / den).astype(o_ref.dtype)


def _attention(qk, pb, tq=512):
    return pl.pallas_call(
        _attn_kernel,
        grid=(NB, KVH, T // tq),
        in_specs=[pl.BlockSpec((None, tq, GQ * HD), lambda b, kv, i: (b, i, kv)),
                  pl.BlockSpec((None, TT, HD), lambda b, kv, i: (b, 0, NH + kv)),
                  pl.BlockSpec((None, TT, HD), lambda b, kv, i: (b, 0, COL_VB // HD + kv))],
        out_specs=pl.BlockSpec((None, tq, GQ * HD), lambda b, kv, i: (b, i, kv)),
        out_shape=jax.ShapeDtypeStruct((NB, T, W_A), BF),
        compiler_params=_cp(("parallel", "parallel", "arbitrary")),
        name="attention",
    )(qk, qk, pb)


def _mixout_kernel(of_ref, ob_ref, z_ref, a_ref, h_ref, gt_ref, gain_ref, w_ref, o_ref):
    parts = []
    for hd in range(NH):
        sl = slice(hd * HD, (hd + 1) * HD)
        o = of_ref[:, sl] + ob_ref[:, sl]
        n = o * lax.rsqrt(jnp.mean(o * o, axis=-1, keepdims=True) + EPS) * gain_ref[...]
        parts.append((n * _silu(z_ref[:, sl].astype(F32))).astype(BF))
    gd = jnp.concatenate(parts, axis=1)
    y = _dot(gd, w_ref[0:W_A, :]) + _dot(a_ref[...], w_ref[W_A:2 * W_A, :])
    o_ref[...] = h_ref[...] + gt_ref[...] * y


def _mixout(o_f, o_b, pa, attn, h, mod, gain, w_out, tm=256):
    return pl.pallas_call(
        _mixout_kernel,
        grid=(NB, T // tm),
        in_specs=[pl.BlockSpec((None, tm, W_A), lambda b, i: (b, i, 0)),
                  pl.BlockSpec((None, tm, W_A), lambda b, i: (b, i, 0)),
                  pl.BlockSpec((None, tm, W_A), lambda b, i: (b, i, COL_Z // W_A)),
                  pl.BlockSpec((None, tm, W_A), lambda b, i: (b, i, 0)),
                  pl.BlockSpec((None, tm, D), lambda b, i: (b, i, 0)),
                  pl.BlockSpec((None, None, 1, D), lambda b, i: (b, 5, 0, 0)),
                  pl.BlockSpec((1, HD), lambda b, i: (0, 0)),
                  pl.BlockSpec((D, D), lambda b, i: (0, 0))],
        out_specs=pl.BlockSpec((None, tm, D), lambda b, i: (b, i, 0)),
        out_shape=jax.ShapeDtypeStruct((NB, T, D), F32),
        compiler_params=_cp(("parallel", "parallel")),
        name="mix_out",
    )(o_f, o_b, pa, attn, h, mod, gain.reshape(1, HD), w_out)


def _dft_tables():
    def cs(n):
        idx = np.arange(n, dtype=np.int64)
        ang = 2.0 * np.pi * ((idx[:, None] * idx[None, :]) % n).astype(np.float64) / n
        return np.cos(ang) / math.sqrt(n), np.sin(ang) / math.sqrt(n)

    cc, sc = cs(GC)
    ct, st = cs(T)
    ch = np.concatenate([cc, sc], axis=1).astype(np.float32)
    tm = np.concatenate([ct, -st], axis=1).astype(np.float32)
    return jnp.asarray(ch, dtype=BF), jnp.asarray(tm, dtype=BF)


def _fchan_kernel(h_ref, sh_ref, sc_ref, gain_ref, w_ref, a_ref, b_ref):
    u = _modnorm(h_ref[...], gain_ref[...], sh_ref[...], sc_ref[...]).astype(BF)
    w = w_ref[...]
    for g in range(FG):
        sl = slice(g * GC, (g + 1) * GC)
        ab = _dot(u[:, sl], w)
        a_ref[:, sl] = ab[:, :GC].astype(a_ref.dtype)
        b_ref[:, sl] = ab[:, GC:].astype(b_ref.dtype)


def _fourier_channels(h, mod, gain, ch_tab, tm=512):
    return pl.pallas_call(
        _fchan_kernel,
        grid=(NB, T // tm),
        in_specs=[pl.BlockSpec((None, tm, D), lambda b, i: (b, i, 0)),
                  pl.BlockSpec((None, None, 1, D), lambda b, i: (b, 3, 0, 0)),
                  pl.BlockSpec((None, None, 1, D), lambda b, i: (b, 4, 0, 0)),
                  pl.BlockSpec((1, D), lambda b, i: (0, 0)),
                  pl.BlockSpec((GC, 2 * GC), lambda b, i: (0, 0))],
        out_specs=[pl.BlockSpec((None, tm, D), lambda b, i: (b, i, 0))] * 2,
        out_shape=[jax.ShapeDtypeStruct((NB, T, D), BF)] * 2,
        compiler_params=_cp(("parallel", "parallel")),
        name="fourier_channels",
    )(h, mod, mod, gain.reshape(1, D), ch_tab)


def _ftime_kernel(t_ref, a_ref, b_ref, o_ref):
    y = _dot(t_ref[:, :T], a_ref[...]) + _dot(t_ref[:, T:], b_ref[...])
    o_ref[...] = y.astype(o_ref.dtype)


def _fourier_time(tm_tab, a, b, tm=1024, tn=1024):
    return pl.pallas_call(
        _ftime_kernel,
        grid=(NB, D // tn, T // tm),
        in_specs=[pl.BlockSpec((tm, 2 * T), lambda b, j, i: (i, 0)),
                  pl.BlockSpec((None, T, tn), lambda b, j, i: (b, 0, j)),
                  pl.BlockSpec((None, T, tn), lambda b, j, i: (b, 0, j))],
        out_specs=pl.BlockSpec((None, tm, tn), lambda b, j, i: (b, i, j)),
        out_shape=jax.ShapeDtypeStruct((NB, T, D), BF),
        compiler_params=_cp(("parallel", "parallel", "arbitrary")),
        name="fourier_time",
    )(tm_tab, a, b)


def _resmm_kernel(y_ref, w_ref, h_ref, gt_ref, o_ref):
    o_ref[...] = h_ref[...] + gt_ref[...] * _dot(y_ref[...], w_ref[...])


def _fourier_out(y, w_f, h, mod, tm=512):
    return pl.pallas_call(
        _resmm_kernel,
        grid=(NB, T // tm),
        in_specs=[pl.BlockSpec((None, tm, D), lambda b, i: (b, i, 0)),
                  pl.BlockSpec((D, D), lambda b, i: (0, 0)),
                  pl.BlockSpec((None, tm, D), lambda b, i: (b, i, 0)),
                  pl.BlockSpec((None, None, 1, D), lambda b, i: (b, 5, 0, 0))],
        out_specs=pl.BlockSpec((None, tm, D), lambda b, i: (b, i, 0)),
        out_shape=jax.ShapeDtypeStruct((NB, T, D), F32),
        compiler_params=_cp(("parallel", "parallel")),
        name="fourier_out",
    )(y, w_f, h, mod)


def _rope_tables():
    half = HD // 4
    inv = ROPE_THETA ** (-np.arange(half, dtype=np.float64) / half)
    t = np.arange(T)
    ang_r = (t // GRID_W)[:, None] * inv[None, :]
    ang_c = (t % GRID_W)[:, None] * inv[None, :]
    cos = np.concatenate([np.cos(ang_r)] * 2 + [np.cos(ang_c)] * 2, axis=1)
    sin = np.concatenate([-np.sin(ang_r), np.sin(ang_r), -np.sin(ang_c), np.sin(ang_c)], axis=1)
    cos = np.concatenate([cos, np.ones((TC, HD))], axis=0).astype(np.float32)
    sin = np.concatenate([sin, np.zeros((TC, HD))], axis=0).astype(np.float32)
    return jnp.asarray(cos), jnp.asarray(sin)


def kernel(x, c, ctx, c_ctx, w_mod, b_mod, norm_w, ffn_w_gu, ffn_w_down, w_in, conv_w, a_log, dt_bias,
           gdn_norm, q_norm, k_norm, w_out, w_fourier, final_norm):
    cvec = jnp.concatenate([c, c_ctx[None, :], jnp.zeros((8 - NB - 1, D), F32)], axis=0)
    mod = _modulation(cvec, w_mod, b_mod).reshape(DEPTH, 8, N_MOD, 1, D)
    w_gu = ffn_w_gu[0, 0].astype(BF)
    w_dn = ffn_w_down[0, 0].astype(BF)

    m0 = mod[0]
    h_ctx = _ffn(ctx.reshape(1, NB * TC, D), m0, 0, NB, norm_w[0, 0], w_gu, w_dn).reshape(NB, TC, D)
    h, w_gu, w_dn = _ffn(x, m0, 0, 0, norm_w[0, 0], w_gu, w_dn, nxt=(ffn_w_gu, ffn_w_down, 0, 1))

    wi = w_in[0]
    w_a = wi[:, :NPA].astype(BF)
    w_b = jnp.concatenate([wi[:, NPA + 4 * NH:].astype(BF), wi[:, NPA:NPA + 4 * NH].astype(BF),
                           jnp.zeros((D, 128 - 4 * NH), BF)], axis=1)
    u = _prenorm(h, h_ctx, m0, norm_w[0, 1]).reshape(NB * TT, D)
    pa = _matmul(u, w_a, 1024, 2048, BF, "in_proj_a").reshape(NB, TT, NPA)
    pb, gates = _matmul_split(u, w_b, 1024, NPB, "in_proj_b")
    pb = pb.reshape(NB, TT, NPB)

    qkv = _gdn_conv(pa, conv_w[0])
    prm = jnp.zeros((8, 128), F32)
    prm = prm.at[0, :2 * NH].set(a_log[0].reshape(-1)).at[1, :2 * NH].set(dt_bias[0].reshape(-1))
    o_f, o_b = _gdn_scan(qkv, gates.reshape(NB, TT, 128), prm)

    gains = jnp.concatenate([jnp.broadcast_to(q_norm[0] * (HD ** -0.5), (NH, HD)),
                             jnp.broadcast_to(k_norm[0], (KVH, HD))], axis=0).reshape(NH + KVH, 1, HD)
    cos_t, sin_t = _rope_tables()
    qk = _rope_norm(pb, gains, cos_t, sin_t)
    attn = _attention(qk, pb)

    h = _mixout(o_f, o_b, pa, attn, h, m0, gdn_norm[0], w_out[0].astype(BF))
    h, w_gu, w_dn = _ffn(h, m0, 6, 0, norm_w[0, 2], w_gu, w_dn, nxt=(ffn_w_gu, ffn_w_down, 1, 0))

    m1 = mod[1]
    h, w_gu, w_dn = _ffn(h, m1, 0, 0, norm_w[1, 0], w_gu, w_dn, nxt=(ffn_w_gu, ffn_w_down, 1, 1))
    ch_tab, tm_tab = _dft_tables()
    fa, fb = _fourier_channels(h, m1, norm_w[1, 1], ch_tab)
    y = _fourier_time(tm_tab, fa, fb)
    h = _fourier_out(y, w_fourier[0].astype(BF), h, m1)
    return _ffn(h, m1, 6, 0, norm_w[1, 2], w_gu, w_dn, final_gain=final_norm)
```

```python
import functools
import math

import numpy as np
import jax
import jax.numpy as jnp
from jax import lax
from jax.experimental import pallas as pl
from jax.experimental.pallas import tpu as pltpu

D = 2048
NB = 4
T = 2048
GRID_W = 64
TC = 256
TT = T + TC
DEPTH = 2
EPS = 1e-6
N_MOD = 9
FH = 5632
NH = 8
HD = 128
KVH = 2
GQ = NH // KVH
CONV_K = 5
CS = 64
NCH = TT // CS
NCH_LAT = T // CS
ROPE_THETA = 10000.0
FG = 4
GC = D // FG
W_A = NH * HD
NPA = 4 * W_A
COL_Z = 3 * W_A
NPB = W_A + 2 * KVH * HD
COL_KB = W_A
COL_VB = COL_KB + KVH * HD

VMEM_LIMIT = 58 * 1024 * 1024
BF = jnp.bfloat16
F32 = jnp.float32


def _cp(sem):
    return pltpu.CompilerParams(dimension_semantics=sem, vmem_limit_bytes=VMEM_LIMIT)


def _dot(a, b):
    return jnp.dot(a, b, preferred_element_type=F32)


def _dot_nt(a, b):
    return lax.dot_general(a, b, (((1,), (1,)), ((), ())), preferred_element_type=F32)


def _dot_tn(a, b):
    return lax.dot_general(a, b, (((0,), (0,)), ((), ())), preferred_element_type=F32)


def _silu(x):
    return x * jax.nn.sigmoid(x)


def _modnorm(x, gain, shift, scale):
    return x * lax.rsqrt(jnp.mean(x * x, axis=-1, keepdims=True) + EPS) * (gain * (1.0 + scale)) + shift


def _mod_kernel(c_ref, w_ref, b_ref, o_ref):
    a = _silu(c_ref[...]).astype(BF)
    o_ref[...] = _dot(a, w_ref[...].astype(BF)) + b_ref[...]


def _modulation(cvec, w_mod, b_mod):
    tn = 1024
    n = N_MOD * D
    per = D // tn
    return pl.pallas_call(
        _mod_kernel,
        grid=(DEPTH, n // tn),
        in_specs=[pl.BlockSpec((8, D), lambda l, j: (0, 0)),
                  pl.BlockSpec((None, D, tn), lambda l, j: (l, 0, j)),
                  pl.BlockSpec((None, 1, tn), lambda l, j: (l, 0, j))],
        out_specs=pl.BlockSpec((None, None, 8, tn), lambda l, j: (l, j // per, 0, j % per)),
        out_shape=jax.ShapeDtypeStruct((DEPTH, N_MOD, 8, D), F32),
        compiler_params=_cp(("parallel", "arbitrary")),
        name="modulation",
    )(cvec, w_mod, b_mod.reshape(DEPTH, 1, n))


def _mod_spec(layer, k):
    return pl.BlockSpec((None, None, 8, D), lambda *_: (layer, k, 0, 0))


def _mrow(ref, r):
    return ref[pl.ds(r, 1), :]


def _ffn_kernel(hc_ref, hn_ref, sh_ref, sc_ref, gt_ref, gain_ref, wg_ref, wu_ref, wd_ref, *rest,
                nt, ni, nj, row0, nchunk, final, convert):
    rest = list(rest)
    fin_ref = rest.pop(0) if final else None
    if convert:
        ngu_ref, ndn_ref = rest.pop(0), rest.pop(0)
    o_ref = rest.pop(0)
    if convert:
        ogu_ref, odn_ref = rest.pop(0), rest.pop(0)
    ua_ref, ub_ref = rest
    t = pl.program_id(0)
    j = pl.program_id(1)
    r_cur = t // ni + row0
    r_next = jnp.minimum(t + 1, nt - 1) // ni + row0
    gain = gain_ref[...]

    @pl.when((t == 0) & (j == 0))
    def _():
        ua_ref[...] = _modnorm(hc_ref[...], gain, _mrow(sh_ref, r_cur), _mrow(sc_ref, r_cur)).astype(BF)

    @pl.when(j == 0)
    def _():
        o_ref[...] = jnp.zeros_like(o_ref)

    rc = hn_ref.shape[0] // nchunk
    rows = pl.ds(pl.multiple_of(jnp.minimum(j, nchunk - 1) * rc, rc), rc)

    def step(u_cur_ref, u_next_ref):
        u_next_ref[rows, :] = _modnorm(hn_ref[rows, :], gain, _mrow(sh_ref, r_next),
                                       _mrow(sc_ref, r_next)).astype(BF)
        u = u_cur_ref[...]
        g = _dot(u, wg_ref[...])
        up = _dot(u, wu_ref[...])
        act = (_silu(g) * up).astype(BF)
        o_ref[...] += _dot(act, wd_ref[...])
        if convert:
            ogu_ref[...] = ngu_ref[...].astype(BF)
            odn_ref[...] = ndn_ref[...].astype(BF)

    pl.when(t % 2 == 0)(lambda: step(ua_ref, ub_ref))
    pl.when(t % 2 == 1)(lambda: step(ub_ref, ua_ref))

    @pl.when(j == nj - 1)
    def _():
        r = hc_ref[...] + 0.5 * _mrow(gt_ref, r_cur) * o_ref[...]
        if final:
            r = r * lax.rsqrt(jnp.mean(r * r, axis=-1, keepdims=True) + EPS) * fin_ref[...]
        o_ref[...] = r


def _ffn(h, mod, layer, base, mod_row0, gain, w_gu, w_down, final_gain=None, nxt=None, tm=512, tf=512, nchunk=8):
    nb, rows, _ = h.shape
    nj = FH // tf
    ni = rows // tm
    nt = nb * ni
    assert nj >= nchunk and tm % nchunk == 0
    h2 = h.reshape(nb * rows, D)
    vec = pl.BlockSpec((1, D), lambda t, j: (0, 0))
    in_specs = [pl.BlockSpec((tm, D), lambda t, j: (t, 0)),
                pl.BlockSpec((tm, D), lambda t, j: (jnp.minimum(t + 1, nt - 1), 0)),
                _mod_spec(layer, base), _mod_spec(layer, base + 1), _mod_spec(layer, base + 2), vec,
                pl.BlockSpec((D, tf), lambda t, j: (0, j)),
                pl.BlockSpec((D, tf), lambda t, j: (0, j + nj)),
                pl.BlockSpec((tf, D), lambda t, j: (j, 0))]
    args = [h2, h2, mod, mod, mod, gain.reshape(1, D), w_gu, w_gu, w_down]
    out_specs = [pl.BlockSpec((tm, D), lambda t, j: (t, 0))]
    out_shape = [jax.ShapeDtypeStruct(h2.shape, F32)]
    if final_gain is not None:
        in_specs.append(vec)
        args.append(final_gain.reshape(1, D))
    if nxt is not None:
        n_gu, n_dn, nl, nw = nxt
        gu_blk = (D // nt, 2 * FH // nj)
        dn_blk = (FH // (nt * nj), D)
        gu_idx = lambda t, j: (t, j)
        dn_idx = lambda t, j: (t * nj + j, 0)
        in_specs += [pl.BlockSpec((None, None) + gu_blk, lambda t, j: (nl, nw) + gu_idx(t, j)),
                     pl.BlockSpec((None, None) + dn_blk, lambda t, j: (nl, nw) + dn_idx(t, j))]
        args += [n_gu, n_dn]
        out_specs += [pl.BlockSpec(gu_blk, gu_idx), pl.BlockSpec(dn_blk, dn_idx)]
        out_shape += [jax.ShapeDtypeStruct((D, 2 * FH), BF), jax.ShapeDtypeStruct((FH, D), BF)]
    out = pl.pallas_call(
        functools.partial(_ffn_kernel, nt=nt, ni=ni, nj=nj, row0=mod_row0, nchunk=nchunk,
                          final=final_gain is not None, convert=nxt is not None),
        grid=(nt, nj),
        in_specs=in_specs,
        out_specs=out_specs,
        out_shape=out_shape,
        scratch_shapes=[pltpu.VMEM((tm, D), BF), pltpu.VMEM((tm, D), BF)],
        compiler_params=_cp(("arbitrary", "arbitrary")),
        name="ffn",
    )(*args)
    res = out[0].reshape(h.shape)
    return (res, out[1], out[2]) if nxt is not None else res


def _prenorm_kernel(hl_ref, hc_ref, sh_ref, sc_ref, gain_ref, o_ref):
    b = pl.program_id(0)
    i = pl.program_id(1)

    @pl.when(i < T // TC)
    def _():
        o_ref[...] = _modnorm(hl_ref[...], gain_ref[...], _mrow(sh_ref, b), _mrow(sc_ref, b)).astype(BF)

    @pl.when(i >= T // TC)
    def _():
        o_ref[...] = _modnorm(hc_ref[...], gain_ref[...], _mrow(sh_ref, NB), _mrow(sc_ref, NB)).astype(BF)


def _prenorm(h, h_ctx, mod, layer, gain):
    nlat = T // TC
    return pl.pallas_call(
        _prenorm_kernel,
        grid=(NB, TT // TC),
        in_specs=[pl.BlockSpec((None, TC, D), lambda b, i: (b, jnp.minimum(i, nlat - 1), 0)),
                  pl.BlockSpec((None, TC, D), lambda b, i: (b, 0, 0)),
                  _mod_spec(layer, 3), _mod_spec(layer, 4),
                  pl.BlockSpec((1, D), lambda b, i: (0, 0))],
        out_specs=pl.BlockSpec((None, TC, D), lambda b, i: (b, i, 0)),
        out_shape=jax.ShapeDtypeStruct((NB, TT, D), BF),
        compiler_params=_cp(("parallel", "arbitrary")),
        name="prenorm",
    )(h, h_ctx, mod, mod, gain.reshape(1, D))


def _mm_kernel(a_ref, b_ref, o_ref):
    o_ref[...] = _dot(a_ref[...], b_ref[...]).astype(o_ref.dtype)


def _mm_split_kernel(a_ref, b_ref, o1_ref, o2_ref):
    r = _dot(a_ref[...], b_ref[...])
    n1 = o1_ref.shape[1]
    o1_ref[...] = r[:, :n1].astype(o1_ref.dtype)
    o2_ref[...] = r[:, n1:]


def _matmul_split(a, w, tm, n1, name):
    m, k = a.shape
    n = w.shape[1]
    return pl.pallas_call(
        _mm_split_kernel,
        grid=(m // tm,),
        in_specs=[pl.BlockSpec((tm, k), lambda i: (i, 0)),
                  pl.BlockSpec((k, n), lambda i: (0, 0))],
        out_specs=[pl.BlockSpec((tm, n1), lambda i: (i, 0)),
                   pl.BlockSpec((tm, n - n1), lambda i: (i, 0))],
        out_shape=[jax.ShapeDtypeStruct((m, n1), BF), jax.ShapeDtypeStruct((m, n - n1), F32)],
        compiler_params=_cp(("parallel",)),
        name=name,
    )(a, w)


def _matmul(a, w, tm, tn, out_dtype, name):
    m, k = a.shape
    n = w.shape[1]
    return pl.pallas_call(
        _mm_kernel,
        grid=(m // tm, n // tn),
        in_specs=[pl.BlockSpec((tm, k), lambda i, j: (i, 0)),
                  pl.BlockSpec((k, tn), lambda i, j: (0, j))],
        out_specs=pl.BlockSpec((tm, tn), lambda i, j: (i, j)),
        out_shape=jax.ShapeDtypeStruct((m, n), out_dtype),
        compiler_params=_cp(("parallel", "arbitrary")),
        name=name,
    )(a, w)


def _conv_kernel(x_ref, w_ref, o_ref, pad_ref):
    j = pl.program_id(1)
    half = CONV_K // 2
    for r0, n in ((0, T), (T, TC)):
        pad_ref[0:8, :] = jnp.zeros((8, HD), F32)
        pad_ref[8:8 + n, :] = x_ref[r0:r0 + n, :].astype(F32)
        pad_ref[8 + n:16 + n, :] = jnp.zeros((8, HD), F32)
        acc = jnp.zeros((n, HD), F32)
        for k in range(CONV_K):
            acc = acc + w_ref[k:k + 1, :] * pad_ref[8 + k - half:8 + k - half + n, :]
        y = _silu(acc)
        yn = y * lax.rsqrt(jnp.sum(y * y, axis=-1, keepdims=True) + EPS)
        o_ref[r0:r0 + n, :] = jnp.where(j < 2 * NH, yn, y).astype(o_ref.dtype)


def _gdn_conv(p, conv_w):
    return pl.pallas_call(
        _conv_kernel,
        grid=(NB, 3 * NH),
        in_specs=[pl.BlockSpec((None, TT, HD), lambda b, j: (b, 0, j)),
                  pl.BlockSpec((CONV_K, HD), lambda b, j: (0, j))],
        out_specs=pl.BlockSpec((None, TT, HD), lambda b, j: (b, 0, j)),
        out_shape=jax.ShapeDtypeStruct((NB, TT, 3 * W_A), BF),
        scratch_shapes=[pltpu.VMEM((T + 16, HD), F32)],
        compiler_params=_cp(("parallel", "parallel")),
        name="gdn_conv",
    )(p, conv_w)


def _split3(x):
    a = x.astype(BF)
    r = x - a.astype(F32)
    b = r.astype(BF)
    return a, b, (r - b.astype(F32)).astype(BF)


def _gdn_kernel(qf_ref, kf_ref, vf_ref, gf_ref, qb_ref, kb_ref, vb_ref, gb_ref, prm_ref, of_ref, ob_ref, s_ref):
    @pl.when(pl.program_id(1) == 0)
    def _():
        s_ref[...] = jnp.zeros_like(s_ref)

    ri = lax.broadcasted_iota(jnp.int32, (CS, CS), 0)
    ci = lax.broadcasted_iota(jnp.int32, (CS, CS), 1)
    eye_f = jnp.where(ri == ci, 1.0, 0.0)
    pair = (ri // 2) == (ci // 2)
    levels = [((ri // (2 * bs)) == (ci // (2 * bs))) & ((ri // bs) != (ci // bs)) for bs in (2, 4, 8, 16, 32)]
    prm = prm_ref[...]
    a_neg = -jnp.exp(prm[0:1, :])
    dt_b = prm[1:2, :]

    ch = []
    for d, refs in enumerate(((qf_ref, kf_ref, vf_ref, gf_ref, of_ref), (qb_ref, kb_ref, vb_ref, gb_ref, ob_ref))):
        q_ref, k_ref, v_ref, g_ref, o_ref = refs
        dm = (ri - ci) if d == 0 else (ci - ri)
        incl = dm >= 0
        strict = dm > 0
        tri = jnp.where(incl, 1.0, 0.0).astype(BF)
        tri_t = jnp.where(dm <= 0, 1.0, 0.0).astype(BF)
        gts = g_ref[...]
        x = gts + dt_b
        glog = a_neg * (jnp.maximum(x, 0.0) + jnp.log(1.0 + jnp.exp(-jnp.abs(x))))
        beta = jax.nn.sigmoid(gts)
        parts = _split3(glog)
        gc = sum(_dot(tri, g) for g in parts)
        gc_t = sum(_dot_tn(g, tri_t) for g in parts)
        last = CS - 1 if d == 0 else 0
        g_last = gc[last:last + 1, :]
        eg = jnp.exp(gc)
        ek = jnp.exp(g_last - gc)
        g_tot = jnp.exp(g_last)
        for h in range(NH):
            c = d * NH + h
            ch.append(dict(sl=slice(h * HD, (h + 1) * HD), c=c, incl=incl, strict=strict, o_ref=o_ref,
                           q_ref=q_ref, k_ref=k_ref, v_ref=v_ref,
                           b_c=beta[:, 2 * NH + c:2 * NH + c + 1], gc_c=gc[:, c:c + 1], gc_r=gc_t[c:c + 1, :],
                           eg=eg[:, c:c + 1], ek=ek[:, c:c + 1], g_tot=g_tot[:, c:c + 1]))

    for x in ch:
        kb = x["k_ref"][:, x["sl"]]
        qh = x["q_ref"][:, x["sl"]].astype(F32) * (HD ** -0.5)
        x["kh"], x["qh"] = kb.astype(F32), qh
        kq = _dot_nt(jnp.concatenate([kb, qh.astype(BF)], axis=0), kb)
        x["kk"], x["qk"] = kq[:CS], kq[CS:]
    for x in ch:
        decay = jnp.exp(jnp.where(x["incl"], x["gc_c"] - x["gc_r"], -1e30))
        lower = jnp.where(x["strict"], x["b_c"] * x["kk"] * decay, 0.0)
        x["qk"] = (x["qk"] * decay).astype(BF)
        x["lower"] = lower.astype(BF)
        x["t"] = eye_f - jnp.where(pair, lower, 0.0)
    for off in levels:
        for x in ch:
            x["tb"] = x["t"].astype(BF)
            x["mt"] = _dot(jnp.where(off, x["lower"], jnp.zeros_like(x["lower"])), x["tb"]).astype(BF)
        for x in ch:
            x["t"] = x["t"] - _dot(x["tb"], x["mt"])
    for x in ch:
        vh = x["v_ref"][:, x["sl"]].astype(F32)
        rhs = jnp.concatenate([vh * x["b_c"], x["kh"] * (x["b_c"] * x["eg"])], axis=1).astype(BF)
        x["uw"] = _dot(x["t"].astype(BF), rhs)
        x["q_dec"] = (x["qh"] * x["eg"]).astype(BF)
        x["k_dec"] = (x["kh"] * x["ek"]).astype(BF)
    for x in ch:
        st = s_ref[x["c"]]
        x["st"] = st
        stb = st.astype(BF)
        ws = _dot(jnp.concatenate([x["uw"][:, HD:].astype(BF), x["q_dec"]], axis=0), stb)
        x["vn"] = (x["uw"][:, :HD] - ws[:CS]).astype(BF)
        x["qs"] = ws[CS:]
    for x in ch:
        x["o_ref"][:, x["sl"]] = x["qs"] + _dot(x["qk"], x["vn"])
        s_ref[x["c"]] = x["st"] * x["g_tot"] + _dot_tn(x["k_dec"], x["vn"])


def _gdn_scan(qkv, gates, prm):
    chunk_f = lambda s: (s + NCH_LAT) % NCH
    chunk_b = lambda s: NCH - 1 - s

    def specs(chunk):
        qkv_spec = lambda c: pl.BlockSpec((None, CS, W_A), lambda b, s: (b, chunk(s), c))
        return [qkv_spec(0), qkv_spec(1), qkv_spec(2),
                pl.BlockSpec((None, CS, 128), lambda b, s: (b, chunk(s), 0))]

    return pl.pallas_call(
        _gdn_kernel,
        grid=(NB, NCH),
        in_specs=specs(chunk_f) + specs(chunk_b) + [pl.BlockSpec((8, 128), lambda b, s: (0, 0))],
        out_specs=[pl.BlockSpec((None, CS, W_A), lambda b, s: (b, chunk_f(s), 0)),
                   pl.BlockSpec((None, CS, W_A), lambda b, s: (b, chunk_b(s), 0))],
        out_shape=[jax.ShapeDtypeStruct((NB, TT, W_A), F32)] * 2,
        scratch_shapes=[pltpu.VMEM((2 * NH, HD, HD), F32)],
        compiler_params=_cp(("parallel", "arbitrary")),
        name="gdn_scan",
    )(qkv, qkv, qkv, gates, qkv, qkv, qkv, gates, prm)


def _rope_kernel(x_ref, g_ref, cos_ref, sin_ref, o_ref):
    x = x_ref[...].astype(F32)
    y = x * lax.rsqrt(jnp.mean(x * x, axis=-1, keepdims=True) + EPS) * g_ref[...]
    lane = lax.broadcasted_iota(jnp.int32, y.shape, 1)
    partner = jnp.where((lane % 64) < 32, pltpu.roll(y, 128 - 32, axis=1), pltpu.roll(y, 32, axis=1))
    o_ref[...] = (y * cos_ref[...] + partner * sin_ref[...]).astype(o_ref.dtype)


def _rope_norm(pb, gains, cos_t, sin_t):
    nh = NH + KVH
    return pl.pallas_call(
        _rope_kernel,
        grid=(NB, nh),
        in_specs=[pl.BlockSpec((None, TT, HD), lambda b, j: (b, 0, j)),
                  pl.BlockSpec((None, 1, HD), lambda b, j: (j, 0, 0)),
                  pl.BlockSpec((TT, HD), lambda b, j: (0, 0)),
                  pl.BlockSpec((TT, HD), lambda b, j: (0, 0))],
        out_specs=pl.BlockSpec((None, TT, HD), lambda b, j: (b, 0, j)),
        out_shape=jax.ShapeDtypeStruct((NB, TT, nh * HD), BF),
        compiler_params=_cp(("parallel", "parallel")),
        name="rope_norm",
    )(pb, gains, cos_t, sin_t)


def _attn_kernel(q_ref, k_ref, v_ref, o_ref, s_ref, p_ref):
    k = k_ref[...]
    v_aug = jnp.concatenate([v_ref[...], jnp.ones((TT, HD), BF)], axis=1)
    tq = q_ref.shape[0]
    rb = 16
    for g in range(GQ):
        sl = slice(g * HD, (g + 1) * HD)
        buf = g % 2
        s_ref[buf] = _dot_nt(q_ref[:, sl], k)
        for r in range(tq // rb):
            rows = slice(r * rb, (r + 1) * rb)
            m = s_ref[buf, rows, 0:HD]
            for c in range(1, TT // HD):
                m = jnp.maximum(m, s_ref[buf, rows, c * HD:(c + 1) * HD])
            mx = jnp.broadcast_to(jnp.max(m, axis=-1, keepdims=True), (rb, HD))
            for c in range(TT // HD):
                cols = slice(c * HD, (c + 1) * HD)
                p_ref[buf, rows, cols] = jnp.exp(s_ref[buf, rows, cols] - mx).astype(BF)
        o = _dot(p_ref[buf], v_aug)
        o_ref[:, sl] = (o[:, :HD] / o[:, HD:]).astype(o_ref.dtype)


def _attention(qk, pb, tq=512):
    return pl.pallas_call(
        _attn_kernel,
        grid=(NB, KVH, T // tq),
        in_specs=[pl.BlockSpec((None, tq, GQ * HD), lambda b, kv, i: (b, i, kv)),
                  pl.BlockSpec((None, TT, HD), lambda b, kv, i: (b, 0, NH + kv)),
                  pl.BlockSpec((None, TT, HD), lambda b, kv, i: (b, 0, COL_VB // HD + kv))],
        out_specs=pl.BlockSpec((None, tq, GQ * HD), lambda b, kv, i: (b, i, kv)),
        out_shape=jax.ShapeDtypeStruct((NB, T, W_A), BF),
        scratch_shapes=[pltpu.VMEM((2, tq, TT), F32), pltpu.VMEM((2, tq, TT), BF)],
        compiler_params=_cp(("parallel", "parallel", "arbitrary")),
        name="attention",
    )(qk, qk, pb)


def _mixout_kernel(of_ref, ob_ref, z_ref, a_ref, h_ref, gt_ref, gain_ref, w_ref, o_ref):
    parts = []
    for hd in range(NH):
        sl = slice(hd * HD, (hd + 1) * HD)
        o = of_ref[:, sl] + ob_ref[:, sl]
        n = o * lax.rsqrt(jnp.mean(o * o, axis=-1, keepdims=True) + EPS) * gain_ref[...]
        parts.append((n * _silu(z_ref[:, sl].astype(F32))).astype(BF))
    gd = jnp.concatenate(parts, axis=1)
    y = _dot(gd, w_ref[0:W_A, :]) + _dot(a_ref[...], w_ref[W_A:2 * W_A, :])
    o_ref[...] = h_ref[...] + _mrow(gt_ref, pl.program_id(0)) * y


def _mixout(o_f, o_b, pa, attn, h, mod, layer, gain, w_out, tm=256):
    return pl.pallas_call(
        _mixout_kernel,
        grid=(NB, T // tm),
        in_specs=[pl.BlockSpec((None, tm, W_A), lambda b, i: (b, i, 0)),
                  pl.BlockSpec((None, tm, W_A), lambda b, i: (b, i, 0)),
                  pl.BlockSpec((None, tm, W_A), lambda b, i: (b, i, COL_Z // W_A)),
                  pl.BlockSpec((None, tm, W_A), lambda b, i: (b, i, 0)),
                  pl.BlockSpec((None, tm, D), lambda b, i: (b, i, 0)),
                  _mod_spec(layer, 5),
                  pl.BlockSpec((1, HD), lambda b, i: (0, 0)),
                  pl.BlockSpec((D, D), lambda b, i: (0, 0))],
        out_specs=pl.BlockSpec((None, tm, D), lambda b, i: (b, i, 0)),
        out_shape=jax.ShapeDtypeStruct((NB, T, D), F32),
        compiler_params=_cp(("parallel", "parallel")),
        name="mix_out",
    )(o_f, o_b, pa, attn, h, mod, gain.reshape(1, HD), w_out)


def _dft_tables():
    def cs(n):
        idx = np.arange(n, dtype=np.int64)
        ang = 2.0 * np.pi * ((idx[:, None] * idx[None, :]) % n).astype(np.float64) / n
        return np.cos(ang) / math.sqrt(n), np.sin(ang) / math.sqrt(n)

    cc, sc = cs(GC)
    ct, st = cs(T)
    ch = np.concatenate([cc, sc], axis=1).astype(np.float32)
    tm = np.concatenate([ct, -st], axis=1).astype(np.float32)
    return jnp.asarray(ch, dtype=BF), jnp.asarray(tm, dtype=BF)


def _fchan_kernel(h_ref, sh_ref, sc_ref, gain_ref, w_ref, a_ref, b_ref):
    b = pl.program_id(0)
    u = _modnorm(h_ref[...], gain_ref[...], _mrow(sh_ref, b), _mrow(sc_ref, b)).astype(BF)
    w = w_ref[...]
    for g in range(FG):
        sl = slice(g * GC, (g + 1) * GC)
        ab = _dot(u[:, sl], w)
        a_ref[:, sl] = ab[:, :GC].astype(a_ref.dtype)
        b_ref[:, sl] = ab[:, GC:].astype(b_ref.dtype)


def _fourier_channels(h, mod, layer, gain, ch_tab, tm=512):
    return pl.pallas_call(
        _fchan_kernel,
        grid=(NB, T // tm),
        in_specs=[pl.BlockSpec((None, tm, D), lambda b, i: (b, i, 0)),
                  _mod_spec(layer, 3), _mod_spec(layer, 4),
                  pl.BlockSpec((1, D), lambda b, i: (0, 0)),
                  pl.BlockSpec((GC, 2 * GC), lambda b, i: (0, 0))],
        out_specs=[pl.BlockSpec((None, tm, D), lambda b, i: (b, i, 0))] * 2,
        out_shape=[jax.ShapeDtypeStruct((NB, T, D), BF)] * 2,
        compiler_params=_cp(("parallel", "parallel")),
        name="fourier_channels",
    )(h, mod, mod, gain.reshape(1, D), ch_tab)


def _ftime_kernel(t_ref, a_ref, b_ref, o_ref):
    y = _dot(t_ref[:, :T], a_ref[...]) + _dot(t_ref[:, T:], b_ref[...])
    o_ref[...] = y.astype(o_ref.dtype)


def _fourier_time(tm_tab, a, b, tm=1024, tn=1024):
    return pl.pallas_call(
        _ftime_kernel,
        grid=(NB, D // tn, T // tm),
        in_specs=[pl.BlockSpec((tm, 2 * T), lambda b, j, i: (i, 0)),
                  pl.BlockSpec((None, T, tn), lambda b, j, i: (b, 0, j)),
                  pl.BlockSpec((None, T, tn), lambda b, j, i: (b, 0, j))],
        out_specs=pl.BlockSpec((None, tm, tn), lambda b, j, i: (b, i, j)),
        out_shape=jax.ShapeDtypeStruct((NB, T, D), BF),
        compiler_params=_cp(("parallel", "parallel", "arbitrary")),
        name="fourier_time",
    )(tm_tab, a, b)


def _resmm_kernel(y_ref, w_ref, h_ref, gt_ref, o_ref):
    o_ref[...] = h_ref[...] + _mrow(gt_ref, pl.program_id(0)) * _dot(y_ref[...], w_ref[...])


def _fourier_out(y, w_f, h, mod, layer, tm=512):
    return pl.pallas_call(
        _resmm_kernel,
        grid=(NB, T // tm),
        in_specs=[pl.BlockSpec((None, tm, D), lambda b, i: (b, i, 0)),
                  pl.BlockSpec((D, D), lambda b, i: (0, 0)),
                  pl.BlockSpec((None, tm, D), lambda b, i: (b, i, 0)),
                  _mod_spec(layer, 5)],
        out_specs=pl.BlockSpec((None, tm, D), lambda b, i: (b, i, 0)),
        out_shape=jax.ShapeDtypeStruct((NB, T, D), F32),
        compiler_params=_cp(("parallel", "parallel")),
        name="fourier_out",
    )(y, w_f, h, mod)


def _rope_tables():
    half = HD // 4
    inv = ROPE_THETA ** (-np.arange(half, dtype=np.float64) / half)
    t = np.arange(T)
    ang_r = (t // GRID_W)[:, None] * inv[None, :]
    ang_c = (t % GRID_W)[:, None] * inv[None, :]
    cos = np.concatenate([np.cos(ang_r)] * 2 + [np.cos(ang_c)] * 2, axis=1)
    sin = np.concatenate([-np.sin(ang_r), np.sin(ang_r), -np.sin(ang_c), np.sin(ang_c)], axis=1)
    cos = np.concatenate([cos, np.ones((TC, HD))], axis=0).astype(np.float32)
    sin = np.concatenate([sin, np.zeros((TC, HD))], axis=0).astype(np.float32)
    return jnp.asarray(cos), jnp.asarray(sin)


def kernel(x, c, ctx, c_ctx, w_mod, b_mod, norm_w, ffn_w_gu, ffn_w_down, w_in, conv_w, a_log, dt_bias,
           gdn_norm, q_norm, k_norm, w_out, w_fourier, final_norm):
    cvec = jnp.concatenate([c, c_ctx[None, :], jnp.zeros((8 - NB - 1, D), F32)], axis=0)
    mod = _modulation(cvec, w_mod, b_mod)
    w_gu = ffn_w_gu[0, 0].astype(BF)
    w_dn = ffn_w_down[0, 0].astype(BF)

    h_ctx = _ffn(ctx.reshape(1, NB * TC, D), mod, 0, 0, NB, norm_w[0, 0], w_gu, w_dn).reshape(NB, TC, D)
    h, w_gu, w_dn = _ffn(x, mod, 0, 0, 0, norm_w[0, 0], w_gu, w_dn, nxt=(ffn_w_gu, ffn_w_down, 0, 1))

    wi = w_in[0]
    w_a = wi[:, :NPA].astype(BF)
    w_b = jnp.concatenate([wi[:, NPA + 4 * NH:].astype(BF), wi[:, NPA:NPA + 4 * NH].astype(BF),
                           jnp.zeros((D, 128 - 4 * NH), BF)], axis=1)
    u = _prenorm(h, h_ctx, mod, 0, norm_w[0, 1]).reshape(NB * TT, D)
    pa = _matmul(u, w_a, 1024, 2048, BF, "in_proj_a").reshape(NB, TT, NPA)
    pb, gates = _matmul_split(u, w_b, 1024, NPB, "in_proj_b")
    pb = pb.reshape(NB, TT, NPB)

    qkv = _gdn_conv(pa, conv_w[0])
    prm = jnp.zeros((8, 128), F32)
    prm = prm.at[0, :2 * NH].set(a_log[0].reshape(-1)).at[1, :2 * NH].set(dt_bias[0].reshape(-1))
    o_f, o_b = _gdn_scan(qkv, gates.reshape(NB, TT, 128), prm)

    gains = jnp.concatenate([jnp.broadcast_to(q_norm[0] * (HD ** -0.5), (NH, HD)),
                             jnp.broadcast_to(k_norm[0], (KVH, HD))], axis=0).reshape(NH + KVH, 1, HD)
    cos_t, sin_t = _rope_tables()
    qk = _rope_norm(pb, gains, cos_t, sin_t)
    attn = _attention(qk, pb)

    h = _mixout(o_f, o_b, pa, attn, h, mod, 0, gdn_norm[0], w_out[0].astype(BF))
    h, w_gu, w_dn = _ffn(h, mod, 0, 6, 0, norm_w[0, 2], w_gu, w_dn, nxt=(ffn_w_gu, ffn_w_down, 1, 0))

    h, w_gu, w_dn = _ffn(h, mod, 1, 0, 0, norm_w[1, 0], w_gu, w_dn, nxt=(ffn_w_gu, ffn_w_down, 1, 1))
    ch_tab, tm_tab = _dft_tables()
    fa, fb = _fourier_channels(h, mod, 1, norm_w[1, 1], ch_tab)
    y = _fourier_time(tm_tab, fa, fb)
    h = _fourier_out(y, w_fourier[0].astype(BF), h, mod, 1)
    return _ffn(h, mod, 1, 6, 0, norm_w[1, 2], w_gu, w_dn, final_gain=final_norm)
```

```python
import functools
import math

import numpy as np
import jax
import jax.numpy as jnp
from jax import lax
from jax.experimental import pallas as pl
from jax.experimental.pallas import tpu as pltpu

D = 2048
NB = 4
T = 2048
GRID_W = 64
TC = 256
TT = T + TC
DEPTH = 2
EPS = 1e-6
N_MOD = 9
FH = 5632
NH = 8
HD = 128
KVH = 2
GQ = NH // KVH
CONV_K = 5
CS = 64
NCH = TT // CS
NCH_LAT = T // CS
ROPE_THETA = 10000.0
FG = 4
GC = D // FG
W_A = NH * HD
NPA = 4 * W_A
COL_Z = 3 * W_A
NPB = W_A + 2 * KVH * HD
COL_KB = W_A
COL_VB = COL_KB + KVH * HD

VMEM_LIMIT = 58 * 1024 * 1024
BF = jnp.bfloat16
F32 = jnp.float32


def _cp(sem):
    return pltpu.CompilerParams(dimension_semantics=sem, vmem_limit_bytes=VMEM_LIMIT)


def _dot(a, b):
    return jnp.dot(a, b, preferred_element_type=F32)


def _dot_nt(a, b):
    return lax.dot_general(a, b, (((1,), (1,)), ((), ())), preferred_element_type=F32)


def _dot_tn(a, b):
    return lax.dot_general(a, b, (((0,), (0,)), ((), ())), preferred_element_type=F32)


def _silu(x):
    return x * jax.nn.sigmoid(x)


def _modnorm(x, gain, shift, scale):
    return x * lax.rsqrt(jnp.mean(x * x, axis=-1, keepdims=True) + EPS) * (gain * (1.0 + scale)) + shift


def _mod_kernel(c_ref, w_ref, b_ref, o_ref):
    a = _silu(c_ref[...]).astype(BF)
    o_ref[...] = _dot(a, w_ref[...].astype(BF)) + b_ref[...]


def _modulation(cvec, w_mod, b_mod):
    tn = 1024
    n = N_MOD * D
    per = D // tn
    return pl.pallas_call(
        _mod_kernel,
        grid=(DEPTH, n // tn),
        in_specs=[pl.BlockSpec((8, D), lambda l, j: (0, 0)),
                  pl.BlockSpec((None, D, tn), lambda l, j: (l, 0, j)),
                  pl.BlockSpec((None, 1, tn), lambda l, j: (l, 0, j))],
        out_specs=pl.BlockSpec((None, None, 8, tn), lambda l, j: (l, j // per, 0, j % per)),
        out_shape=jax.ShapeDtypeStruct((DEPTH, N_MOD, 8, D), F32),
        compiler_params=_cp(("parallel", "arbitrary")),
        name="modulation",
    )(cvec, w_mod, b_mod.reshape(DEPTH, 1, n))


def _mod_spec(layer, k):
    return pl.BlockSpec((None, None, 8, D), lambda *_: (layer, k, 0, 0))


def _mrow(ref, r):
    return ref[pl.ds(r, 1), :]


def _ffn_kernel(h_ref, sh_ref, sc_ref, gt_ref, gain_ref, wg_ref, wu_ref, wd_ref, *rest, nj, row0, final, convert):
    rest = list(rest)
    fin_ref = rest.pop(0) if final else None
    if convert:
        ngu_ref, ndn_ref = rest.pop(0), rest.pop(0)
    o_ref = rest.pop(0)
    if convert:
        ogu_ref, odn_ref = rest.pop(0), rest.pop(0)
    u_ref = rest.pop(0)
    r = pl.program_id(0) + row0
    j = pl.program_id(2)

    @pl.when(j == 0)
    def _():
        u = _modnorm(h_ref[...], gain_ref[...], _mrow(sh_ref, r), _mrow(sc_ref, r))
        u_ref[...] = u.astype(BF)
        o_ref[...] = jnp.zeros_like(o_ref)

    u = u_ref[...]
    g = _dot(u, wg_ref[...])
    up = _dot(u, wu_ref[...])
    act = (_silu(g) * up).astype(BF)
    o_ref[...] += _dot(act, wd_ref[...])
    if convert:
        ogu_ref[...] = ngu_ref[...].astype(BF)
        odn_ref[...] = ndn_ref[...].astype(BF)

    @pl.when(j == nj - 1)
    def _():
        res = h_ref[...] + 0.5 * _mrow(gt_ref, r) * o_ref[...]
        if final:
            res = res * lax.rsqrt(jnp.mean(res * res, axis=-1, keepdims=True) + EPS) * fin_ref[...]
        o_ref[...] = res


def _ffn(h, mod, layer, base, mod_row0, gain, w_gu, w_down, final_gain=None, nxt=None, tm=1024, tf=256):
    nb, rows, _ = h.shape
    nj = FH // tf
    ni = rows // tm
    vec = pl.BlockSpec((1, D), lambda b, i, j: (0, 0))
    in_specs = [pl.BlockSpec((None, tm, D), lambda b, i, j: (b, i, 0)),
                _mod_spec(layer, base), _mod_spec(layer, base + 1), _mod_spec(layer, base + 2), vec,
                pl.BlockSpec((D, tf), lambda b, i, j: (0, j)),
                pl.BlockSpec((D, tf), lambda b, i, j: (0, j + nj)),
                pl.BlockSpec((tf, D), lambda b, i, j: (j, 0))]
    args = [h, mod, mod, mod, gain.reshape(1, D), w_gu, w_gu, w_down]
    out_specs = [pl.BlockSpec((None, tm, D), lambda b, i, j: (b, i, 0))]
    out_shape = [jax.ShapeDtypeStruct(h.shape, F32)]
    if final_gain is not None:
        in_specs.append(vec)
        args.append(final_gain.reshape(1, D))
    if nxt is not None:
        n_gu, n_dn, nl, nw = nxt
        nr = nb * ni
        gu_blk = (D // nr, 2 * FH // nj)
        dn_blk = (FH // (nr * nj), D)
        gu_idx = lambda b, i, j: (b * ni + i, j)
        dn_idx = lambda b, i, j: ((b * ni + i) * nj + j, 0)
        in_specs += [pl.BlockSpec((None, None) + gu_blk, lambda b, i, j: (nl, nw) + gu_idx(b, i, j)),
                     pl.BlockSpec((None, None) + dn_blk, lambda b, i, j: (nl, nw) + dn_idx(b, i, j))]
        args += [n_gu, n_dn]
        out_specs += [pl.BlockSpec(gu_blk, gu_idx), pl.BlockSpec(dn_blk, dn_idx)]
        out_shape += [jax.ShapeDtypeStruct((D, 2 * FH), BF), jax.ShapeDtypeStruct((FH, D), BF)]
    out = pl.pallas_call(
        functools.partial(_ffn_kernel, nj=nj, row0=mod_row0, final=final_gain is not None, convert=nxt is not None),
        grid=(nb, ni, nj),
        in_specs=in_specs,
        out_specs=out_specs,
        out_shape=out_shape,
        scratch_shapes=[pltpu.VMEM((tm, D), BF)],
        compiler_params=_cp(("parallel", "parallel", "arbitrary")),
        name="ffn",
    )(*args)
    return out if nxt is not None else out[0]


def _prenorm_kernel(hl_ref, hc_ref, sh_ref, sc_ref, gain_ref, o_ref):
    b = pl.program_id(0)
    i = pl.program_id(1)

    @pl.when(i < T // TC)
    def _():
        o_ref[...] = _modnorm(hl_ref[...], gain_ref[...], _mrow(sh_ref, b), _mrow(sc_ref, b)).astype(BF)

    @pl.when(i >= T // TC)
    def _():
        o_ref[...] = _modnorm(hc_ref[...], gain_ref[...], _mrow(sh_ref, NB), _mrow(sc_ref, NB)).astype(BF)


def _prenorm(h, h_ctx, mod, layer, gain):
    nlat = T // TC
    return pl.pallas_call(
        _prenorm_kernel,
        grid=(NB, TT // TC),
        in_specs=[pl.BlockSpec((None, TC, D), lambda b, i: (b, jnp.minimum(i, nlat - 1), 0)),
                  pl.BlockSpec((None, TC, D), lambda b, i: (b, 0, 0)),
                  _mod_spec(layer, 3), _mod_spec(layer, 4),
                  pl.BlockSpec((1, D), lambda b, i: (0, 0))],
        out_specs=pl.BlockSpec((None, TC, D), lambda b, i: (b, i, 0)),
        out_shape=jax.ShapeDtypeStruct((NB, TT, D), BF),
        compiler_params=_cp(("parallel", "arbitrary")),
        name="prenorm",
    )(h, h_ctx, mod, mod, gain.reshape(1, D))


def _mm_kernel(a_ref, b_ref, o_ref):
    o_ref[...] = _dot(a_ref[...], b_ref[...]).astype(o_ref.dtype)


def _mm_split_kernel(a_ref, b_ref, o1_ref, o2_ref):
    r = _dot(a_ref[...], b_ref[...])
    n1 = o1_ref.shape[1]
    o1_ref[...] = r[:, :n1].astype(o1_ref.dtype)
    o2_ref[...] = r[:, n1:]


def _matmul_split(a, w, tm, n1, name):
    m, k = a.shape
    n = w.shape[1]
    return pl.pallas_call(
        _mm_split_kernel,
        grid=(m // tm,),
        in_specs=[pl.BlockSpec((tm, k), lambda i: (i, 0)),
                  pl.BlockSpec((k, n), lambda i: (0, 0))],
        out_specs=[pl.BlockSpec((tm, n1), lambda i: (i, 0)),
                   pl.BlockSpec((tm, n - n1), lambda i: (i, 0))],
        out_shape=[jax.ShapeDtypeStruct((m, n1), BF), jax.ShapeDtypeStruct((m, n - n1), F32)],
        compiler_params=_cp(("parallel",)),
        name=name,
    )(a, w)


def _matmul(a, w, tm, tn, out_dtype, name):
    m, k = a.shape
    n = w.shape[1]
    return pl.pallas_call(
        _mm_kernel,
        grid=(m // tm, n // tn),
        in_specs=[pl.BlockSpec((tm, k), lambda i, j: (i, 0)),
                  pl.BlockSpec((k, tn), lambda i, j: (0, j))],
        out_specs=pl.BlockSpec((tm, tn), lambda i, j: (i, j)),
        out_shape=jax.ShapeDtypeStruct((m, n), out_dtype),
        compiler_params=_cp(("parallel", "arbitrary")),
        name=name,
    )(a, w)


def _conv_kernel(x_ref, w_ref, o_ref, pad_ref):
    j = pl.program_id(1)
    half = CONV_K // 2
    for r0, n in ((0, T), (T, TC)):
        pad_ref[0:8, :] = jnp.zeros((8, HD), F32)
        pad_ref[8:8 + n, :] = x_ref[r0:r0 + n, :].astype(F32)
        pad_ref[8 + n:16 + n, :] = jnp.zeros((8, HD), F32)
        acc = jnp.zeros((n, HD), F32)
        for k in range(CONV_K):
            acc = acc + w_ref[k:k + 1, :] * pad_ref[8 + k - half:8 + k - half + n, :]
        y = _silu(acc)
        yn = y * lax.rsqrt(jnp.sum(y * y, axis=-1, keepdims=True) + EPS)
        o_ref[r0:r0 + n, :] = jnp.where(j < 2 * NH, yn, y).astype(o_ref.dtype)


def _gdn_conv(p, conv_w):
    return pl.pallas_call(
        _conv_kernel,
        grid=(NB, 3 * NH),
        in_specs=[pl.BlockSpec((None, TT, HD), lambda b, j: (b, 0, j)),
                  pl.BlockSpec((CONV_K, HD), lambda b, j: (0, j))],
        out_specs=pl.BlockSpec((None, TT, HD), lambda b, j: (b, 0, j)),
        out_shape=jax.ShapeDtypeStruct((NB, TT, 3 * W_A), BF),
        scratch_shapes=[pltpu.VMEM((T + 16, HD), F32)],
        compiler_params=_cp(("parallel", "parallel")),
        name="gdn_conv",
    )(p, conv_w)


def _split3(x):
    a = x.astype(BF)
    r = x - a.astype(F32)
    b = r.astype(BF)
    return a, b, (r - b.astype(F32)).astype(BF)


def _gdn_kernel(qf_ref, kf_ref, vf_ref, gf_ref, qb_ref, kb_ref, vb_ref, gb_ref, prm_ref, of_ref, ob_ref, s_ref):
    nbs = qf_ref.shape[0]
    @pl.when(pl.program_id(1) == 0)
    def _():
        s_ref[...] = jnp.zeros_like(s_ref)

    ri = lax.broadcasted_iota(jnp.int32, (CS, CS), 0)
    ci = lax.broadcasted_iota(jnp.int32, (CS, CS), 1)
    eye_f = jnp.where(ri == ci, 1.0, 0.0)
    pair = (ri // 2) == (ci // 2)
    levels = [((ri // (2 * bs)) == (ci // (2 * bs))) & ((ri // bs) != (ci // bs)) for bs in (2, 4, 8, 16, 32)]
    prm = prm_ref[...]
    a_neg = -jnp.exp(prm[0:1, :])
    dt_b = prm[1:2, :]

    ch = []
    dir_refs = ((qf_ref, kf_ref, vf_ref, gf_ref, of_ref), (qb_ref, kb_ref, vb_ref, gb_ref, ob_ref))
    for bb, d in [(bb, d) for bb in range(nbs) for d in range(2)]:
        q_ref, k_ref, v_ref, g_ref, o_ref = [r.at[bb] for r in dir_refs[d]]
        dm = (ri - ci) if d == 0 else (ci - ri)
        incl = dm >= 0
        strict = dm > 0
        tri = jnp.where(incl, 1.0, 0.0).astype(BF)
        tri_t = jnp.where(dm <= 0, 1.0, 0.0).astype(BF)
        gts = g_ref[...]
        x = gts + dt_b
        glog = a_neg * (jnp.maximum(x, 0.0) + jnp.log(1.0 + jnp.exp(-jnp.abs(x))))
        beta = jax.nn.sigmoid(gts)
        parts = _split3(glog)
        gc = sum(_dot(tri, g) for g in parts)
        gc_t = sum(_dot_tn(g, tri_t) for g in parts)
        last = CS - 1 if d == 0 else 0
        g_last = gc[last:last + 1, :]
        eg = jnp.exp(gc)
        ek = jnp.exp(g_last - gc)
        g_tot = jnp.exp(g_last)
        for h in range(NH):
            c = d * NH + h
            ch.append(dict(sl=slice(h * HD, (h + 1) * HD), c=bb * 2 * NH + c, incl=incl, strict=strict, o_ref=o_ref,
                           q_ref=q_ref, k_ref=k_ref, v_ref=v_ref,
                           b_c=beta[:, 2 * NH + c:2 * NH + c + 1], gc_c=gc[:, c:c + 1], gc_r=gc_t[c:c + 1, :],
                           eg=eg[:, c:c + 1], ek=ek[:, c:c + 1], g_tot=g_tot[:, c:c + 1]))

    for x in ch:
        kb = x["k_ref"][:, x["sl"]]
        qh = x["q_ref"][:, x["sl"]].astype(F32) * (HD ** -0.5)
        x["kh"], x["qh"] = kb.astype(F32), qh
        kq = _dot_nt(jnp.concatenate([kb, qh.astype(BF)], axis=0), kb)
        x["kk"], x["qk"] = kq[:CS], kq[CS:]
    for x in ch:
        decay = jnp.exp(jnp.where(x["incl"], x["gc_c"] - x["gc_r"], -1e30))
        lower = jnp.where(x["strict"], x["b_c"] * x["kk"] * decay, 0.0)
        x["qk"] = (x["qk"] * decay).astype(BF)
        x["lower"] = lower.astype(BF)
        x["t"] = eye_f - jnp.where(pair, lower, 0.0)
    for off in levels:
        for x in ch:
            x["tb"] = x["t"].astype(BF)
            x["mt"] = _dot(jnp.where(off, x["lower"], jnp.zeros_like(x["lower"])), x["tb"]).astype(BF)
        for x in ch:
            x["t"] = x["t"] - _dot(x["tb"], x["mt"])
    for x in ch:
        vh = x["v_ref"][:, x["sl"]].astype(F32)
        rhs = jnp.concatenate([vh * x["b_c"], x["kh"] * (x["b_c"] * x["eg"])], axis=1).astype(BF)
        x["uw"] = _dot(x["t"].astype(BF), rhs)
        x["q_dec"] = (x["qh"] * x["eg"]).astype(BF)
        x["k_dec"] = (x["kh"] * x["ek"]).astype(BF)
    for x in ch:
        st = s_ref[x["c"]]
        x["st"] = st
        stb = st.astype(BF)
        ws = _dot(jnp.concatenate([x["uw"][:, HD:].astype(BF), x["q_dec"]], axis=0), stb)
        x["vn"] = (x["uw"][:, :HD] - ws[:CS]).astype(BF)
        x["qs"] = ws[CS:]
    for x in ch:
        x["o_ref"][:, x["sl"]] = x["qs"] + _dot(x["qk"], x["vn"])
        s_ref[x["c"]] = x["st"] * x["g_tot"] + _dot_tn(x["k_dec"], x["vn"])


def _gdn_scan(qkv, gates, prm, nbs=4):
    chunk_f = lambda s: (s + NCH_LAT) % NCH
    chunk_b = lambda s: NCH - 1 - s

    def specs(chunk):
        qkv_spec = lambda c: pl.BlockSpec((nbs, CS, W_A), lambda b, s: (b, chunk(s), c))
        return [qkv_spec(0), qkv_spec(1), qkv_spec(2),
                pl.BlockSpec((nbs, CS, 128), lambda b, s: (b, chunk(s), 0))]

    return pl.pallas_call(
        _gdn_kernel,
        grid=(NB // nbs, NCH),
        in_specs=specs(chunk_f) + specs(chunk_b) + [pl.BlockSpec((8, 128), lambda b, s: (0, 0))],
        out_specs=[pl.BlockSpec((nbs, CS, W_A), lambda b, s: (b, chunk_f(s), 0)),
                   pl.BlockSpec((nbs, CS, W_A), lambda b, s: (b, chunk_b(s), 0))],
        out_shape=[jax.ShapeDtypeStruct((NB, TT, W_A), F32)] * 2,
        scratch_shapes=[pltpu.VMEM((nbs * 2 * NH, HD, HD), F32)],
        compiler_params=_cp(("parallel", "arbitrary")),
        name="gdn_scan",
    )(qkv, qkv, qkv, gates, qkv, qkv, qkv, gates, prm)


def _rope_kernel(x_ref, g_ref, cos_ref, sin_ref, o_ref):
    x = x_ref[...].astype(F32)
    y = x * lax.rsqrt(jnp.mean(x * x, axis=-1, keepdims=True) + EPS) * g_ref[...]
    lane = lax.broadcasted_iota(jnp.int32, y.shape, 1)
    partner = jnp.where((lane % 64) < 32, pltpu.roll(y, 128 - 32, axis=1), pltpu.roll(y, 32, axis=1))
    o_ref[...] = (y * cos_ref[...] + partner * sin_ref[...]).astype(o_ref.dtype)


def _rope_norm(pb, gains, cos_t, sin_t):
    nh = NH + KVH
    return pl.pallas_call(
        _rope_kernel,
        grid=(NB, nh),
        in_specs=[pl.BlockSpec((None, TT, HD), lambda b, j: (b, 0, j)),
                  pl.BlockSpec((None, 1, HD), lambda b, j: (j, 0, 0)),
                  pl.BlockSpec((TT, HD), lambda b, j: (0, 0)),
                  pl.BlockSpec((TT, HD), lambda b, j: (0, 0))],
        out_specs=pl.BlockSpec((None, TT, HD), lambda b, j: (b, 0, j)),
        out_shape=jax.ShapeDtypeStruct((NB, TT, nh * HD), BF),
        compiler_params=_cp(("parallel", "parallel")),
        name="rope_norm",
    )(pb, gains, cos_t, sin_t)


def _attn_kernel(q_ref, k_ref, v_ref, o_ref, s_ref, p_ref):
    k = k_ref[...]
    v_aug = jnp.concatenate([v_ref[...], jnp.ones((TT, HD), BF)], axis=1)
    tq = q_ref.shape[0]
    rb = 16
    for g in range(GQ):
        sl = slice(g * HD, (g + 1) * HD)
        buf = g % 2
        s_ref[buf] = _dot_nt(q_ref[:, sl], k)
        for r in range(tq // rb):
            rows = slice(r * rb, (r + 1) * rb)
            m = s_ref[buf, rows, 0:HD]
            for c in range(1, TT // HD):
                m = jnp.maximum(m, s_ref[buf, rows, c * HD:(c + 1) * HD])
            mx = jnp.broadcast_to(jnp.max(m, axis=-1, keepdims=True), (rb, HD))
            for c in range(TT // HD):
                cols = slice(c * HD, (c + 1) * HD)
                p_ref[buf, rows, cols] = jnp.exp(s_ref[buf, rows, cols] - mx).astype(BF)
        o = _dot(p_ref[buf], v_aug)
        o_ref[:, sl] = (o[:, :HD] / o[:, HD:]).astype(o_ref.dtype)


def _attention(qk, pb, tq=512):
    return pl.pallas_call(
        _attn_kernel,
        grid=(NB, KVH, T // tq),
        in_specs=[pl.BlockSpec((None, tq, GQ * HD), lambda b, kv, i: (b, i, kv)),
                  pl.BlockSpec((None, TT, HD), lambda b, kv, i: (b, 0, NH + kv)),
                  pl.BlockSpec((None, TT, HD), lambda b, kv, i: (b, 0, COL_VB // HD + kv))],
        out_specs=pl.BlockSpec((None, tq, GQ * HD), lambda b, kv, i: (b, i, kv)),
        out_shape=jax.ShapeDtypeStruct((NB, T, W_A), BF),
        scratch_shapes=[pltpu.VMEM((2, tq, TT), F32), pltpu.VMEM((2, tq, TT), BF)],
        compiler_params=_cp(("parallel", "parallel", "arbitrary")),
        name="attention",
    )(qk, qk, pb)


def _mixout_kernel(of_ref, ob_ref, z_ref, a_ref, h_ref, gt_ref, gain_ref, w_ref, o_ref):
    parts = []
    for hd in range(NH):
        sl = slice(hd * HD, (hd + 1) * HD)
        o = of_ref[:, sl] + ob_ref[:, sl]
        n = o * lax.rsqrt(jnp.mean(o * o, axis=-1, keepdims=True) + EPS) * gain_ref[...]
        parts.append((n * _silu(z_ref[:, sl].astype(F32))).astype(BF))
    gd = jnp.concatenate(parts, axis=1)
    y = _dot(gd, w_ref[0:W_A, :]) + _dot(a_ref[...], w_ref[W_A:2 * W_A, :])
    o_ref[...] = h_ref[...] + _mrow(gt_ref, pl.program_id(0)) * y


def _mixout(o_f, o_b, pa, attn, h, mod, layer, gain, w_out, tm=256):
    return pl.pallas_call(
        _mixout_kernel,
        grid=(NB, T // tm),
        in_specs=[pl.BlockSpec((None, tm, W_A), lambda b, i: (b, i, 0)),
                  pl.BlockSpec((None, tm, W_A), lambda b, i: (b, i, 0)),
                  pl.BlockSpec((None, tm, W_A), lambda b, i: (b, i, COL_Z // W_A)),
                  pl.BlockSpec((None, tm, W_A), lambda b, i: (b, i, 0)),
                  pl.BlockSpec((None, tm, D), lambda b, i: (b, i, 0)),
                  _mod_spec(layer, 5),
                  pl.BlockSpec((1, HD), lambda b, i: (0, 0)),
                  pl.BlockSpec((D, D), lambda b, i: (0, 0))],
        out_specs=pl.BlockSpec((None, tm, D), lambda b, i: (b, i, 0)),
        out_shape=jax.ShapeDtypeStruct((NB, T, D), F32),
        compiler_params=_cp(("parallel", "parallel")),
        name="mix_out",
    )(o_f, o_b, pa, attn, h, mod, gain.reshape(1, HD), w_out)


def _dft_tables():
    def cs(n):
        idx = np.arange(n, dtype=np.int64)
        ang = 2.0 * np.pi * ((idx[:, None] * idx[None, :]) % n).astype(np.float64) / n
        return np.cos(ang) / math.sqrt(n), np.sin(ang) / math.sqrt(n)

    cc, sc = cs(GC)
    ct, st = cs(T)
    ch = np.concatenate([cc, sc], axis=1).astype(np.float32)
    tm = np.concatenate([ct, -st], axis=1).astype(np.float32)
    return jnp.asarray(ch, dtype=BF), jnp.asarray(tm, dtype=BF)


def _fchan_kernel(h_ref, sh_ref, sc_ref, gain_ref, w_ref, a_ref, b_ref):
    b = pl.program_id(0)
    u = _modnorm(h_ref[...], gain_ref[...], _mrow(sh_ref, b), _mrow(sc_ref, b)).astype(BF)
    w = w_ref[...]
    for g in range(FG):
        sl = slice(g * GC, (g + 1) * GC)
        ab = _dot(u[:, sl], w)
        a_ref[:, sl] = ab[:, :GC].astype(a_ref.dtype)
        b_ref[:, sl] = ab[:, GC:].astype(b_ref.dtype)


def _fourier_channels(h, mod, layer, gain, ch_tab, tm=512):
    return pl.pallas_call(
        _fchan_kernel,
        grid=(NB, T // tm),
        in_specs=[pl.BlockSpec((None, tm, D), lambda b, i: (b, i, 0)),
                  _mod_spec(layer, 3), _mod_spec(layer, 4),
                  pl.BlockSpec((1, D), lambda b, i: (0, 0)),
                  pl.BlockSpec((GC, 2 * GC), lambda b, i: (0, 0))],
        out_specs=[pl.BlockSpec((None, tm, D), lambda b, i: (b, i, 0))] * 2,
        out_shape=[jax.ShapeDtypeStruct((NB, T, D), BF)] * 2,
        compiler_params=_cp(("parallel", "parallel")),
        name="fourier_channels",
    )(h, mod, mod, gain.reshape(1, D), ch_tab)


def _ftime_kernel(t_ref, a_ref, b_ref, o_ref):
    y = _dot(t_ref[:, :T], a_ref[...]) + _dot(t_ref[:, T:], b_ref[...])
    o_ref[...] = y.astype(o_ref.dtype)


def _fourier_time(tm_tab, a, b, tm=1024, tn=1024):
    return pl.pallas_call(
        _ftime_kernel,
        grid=(NB, D // tn, T // tm),
        in_specs=[pl.BlockSpec((tm, 2 * T), lambda b, j, i: (i, 0)),
                  pl.BlockSpec((None, T, tn), lambda b, j, i: (b, 0, j)),
                  pl.BlockSpec((None, T, tn), lambda b, j, i: (b, 0, j))],
        out_specs=pl.BlockSpec((None, tm, tn), lambda b, j, i: (b, i, j)),
        out_shape=jax.ShapeDtypeStruct((NB, T, D), BF),
        compiler_params=_cp(("parallel", "parallel", "arbitrary")),
        name="fourier_time",
    )(tm_tab, a, b)


def _resmm_kernel(y_ref, w_ref, h_ref, gt_ref, o_ref):
    o_ref[...] = h_ref[...] + _mrow(gt_ref, pl.program_id(0)) * _dot(y_ref[...], w_ref[...])


def _fourier_out(y, w_f, h, mod, layer, tm=512):
    return pl.pallas_call(
        _resmm_kernel,
        grid=(NB, T // tm),
        in_specs=[pl.BlockSpec((None, tm, D), lambda b, i: (b, i, 0)),
                  pl.BlockSpec((D, D), lambda b, i: (0, 0)),
                  pl.BlockSpec((None, tm, D), lambda b, i: (b, i, 0)),
                  _mod_spec(layer, 5)],
        out_specs=pl.BlockSpec((None, tm, D), lambda b, i: (b, i, 0)),
        out_shape=jax.ShapeDtypeStruct((NB, T, D), F32),
        compiler_params=_cp(("parallel", "parallel")),
        name="fourier_out",
    )(y, w_f, h, mod)


def _rope_tables():
    half = HD // 4
    inv = ROPE_THETA ** (-np.arange(half, dtype=np.float64) / half)
    t = np.arange(T)
    ang_r = (t // GRID_W)[:, None] * inv[None, :]
    ang_c = (t % GRID_W)[:, None] * inv[None, :]
    cos = np.concatenate([np.cos(ang_r)] * 2 + [np.cos(ang_c)] * 2, axis=1)
    sin = np.concatenate([-np.sin(ang_r), np.sin(ang_r), -np.sin(ang_c), np.sin(ang_c)], axis=1)
    cos = np.concatenate([cos, np.ones((TC, HD))], axis=0).astype(np.float32)
    sin = np.concatenate([sin, np.zeros((TC, HD))], axis=0).astype(np.float32)
    return jnp.asarray(cos), jnp.asarray(sin)


def kernel(x, c, ctx, c_ctx, w_mod, b_mod, norm_w, ffn_w_gu, ffn_w_down, w_in, conv_w, a_log, dt_bias,
           gdn_norm, q_norm, k_norm, w_out, w_fourier, final_norm):
    cvec = jnp.concatenate([c, c_ctx[None, :], jnp.zeros((8 - NB - 1, D), F32)], axis=0)
    mod = _modulation(cvec, w_mod, b_mod)
    w_gu = ffn_w_gu[0, 0].astype(BF)
    w_dn = ffn_w_down[0, 0].astype(BF)

    h_ctx = _ffn(ctx.reshape(1, NB * TC, D), mod, 0, 0, NB, norm_w[0, 0], w_gu, w_dn).reshape(NB, TC, D)
    h, w_gu, w_dn = _ffn(x, mod, 0, 0, 0, norm_w[0, 0], w_gu, w_dn, nxt=(ffn_w_gu, ffn_w_down, 0, 1))

    wi = w_in[0]
    w_a = wi[:, :NPA].astype(BF)
    w_b = jnp.concatenate([wi[:, NPA + 4 * NH:].astype(BF), wi[:, NPA:NPA + 4 * NH].astype(BF),
                           jnp.zeros((D, 128 - 4 * NH), BF)], axis=1)
    u = _prenorm(h, h_ctx, mod, 0, norm_w[0, 1]).reshape(NB * TT, D)
    pa = _matmul(u, w_a, 1024, 2048, BF, "in_proj_a").reshape(NB, TT, NPA)
    pb, gates = _matmul_split(u, w_b, 1024, NPB, "in_proj_b")
    pb = pb.reshape(NB, TT, NPB)

    qkv = _gdn_conv(pa, conv_w[0])
    prm = jnp.zeros((8, 128), F32)
    prm = prm.at[0, :2 * NH].set(a_log[0].reshape(-1)).at[1, :2 * NH].set(dt_bias[0].reshape(-1))
    o_f, o_b = _gdn_scan(qkv, gates.reshape(NB, TT, 128), prm)

    gains = jnp.concatenate([jnp.broadcast_to(q_norm[0] * (HD ** -0.5), (NH, HD)),
                             jnp.broadcast_to(k_norm[0], (KVH, HD))], axis=0).reshape(NH + KVH, 1, HD)
    cos_t, sin_t = _rope_tables()
    qk = _rope_norm(pb, gains, cos_t, sin_t)
    attn = _attention(qk, pb)

    h = _mixout(o_f, o_b, pa, attn, h, mod, 0, gdn_norm[0], w_out[0].astype(BF))
    h, w_gu, w_dn = _ffn(h, mod, 0, 6, 0, norm_w[0, 2], w_gu, w_dn, nxt=(ffn_w_gu, ffn_w_down, 1, 0))

    h, w_gu, w_dn = _ffn(h, mod, 1, 0, 0, norm_w[1, 0], w_gu, w_dn, nxt=(ffn_w_gu, ffn_w_down, 1, 1))
    ch_tab, tm_tab = _dft_tables()
    fa, fb = _fourier_channels(h, mod, 1, norm_w[1, 1], ch_tab)
    y = _fourier_time(tm_tab, fa, fb)
    h = _fourier_out(y, w_fourier[0].astype(BF), h, mod, 1)
    return _ffn(h, mod, 1, 6, 0, norm_w[1, 2], w_gu, w_dn, final_gain=final_norm)
```

```python
import functools
import math

import numpy as np
import jax
import jax.numpy as jnp
from jax import lax
from jax.experimental import pallas as pl
from jax.experimental.pallas import tpu as pltpu

D = 2048
NB = 4
T = 2048
GRID_W = 64
TC = 256
TT = T + TC
DEPTH = 2
EPS = 1e-6
N_MOD = 9
FH = 5632
NH = 8
HD = 128
KVH = 2
GQ = NH // KVH
CONV_K = 5
CS = 64
NCH = TT // CS
NCH_LAT = T // CS
ROPE_THETA = 10000.0
FG = 4
GC = D // FG
W_A = NH * HD
NPA = 4 * W_A
COL_Z = 3 * W_A
NPB = W_A + 2 * KVH * HD
COL_KB = W_A
COL_VB = COL_KB + KVH * HD

VMEM_LIMIT = 58 * 1024 * 1024
BF = jnp.bfloat16
F32 = jnp.float32


def _cp(sem):
    return pltpu.CompilerParams(dimension_semantics=sem, vmem_limit_bytes=VMEM_LIMIT)


def _dot(a, b):
    return jnp.dot(a, b, preferred_element_type=F32)


def _dot_nt(a, b):
    return lax.dot_general(a, b, (((1,), (1,)), ((), ())), preferred_element_type=F32)


def _dot_tn(a, b):
    return lax.dot_general(a, b, (((0,), (0,)), ((), ())), preferred_element_type=F32)


def _silu(x):
    return x * jax.nn.sigmoid(x)


def _modnorm(x, gain, shift, scale):
    return x * lax.rsqrt(jnp.mean(x * x, axis=-1, keepdims=True) + EPS) * (gain * (1.0 + scale)) + shift


def _mod_kernel(c_ref, w_ref, b_ref, o_ref):
    a = _silu(c_ref[...]).astype(BF)
    o_ref[...] = _dot(a, w_ref[...].astype(BF)) + b_ref[...]


def _modulation(cvec, w_mod, b_mod, layer):
    tn = 1024
    per = D // tn
    return pl.pallas_call(
        _mod_kernel,
        grid=(N_MOD * per,),
        in_specs=[pl.BlockSpec((8, D), lambda j: (0, 0)),
                  pl.BlockSpec((None, D, tn), lambda j: (layer, 0, j)),
                  pl.BlockSpec((None, 1, tn), lambda j: (layer, 0, j))],
        out_specs=pl.BlockSpec((None, 8, tn), lambda j: (j // per, 0, j % per)),
        out_shape=jax.ShapeDtypeStruct((N_MOD, 8, D), F32),
        compiler_params=_cp(("arbitrary",)),
        name="modulation",
    )(cvec, w_mod, b_mod)


def _mod_spec(k):
    return pl.BlockSpec((None, 8, D), lambda *_: (k, 0, 0))


def _mrow(ref, r):
    return ref[pl.ds(r, 1), :]


def _ffn_kernel(h_ref, sh_ref, sc_ref, gt_ref, gain_ref, wg_ref, wu_ref, wd_ref, *rest,
                nj, row0, final, convert, side):
    rest = list(rest)
    fin_ref = rest.pop(0) if final else None
    if convert:
        ngu_ref, ndn_ref = rest.pop(0), rest.pop(0)
    if side:
        c_ref, wm_ref, bm_ref = rest.pop(0), rest.pop(0), rest.pop(0)
    o_ref = rest.pop(0)
    if convert:
        ogu_ref, odn_ref = rest.pop(0), rest.pop(0)
    if side:
        om_ref = rest.pop(0)
    u_ref = rest.pop(0)
    r = pl.program_id(0) + row0
    j = pl.program_id(2)

    @pl.when(j == 0)
    def _():
        u = _modnorm(h_ref[...], gain_ref[...], _mrow(sh_ref, r), _mrow(sc_ref, r))
        u_ref[...] = u.astype(BF)
        o_ref[...] = jnp.zeros_like(o_ref)

    u = u_ref[...]
    g = _dot(u, wg_ref[...])
    up = _dot(u, wu_ref[...])
    act = (_silu(g) * up).astype(BF)
    o_ref[...] += _dot(act, wd_ref[...])
    if convert:
        ogu_ref[...] = ngu_ref[...].astype(BF)
        odn_ref[...] = ndn_ref[...].astype(BF)
    if side:
        _mod_kernel(c_ref, wm_ref, bm_ref, om_ref)

    @pl.when(j == nj - 1)
    def _():
        res = h_ref[...] + 0.5 * _mrow(gt_ref, r) * o_ref[...]
        if final:
            res = res * lax.rsqrt(jnp.mean(res * res, axis=-1, keepdims=True) + EPS) * fin_ref[...]
        o_ref[...] = res


def _ffn(h, mod, base, mod_row0, gain, w_gu, w_down, final_gain=None, nxt=None, side=None, tm=1024, tf=256):
    nb, rows, _ = h.shape
    nj = FH // tf
    ni = rows // tm
    vec = pl.BlockSpec((1, D), lambda b, i, j: (0, 0))
    in_specs = [pl.BlockSpec((None, tm, D), lambda b, i, j: (b, i, 0)),
                _mod_spec(base), _mod_spec(base + 1), _mod_spec(base + 2), vec,
                pl.BlockSpec((D, tf), lambda b, i, j: (0, j)),
                pl.BlockSpec((D, tf), lambda b, i, j: (0, j + nj)),
                pl.BlockSpec((tf, D), lambda b, i, j: (j, 0))]
    args = [h, mod, mod, mod, gain.reshape(1, D), w_gu, w_gu, w_down]
    out_specs = [pl.BlockSpec((None, tm, D), lambda b, i, j: (b, i, 0))]
    out_shape = [jax.ShapeDtypeStruct(h.shape, F32)]
    if final_gain is not None:
        in_specs.append(vec)
        args.append(final_gain.reshape(1, D))
    if nxt is not None:
        n_gu, n_dn, nl, nw = nxt
        nr = nb * ni
        gu_blk = (D // nr, 2 * FH // nj)
        dn_blk = (FH // (nr * nj), D)
        gu_idx = lambda b, i, j: (b * ni + i, j)
        dn_idx = lambda b, i, j: ((b * ni + i) * nj + j, 0)
        in_specs += [pl.BlockSpec((None, None) + gu_blk, lambda b, i, j: (nl, nw) + gu_idx(b, i, j)),
                     pl.BlockSpec((None, None) + dn_blk, lambda b, i, j: (nl, nw) + dn_idx(b, i, j))]
        args += [n_gu, n_dn]
        out_specs += [pl.BlockSpec(gu_blk, gu_idx), pl.BlockSpec(dn_blk, dn_idx)]
        out_shape += [jax.ShapeDtypeStruct((D, 2 * FH), BF), jax.ShapeDtypeStruct((FH, D), BF)]
    if side is not None:
        cvec, w_mod, b_mod, sl = side
        lanes = 128
        nslab = N_MOD * D // lanes
        assert nb * ni * nj >= nslab
        slab = lambda b, i, j: jnp.minimum((b * ni + i) * nj + j, nslab - 1)
        in_specs += [pl.BlockSpec((8, D), lambda b, i, j: (0, 0)),
                     pl.BlockSpec((None, D, lanes), lambda b, i, j: (sl, 0, slab(b, i, j))),
                     pl.BlockSpec((None, 1, lanes), lambda b, i, j: (sl, 0, slab(b, i, j)))]
        args += [cvec, w_mod, b_mod]
        per = D // lanes
        out_specs.append(pl.BlockSpec((None, 8, lanes), lambda b, i, j: (slab(b, i, j) // per, 0, slab(b, i, j) % per)))
        out_shape.append(jax.ShapeDtypeStruct((N_MOD, 8, D), F32))
    out = pl.pallas_call(
        functools.partial(_ffn_kernel, nj=nj, row0=mod_row0, final=final_gain is not None, convert=nxt is not None,
                          side=side is not None),
        grid=(nb, ni, nj),
        in_specs=in_specs,
        out_specs=out_specs,
        out_shape=out_shape,
        scratch_shapes=[pltpu.VMEM((tm, D), BF)],
        compiler_params=_cp(("arbitrary",) * 3 if side is not None else ("parallel", "parallel", "arbitrary")),
        name="ffn",
    )(*args)
    return out if len(out) > 1 else out[0]


def _prenorm_kernel(hl_ref, hc_ref, sh_ref, sc_ref, gain_ref, o_ref):
    b = pl.program_id(0)
    i = pl.program_id(1)

    @pl.when(i < T // TC)
    def _():
        o_ref[...] = _modnorm(hl_ref[...], gain_ref[...], _mrow(sh_ref, b), _mrow(sc_ref, b)).astype(BF)

    @pl.when(i >= T // TC)
    def _():
        o_ref[...] = _modnorm(hc_ref[...], gain_ref[...], _mrow(sh_ref, NB), _mrow(sc_ref, NB)).astype(BF)


def _prenorm(h, h_ctx, mod, gain):
    nlat = T // TC
    return pl.pallas_call(
        _prenorm_kernel,
        grid=(NB, TT // TC),
        in_specs=[pl.BlockSpec((None, TC, D), lambda b, i: (b, jnp.minimum(i, nlat - 1), 0)),
                  pl.BlockSpec((None, TC, D), lambda b, i: (b, 0, 0)),
                  _mod_spec(3), _mod_spec(4),
                  pl.BlockSpec((1, D), lambda b, i: (0, 0))],
        out_specs=pl.BlockSpec((None, TC, D), lambda b, i: (b, i, 0)),
        out_shape=jax.ShapeDtypeStruct((NB, TT, D), BF),
        compiler_params=_cp(("parallel", "arbitrary")),
        name="prenorm",
    )(h, h_ctx, mod, mod, gain.reshape(1, D))


def _mm_kernel(a_ref, b_ref, o_ref):
    o_ref[...] = _dot(a_ref[...], b_ref[...]).astype(o_ref.dtype)


def _mm_split_kernel(a_ref, b_ref, o1_ref, o2_ref):
    r = _dot(a_ref[...], b_ref[...])
    n1 = o1_ref.shape[1]
    o1_ref[...] = r[:, :n1].astype(o1_ref.dtype)
    o2_ref[...] = r[:, n1:]


def _matmul_split(a, w, tm, n1, name):
    m, k = a.shape
    n = w.shape[1]
    return pl.pallas_call(
        _mm_split_kernel,
        grid=(m // tm,),
        in_specs=[pl.BlockSpec((tm, k), lambda i: (i, 0)),
                  pl.BlockSpec((k, n), lambda i: (0, 0))],
        out_specs=[pl.BlockSpec((tm, n1), lambda i: (i, 0)),
                   pl.BlockSpec((tm, n - n1), lambda i: (i, 0))],
        out_shape=[jax.ShapeDtypeStruct((m, n1), BF), jax.ShapeDtypeStruct((m, n - n1), F32)],
        compiler_params=_cp(("parallel",)),
        name=name,
    )(a, w)


def _matmul(a, w, tm, tn, out_dtype, name):
    m, k = a.shape
    n = w.shape[1]
    return pl.pallas_call(
        _mm_kernel,
        grid=(m // tm, n // tn),
        in_specs=[pl.BlockSpec((tm, k), lambda i, j: (i, 0)),
                  pl.BlockSpec((k, tn), lambda i, j: (0, j))],
        out_specs=pl.BlockSpec((tm, tn), lambda i, j: (i, j)),
        out_shape=jax.ShapeDtypeStruct((m, n), out_dtype),
        compiler_params=_cp(("parallel", "arbitrary")),
        name=name,
    )(a, w)


def _conv_kernel(x_ref, w_ref, o_ref, pad_ref):
    j = pl.program_id(1)
    half = CONV_K // 2
    nblk = x_ref.shape[1] // HD
    for c in range(nblk):
        cols = slice(c * HD, (c + 1) * HD)
        for r0, n in ((0, T), (T, TC)):
            pad_ref[0:8, :] = jnp.zeros((8, HD), F32)
            pad_ref[8:8 + n, :] = x_ref[r0:r0 + n, cols].astype(F32)
            pad_ref[8 + n:16 + n, :] = jnp.zeros((8, HD), F32)
            acc = jnp.zeros((n, HD), F32)
            for k in range(CONV_K):
                acc = acc + w_ref[k:k + 1, cols] * pad_ref[8 + k - half:8 + k - half + n, :]
            y = _silu(acc)
            yn = y * lax.rsqrt(jnp.sum(y * y, axis=-1, keepdims=True) + EPS)
            o_ref[r0:r0 + n, cols] = jnp.where(j * nblk + c < 2 * NH, yn, y).astype(o_ref.dtype)


def _gdn_conv(p, conv_w, nblk=3):
    return pl.pallas_call(
        _conv_kernel,
        grid=(NB, 3 * NH // nblk),
        in_specs=[pl.BlockSpec((None, TT, nblk * HD), lambda b, j: (b, 0, j)),
                  pl.BlockSpec((CONV_K, nblk * HD), lambda b, j: (0, j))],
        out_specs=pl.BlockSpec((None, TT, nblk * HD), lambda b, j: (b, 0, j)),
        out_shape=jax.ShapeDtypeStruct((NB, TT, 3 * W_A), BF),
        scratch_shapes=[pltpu.VMEM((T + 16, HD), F32)],
        compiler_params=_cp(("parallel", "parallel")),
        name="gdn_conv",
    )(p, conv_w)


def _split3(x):
    a = x.astype(BF)
    r = x - a.astype(F32)
    b = r.astype(BF)
    return a, b, (r - b.astype(F32)).astype(BF)


def _gdn_kernel(qf_ref, kf_ref, vf_ref, gf_ref, qb_ref, kb_ref, vb_ref, gb_ref, prm_ref, of_ref, ob_ref, s_ref):
    nbs = qf_ref.shape[0]
    @pl.when(pl.program_id(1) == 0)
    def _():
        s_ref[...] = jnp.zeros_like(s_ref)

    ri = lax.broadcasted_iota(jnp.int32, (CS, CS), 0)
    ci = lax.broadcasted_iota(jnp.int32, (CS, CS), 1)
    eye_f = jnp.where(ri == ci, 1.0, 0.0)
    pair = (ri // 2) == (ci // 2)
    levels = [((ri // (2 * bs)) == (ci // (2 * bs))) & ((ri // bs) != (ci // bs)) for bs in (2, 4, 8, 16, 32)]
    prm = prm_ref[...]
    a_neg = -jnp.exp(prm[0:1, :])
    dt_b = prm[1:2, :]

    ch = []
    dir_refs = ((qf_ref, kf_ref, vf_ref, gf_ref, of_ref), (qb_ref, kb_ref, vb_ref, gb_ref, ob_ref))
    for bb, d in [(bb, d) for bb in range(nbs) for d in range(2)]:
        q_ref, k_ref, v_ref, g_ref, o_ref = [r.at[bb] for r in dir_refs[d]]
        dm = (ri - ci) if d == 0 else (ci - ri)
        incl = dm >= 0
        strict = dm > 0
        tri = jnp.where(incl, 1.0, 0.0).astype(BF)
        tri_t = jnp.where(dm <= 0, 1.0, 0.0).astype(BF)
        gts = g_ref[...]
        x = gts + dt_b
        glog = a_neg * (jnp.maximum(x, 0.0) + jnp.log(1.0 + jnp.exp(-jnp.abs(x))))
        beta = jax.nn.sigmoid(gts)
        parts = _split3(glog)
        gc = sum(_dot(tri, g) for g in parts)
        gc_t = sum(_dot_tn(g, tri_t) for g in parts)
        last = CS - 1 if d == 0 else 0
        g_last = gc[last:last + 1, :]
        eg = jnp.exp(gc)
        ek = jnp.exp(g_last - gc)
        g_tot = jnp.exp(g_last)
        for h in range(NH):
            c = d * NH + h
            ch.append(dict(sl=slice(h * HD, (h + 1) * HD), c=bb * 2 * NH + c, incl=incl, strict=strict, o_ref=o_ref,
                           q_ref=q_ref, k_ref=k_ref, v_ref=v_ref,
                           b_c=beta[:, 2 * NH + c:2 * NH + c + 1], gc_c=gc[:, c:c + 1], gc_r=gc_t[c:c + 1, :],
                           eg=eg[:, c:c + 1], ek=ek[:, c:c + 1], g_tot=g_tot[:, c:c + 1]))

    for x in ch:
        kb = x["k_ref"][:, x["sl"]]
        qh = x["q_ref"][:, x["sl"]].astype(F32) * (HD ** -0.5)
        x["kh"], x["qh"] = kb.astype(F32), qh
        kq = _dot_nt(jnp.concatenate([kb, qh.astype(BF)], axis=0), kb)
        x["kk"], x["qk"] = kq[:CS], kq[CS:]
    for x in ch:
        decay = jnp.exp(jnp.where(x["incl"], x["gc_c"] - x["gc_r"], -1e30))
        lower = jnp.where(x["strict"], x["b_c"] * x["kk"] * decay, 0.0)
        x["qk"] = (x["qk"] * decay).astype(BF)
        x["lower"] = lower.astype(BF)
        x["t"] = eye_f - jnp.where(pair, lower, 0.0)
    for off in levels:
        for x in ch:
            x["tb"] = x["t"].astype(BF)
            x["mt"] = _dot(jnp.where(off, x["lower"], jnp.zeros_like(x["lower"])), x["tb"]).astype(BF)
        for x in ch:
            x["t"] = x["t"] - _dot(x["tb"], x["mt"])
    for x in ch:
        vh = x["v_ref"][:, x["sl"]].astype(F32)
        rhs = jnp.concatenate([vh * x["b_c"], x["kh"] * (x["b_c"] * x["eg"])], axis=1).astype(BF)
        x["uw"] = _dot(x["t"].astype(BF), rhs)
        x["q_dec"] = (x["qh"] * x["eg"]).astype(BF)
        x["k_dec"] = (x["kh"] * x["ek"]).astype(BF)
    for x in ch:
        st = s_ref[x["c"]]
        x["st"] = st
        stb = st.astype(BF)
        ws = _dot(jnp.concatenate([x["uw"][:, HD:].astype(BF), x["q_dec"]], axis=0), stb)
        x["vn"] = (x["uw"][:, :HD] - ws[:CS]).astype(BF)
        x["qs"] = ws[CS:]
    for x in ch:
        x["o_ref"][:, x["sl"]] = x["qs"] + _dot(x["qk"], x["vn"])
        s_ref[x["c"]] = x["st"] * x["g_tot"] + _dot_tn(x["k_dec"], x["vn"])


def _gdn_scan(qkv, gates, prm, nbs=4):
    chunk_f = lambda s: (s + NCH_LAT) % NCH
    chunk_b = lambda s: NCH - 1 - s

    def specs(chunk):
        qkv_spec = lambda c: pl.BlockSpec((nbs, CS, W_A), lambda b, s: (b, chunk(s), c))
        return [qkv_spec(0), qkv_spec(1), qkv_spec(2),
                pl.BlockSpec((nbs, CS, 128), lambda b, s: (b, chunk(s), 0))]

    return pl.pallas_call(
        _gdn_kernel,
        grid=(NB // nbs, NCH),
        in_specs=specs(chunk_f) + specs(chunk_b) + [pl.BlockSpec((8, 128), lambda b, s: (0, 0))],
        out_specs=[pl.BlockSpec((nbs, CS, W_A), lambda b, s: (b, chunk_f(s), 0)),
                   pl.BlockSpec((nbs, CS, W_A), lambda b, s: (b, chunk_b(s), 0))],
        out_shape=[jax.ShapeDtypeStruct((NB, TT, W_A), F32)] * 2,
        scratch_shapes=[pltpu.VMEM((nbs * 2 * NH, HD, HD), F32)],
        compiler_params=_cp(("parallel", "arbitrary")),
        name="gdn_scan",
    )(qkv, qkv, qkv, gates, qkv, qkv, qkv, gates, prm)


def _rope_kernel(x_ref, g_ref, cos_ref, sin_ref, o_ref):
    lane = lax.broadcasted_iota(jnp.int32, (TT, HD), 1)
    first = (lane % 64) < 32
    for hd in range(NH + KVH):
        cols = slice(hd * HD, (hd + 1) * HD)
        x = x_ref[:, cols].astype(F32)
        y = x * lax.rsqrt(jnp.mean(x * x, axis=-1, keepdims=True) + EPS) * g_ref[hd]
        partner = jnp.where(first, pltpu.roll(y, 128 - 32, axis=1), pltpu.roll(y, 32, axis=1))
        o_ref[:, cols] = (y * cos_ref[...] + partner * sin_ref[...]).astype(o_ref.dtype)


def _rope_norm(pb, gains, cos_t, sin_t):
    nh = NH + KVH
    return pl.pallas_call(
        _rope_kernel,
        grid=(NB,),
        in_specs=[pl.BlockSpec((None, TT, nh * HD), lambda b: (b, 0, 0)),
                  pl.BlockSpec((nh, 1, HD), lambda b: (0, 0, 0)),
                  pl.BlockSpec((TT, HD), lambda b: (0, 0)),
                  pl.BlockSpec((TT, HD), lambda b: (0, 0))],
        out_specs=pl.BlockSpec((None, TT, nh * HD), lambda b: (b, 0, 0)),
        out_shape=jax.ShapeDtypeStruct((NB, TT, nh * HD), BF),
        compiler_params=_cp(("parallel",)),
        name="rope_norm",
    )(pb, gains, cos_t, sin_t)


def _attn_kernel(q_ref, k_ref, v_ref, o_ref, s_ref, p_ref):
    k = k_ref[...]
    v_aug = jnp.concatenate([v_ref[...], jnp.ones((TT, HD), BF)], axis=1)
    tq = q_ref.shape[0]
    rb = 16
    for g in range(GQ):
        sl = slice(g * HD, (g + 1) * HD)
        buf = g % 2
        s_ref[buf] = _dot_nt(q_ref[:, sl], k)
        for r in range(tq // rb):
            rows = slice(r * rb, (r + 1) * rb)
            m = s_ref[buf, rows, 0:HD]
            for c in range(1, TT // HD):
                m = jnp.maximum(m, s_ref[buf, rows, c * HD:(c + 1) * HD])
            mx = jnp.broadcast_to(jnp.max(m, axis=-1, keepdims=True), (rb, HD))
            for c in range(TT // HD):
                cols = slice(c * HD, (c + 1) * HD)
                p_ref[buf, rows, cols] = jnp.exp(s_ref[buf, rows, cols] - mx).astype(BF)
        o = _dot(p_ref[buf], v_aug)
        o_ref[:, sl] = (o[:, :HD] / o[:, HD:]).astype(o_ref.dtype)


def _attention(qk, pb, tq=512):
    return pl.pallas_call(
        _attn_kernel,
        grid=(NB, KVH, T // tq),
        in_specs=[pl.BlockSpec((None, tq, GQ * HD), lambda b, kv, i: (b, i, kv)),
                  pl.BlockSpec((None, TT, HD), lambda b, kv, i: (b, 0, NH + kv)),
                  pl.BlockSpec((None, TT, HD), lambda b, kv, i: (b, 0, COL_VB // HD + kv))],
        out_specs=pl.BlockSpec((None, tq, GQ * HD), lambda b, kv, i: (b, i, kv)),
        out_shape=jax.ShapeDtypeStruct((NB, T, W_A), BF),
        scratch_shapes=[pltpu.VMEM((2, tq, TT), F32), pltpu.VMEM((2, tq, TT), BF)],
        compiler_params=_cp(("parallel", "parallel", "arbitrary")),
        name="attention",
    )(qk, qk, pb)


def _mixout_kernel(of_ref, ob_ref, z_ref, a_ref, h_ref, gt_ref, gain_ref, w_ref, o_ref):
    parts = []
    for hd in range(NH):
        sl = slice(hd * HD, (hd + 1) * HD)
        o = of_ref[:, sl] + ob_ref[:, sl]
        n = o * lax.rsqrt(jnp.mean(o * o, axis=-1, keepdims=True) + EPS) * gain_ref[...]
        parts.append((n * _silu(z_ref[:, sl].astype(F32))).astype(BF))
    gd = jnp.concatenate(parts, axis=1)
    y = _dot(gd, w_ref[0:W_A, :]) + _dot(a_ref[...], w_ref[W_A:2 * W_A, :])
    o_ref[...] = h_ref[...] + _mrow(gt_ref, pl.program_id(0)) * y


def _mixout(o_f, o_b, pa, attn, h, mod, gain, w_out, tm=256):
    return pl.pallas_call(
        _mixout_kernel,
        grid=(NB, T // tm),
        in_specs=[pl.BlockSpec((None, tm, W_A), lambda b, i: (b, i, 0)),
                  pl.BlockSpec((None, tm, W_A), lambda b, i: (b, i, 0)),
                  pl.BlockSpec((None, tm, W_A), lambda b, i: (b, i, COL_Z // W_A)),
                  pl.BlockSpec((None, tm, W_A), lambda b, i: (b, i, 0)),
                  pl.BlockSpec((None, tm, D), lambda b, i: (b, i, 0)),
                  _mod_spec(5),
                  pl.BlockSpec((1, HD), lambda b, i: (0, 0)),
                  pl.BlockSpec((D, D), lambda b, i: (0, 0))],
        out_specs=pl.BlockSpec((None, tm, D), lambda b, i: (b, i, 0)),
        out_shape=jax.ShapeDtypeStruct((NB, T, D), F32),
        compiler_params=_cp(("parallel", "parallel")),
        name="mix_out",
    )(o_f, o_b, pa, attn, h, mod, gain.reshape(1, HD), w_out)


def _dft_tables():
    def cs(n):
        idx = np.arange(n, dtype=np.int64)
        ang = 2.0 * np.pi * ((idx[:, None] * idx[None, :]) % n).astype(np.float64) / n
        return np.cos(ang) / math.sqrt(n), np.sin(ang) / math.sqrt(n)

    cc, sc = cs(GC)
    ct, st = cs(T)
    ch = np.concatenate([cc, sc], axis=1).astype(np.float32)
    tm = np.concatenate([ct, -st], axis=1).astype(np.float32)
    return jnp.asarray(ch, dtype=BF), jnp.asarray(tm, dtype=BF)


def _fchan_kernel(h_ref, sh_ref, sc_ref, gain_ref, w_ref, a_ref, b_ref):
    b = pl.program_id(0)
    u = _modnorm(h_ref[...], gain_ref[...], _mrow(sh_ref, b), _mrow(sc_ref, b)).astype(BF)
    w = w_ref[...]
    for g in range(FG):
        sl = slice(g * GC, (g + 1) * GC)
        ab = _dot(u[:, sl], w)
        a_ref[:, sl] = ab[:, :GC].astype(a_ref.dtype)
        b_ref[:, sl] = ab[:, GC:].astype(b_ref.dtype)


def _fourier_channels(h, mod, gain, ch_tab, tm=512):
    return pl.pallas_call(
        _fchan_kernel,
        grid=(NB, T // tm),
        in_specs=[pl.BlockSpec((None, tm, D), lambda b, i: (b, i, 0)),
                  _mod_spec(3), _mod_spec(4),
                  pl.BlockSpec((1, D), lambda b, i: (0, 0)),
                  pl.BlockSpec((GC, 2 * GC), lambda b, i: (0, 0))],
        out_specs=[pl.BlockSpec((None, tm, D), lambda b, i: (b, i, 0))] * 2,
        out_shape=[jax.ShapeDtypeStruct((NB, T, D), BF)] * 2,
        compiler_params=_cp(("parallel", "parallel")),
        name="fourier_channels",
    )(h, mod, mod, gain.reshape(1, D), ch_tab)


def _ftime_kernel(t_ref, a_ref, b_ref, o_ref):
    y = _dot(t_ref[:, :T], a_ref[...]) + _dot(t_ref[:, T:], b_ref[...])
    o_ref[...] = y.astype(o_ref.dtype)


def _fourier_time(tm_tab, a, b, tm=1024, tn=1024):
    return pl.pallas_call(
        _ftime_kernel,
        grid=(NB, D // tn, T // tm),
        in_specs=[pl.BlockSpec((tm, 2 * T), lambda b, j, i: (i, 0)),
                  pl.BlockSpec((None, T, tn), lambda b, j, i: (b, 0, j)),
                  pl.BlockSpec((None, T, tn), lambda b, j, i: (b, 0, j))],
        out_specs=pl.BlockSpec((None, tm, tn), lambda b, j, i: (b, i, j)),
        out_shape=jax.ShapeDtypeStruct((NB, T, D), BF),
        compiler_params=_cp(("parallel", "parallel", "arbitrary")),
        name="fourier_time",
    )(tm_tab, a, b)


def _resmm_kernel(y_ref, w_ref, h_ref, gt_ref, o_ref):
    o_ref[...] = h_ref[...] + _mrow(gt_ref, pl.program_id(0)) * _dot(y_ref[...], w_ref[...])


def _fourier_out(y, w_f, h, mod, tm=512):
    return pl.pallas_call(
        _resmm_kernel,
        grid=(NB, T // tm),
        in_specs=[pl.BlockSpec((None, tm, D), lambda b, i: (b, i, 0)),
                  pl.BlockSpec((D, D), lambda b, i: (0, 0)),
                  pl.BlockSpec((None, tm, D), lambda b, i: (b, i, 0)),
                  _mod_spec(5)],
        out_specs=pl.BlockSpec((None, tm, D), lambda b, i: (b, i, 0)),
        out_shape=jax.ShapeDtypeStruct((NB, T, D), F32),
        compiler_params=_cp(("parallel", "parallel")),
        name="fourier_out",
    )(y, w_f, h, mod)


def _rope_tables():
    half = HD // 4
    inv = ROPE_THETA ** (-np.arange(half, dtype=np.float64) / half)
    t = np.arange(T)
    ang_r = (t // GRID_W)[:, None] * inv[None, :]
    ang_c = (t % GRID_W)[:, None] * inv[None, :]
    cos = np.concatenate([np.cos(ang_r)] * 2 + [np.cos(ang_c)] * 2, axis=1)
    sin = np.concatenate([-np.sin(ang_r), np.sin(ang_r), -np.sin(ang_c), np.sin(ang_c)], axis=1)
    cos = np.concatenate([cos, np.ones((TC, HD))], axis=0).astype(np.float32)
    sin = np.concatenate([sin, np.zeros((TC, HD))], axis=0).astype(np.float32)
    return jnp.asarray(cos), jnp.asarray(sin)


def kernel(x, c, ctx, c_ctx, w_mod, b_mod, norm_w, ffn_w_gu, ffn_w_down, w_in, conv_w, a_log, dt_bias,
           gdn_norm, q_norm, k_norm, w_out, w_fourier, final_norm):
    cvec = jnp.concatenate([c, c_ctx[None, :], jnp.zeros((8 - NB - 1, D), F32)], axis=0)
    b_mod3 = b_mod.reshape(DEPTH, 1, N_MOD * D)
    mod0 = _modulation(cvec, w_mod, b_mod3, 0)
    w_gu = ffn_w_gu[0, 0].astype(BF)
    w_dn = ffn_w_down[0, 0].astype(BF)

    h_ctx = _ffn(ctx.reshape(1, NB * TC, D), mod0, 0, NB, norm_w[0, 0], w_gu, w_dn).reshape(NB, TC, D)
    h, w_gu, w_dn, mod1 = _ffn(x, mod0, 0, 0, norm_w[0, 0], w_gu, w_dn, nxt=(ffn_w_gu, ffn_w_down, 0, 1),
                               side=(cvec, w_mod, b_mod3, 1))

    wi = w_in[0]
    w_a = wi[:, :NPA].astype(BF)
    w_b = jnp.concatenate([wi[:, NPA + 4 * NH:].astype(BF), wi[:, NPA:NPA + 4 * NH].astype(BF),
                           jnp.zeros((D, 128 - 4 * NH), BF)], axis=1)
    u = _prenorm(h, h_ctx, mod0, norm_w[0, 1]).reshape(NB * TT, D)
    pa = _matmul(u, w_a, 1024, 2048, BF, "in_proj_a").reshape(NB, TT, NPA)
    pb, gates = _matmul_split(u, w_b, 1024, NPB, "in_proj_b")
    pb = pb.reshape(NB, TT, NPB)

    qkv = _gdn_conv(pa, conv_w[0])
    prm = jnp.zeros((8, 128), F32)
    prm = prm.at[0, :2 * NH].set(a_log[0].reshape(-1)).at[1, :2 * NH].set(dt_bias[0].reshape(-1))
    o_f, o_b = _gdn_scan(qkv, gates.reshape(NB, TT, 128), prm)

    gains = jnp.concatenate([jnp.broadcast_to(q_norm[0] * (HD ** -0.5), (NH, HD)),
                             jnp.broadcast_to(k_norm[0], (KVH, HD))], axis=0).reshape(NH + KVH, 1, HD)
    cos_t, sin_t = _rope_tables()
    qk = _rope_norm(pb, gains, cos_t, sin_t)
    attn = _attention(qk, pb)

    h = _mixout(o_f, o_b, pa, attn, h, mod0, gdn_norm[0], w_out[0].astype(BF))
    h, w_gu, w_dn = _ffn(h, mod0, 6, 0, norm_w[0, 2], w_gu, w_dn, nxt=(ffn_w_gu, ffn_w_down, 1, 0))

    h, w_gu, w_dn = _ffn(h, mod1, 0, 0, norm_w[1, 0], w_gu, w_dn, nxt=(ffn_w_gu, ffn_w_down, 1, 1))
    ch_tab, tm_tab = _dft_tables()
    fa, fb = _fourier_channels(h, mod1, norm_w[1, 1], ch_tab)
    y = _fourier_time(tm_tab, fa, fb)
    h = _fourier_out(y, w_fourier[0].astype(BF), h, mod1)
    return _ffn(h, mod1, 6, 0, norm_w[1, 2], w_gu, w_dn, final_gain=final_norm)
```

```python
import functools
import math

import numpy as np
import jax
import jax.numpy as jnp
from jax import lax
from jax.experimental import pallas as pl
from jax.experimental.pallas import tpu as pltpu

D = 2048
NB = 4
T = 2048
GRID_W = 64
TC = 256
TT = T + TC
DEPTH = 2
EPS = 1e-6
N_MOD = 9
FH = 5632
NH = 8
HD = 128
KVH = 2
GQ = NH // KVH
CONV_K = 5
CS = 64
NCH = TT // CS
NCH_LAT = T // CS
ROPE_THETA = 10000.0
FG = 4
GC = D // FG
W_A = NH * HD
NPA = 4 * W_A
COL_Z = 3 * W_A
NPB = W_A + 2 * KVH * HD
COL_KB = W_A
COL_VB = COL_KB + KVH * HD

VMEM_LIMIT = 58 * 1024 * 1024
BF = jnp.bfloat16
F32 = jnp.float32


def _cp(sem):
    return pltpu.CompilerParams(dimension_semantics=sem, vmem_limit_bytes=VMEM_LIMIT)


def _dot(a, b):
    return jnp.dot(a, b, preferred_element_type=F32)


def _dot_nt(a, b):
    return lax.dot_general(a, b, (((1,), (1,)), ((), ())), preferred_element_type=F32)


def _dot_tn(a, b):
    return lax.dot_general(a, b, (((0,), (0,)), ((), ())), preferred_element_type=F32)


def _silu(x):
    return x * jax.nn.sigmoid(x)


def _mod_kernel(c_ref, w_ref, b_ref, o_ref):
    a = _silu(c_ref[...]).astype(BF)
    o_ref[...] = _dot(a, w_ref[...].astype(BF)) + b_ref[...]


def _modulation(cvec, w_mod, b_mod, layer):
    tn = 1024
    per = D // tn
    return pl.pallas_call(
        _mod_kernel,
        grid=(N_MOD * per,),
        in_specs=[pl.BlockSpec((8, D), lambda j: (0, 0)),
                  pl.BlockSpec((None, D, tn), lambda j: (layer, 0, j)),
                  pl.BlockSpec((None, 1, tn), lambda j: (layer, 0, j))],
        out_specs=pl.BlockSpec((None, 8, tn), lambda j: (j // per, 0, j % per)),
        out_shape=jax.ShapeDtypeStruct((N_MOD, 8, D), F32),
        compiler_params=_cp(("arbitrary",)),
        name="modulation",
    )(cvec, w_mod, b_mod)


def _mod_spec(k):
    return pl.BlockSpec((None, 8, D), lambda *_: (k, 0, 0))


def _mrow(ref, r):
    return ref[pl.ds(r, 1), :]


def _modnorm_tiles(h_ref, u_ref, rs_ref, gain, shift, scale, rb=32):
    n, d = h_ref.shape
    g = gain * (1.0 + scale)
    lanes = rs_ref.shape[1]
    for c in range(n // rb):
        rows = slice(c * rb, (c + 1) * rb)
        acc = None
        for k in range(d // lanes):
            x = h_ref[rows, k * lanes:(k + 1) * lanes]
            acc = x * x if acc is None else acc + x * x
        ms = jnp.sum(acc, axis=-1, keepdims=True) * (1.0 / d)
        rs_ref[rows, :] = jnp.broadcast_to(lax.rsqrt(ms + EPS), (rb, lanes))
    for c in range(n // rb):
        rows = slice(c * rb, (c + 1) * rb)
        rs = rs_ref[rows, :]
        for k in range(d // lanes):
            cols = slice(k * lanes, (k + 1) * lanes)
            u_ref[rows, cols] = (h_ref[rows, cols] * rs * g[:, cols] + shift[:, cols]).astype(BF)


def _ffn_kernel(h_ref, sh_ref, sc_ref, gt_ref, gain_ref, wg_ref, wu_ref, wd_ref, *rest,
                nj, row0, final, convert, side):
    rest = list(rest)
    fin_ref = rest.pop(0) if final else None
    if convert:
        ngu_ref, ndn_ref = rest.pop(0), rest.pop(0)
    if side:
        c_ref, wm_ref, bm_ref = rest.pop(0), rest.pop(0), rest.pop(0)
    o_ref = rest.pop(0)
    if convert:
        ogu_ref, odn_ref = rest.pop(0), rest.pop(0)
    if side:
        om_ref = rest.pop(0)
    u_ref, rs_ref = rest
    r = pl.program_id(0) + row0
    j = pl.program_id(2)

    @pl.when(j == 0)
    def _():
        _modnorm_tiles(h_ref, u_ref, rs_ref, gain_ref[...], _mrow(sh_ref, r), _mrow(sc_ref, r))
        o_ref[...] = jnp.zeros_like(o_ref)

    u = u_ref[...]
    g = _dot(u, wg_ref[...])
    up = _dot(u, wu_ref[...])
    act = (_silu(g) * up).astype(BF)
    o_ref[...] += _dot(act, wd_ref[...])
    if convert:
        ogu_ref[...] = ngu_ref[...].astype(BF)
        odn_ref[...] = ndn_ref[...].astype(BF)
    if side:
        _mod_kernel(c_ref, wm_ref, bm_ref, om_ref)

    @pl.when(j == nj - 1)
    def _():
        res = h_ref[...] + 0.5 * _mrow(gt_ref, r) * o_ref[...]
        if final:
            res = res * lax.rsqrt(jnp.mean(res * res, axis=-1, keepdims=True) + EPS) * fin_ref[...]
        o_ref[...] = res


def _ffn(h, mod, base, mod_row0, gain, w_gu, w_down, final_gain=None, nxt=None, side=None, tm=1024, tf=256):
    nb, rows, _ = h.shape
    nj = FH // tf
    ni = rows // tm
    vec = pl.BlockSpec((1, D), lambda b, i, j: (0, 0))
    in_specs = [pl.BlockSpec((None, tm, D), lambda b, i, j: (b, i, 0)),
                _mod_spec(base), _mod_spec(base + 1), _mod_spec(base + 2), vec,
                pl.BlockSpec((D, tf), lambda b, i, j: (0, j)),
                pl.BlockSpec((D, tf), lambda b, i, j: (0, j + nj)),
                pl.BlockSpec((tf, D), lambda b, i, j: (j, 0))]
    args = [h, mod, mod, mod, gain.reshape(1, D), w_gu, w_gu, w_down]
    out_specs = [pl.BlockSpec((None, tm, D), lambda b, i, j: (b, i, 0))]
    out_shape = [jax.ShapeDtypeStruct(h.shape, F32)]
    if final_gain is not None:
        in_specs.append(vec)
        args.append(final_gain.reshape(1, D))
    if nxt is not None:
        n_gu, n_dn, nl, nw = nxt
        nr = nb * ni
        gu_blk = (D // nr, 2 * FH // nj)
        dn_blk = (FH // (nr * nj), D)
        gu_idx = lambda b, i, j: (b * ni + i, j)
        dn_idx = lambda b, i, j: ((b * ni + i) * nj + j, 0)
        in_specs += [pl.BlockSpec((None, None) + gu_blk, lambda b, i, j: (nl, nw) + gu_idx(b, i, j)),
                     pl.BlockSpec((None, None) + dn_blk, lambda b, i, j: (nl, nw) + dn_idx(b, i, j))]
        args += [n_gu, n_dn]
        out_specs += [pl.BlockSpec(gu_blk, gu_idx), pl.BlockSpec(dn_blk, dn_idx)]
        out_shape += [jax.ShapeDtypeStruct((D, 2 * FH), BF), jax.ShapeDtypeStruct((FH, D), BF)]
    if side is not None:
        cvec, w_mod, b_mod, sl = side
        lanes = 128
        nslab = N_MOD * D // lanes
        assert nb * ni * nj >= nslab
        slab = lambda b, i, j: jnp.minimum((b * ni + i) * nj + j, nslab - 1)
        in_specs += [pl.BlockSpec((8, D), lambda b, i, j: (0, 0)),
                     pl.BlockSpec((None, D, lanes), lambda b, i, j: (sl, 0, slab(b, i, j))),
                     pl.BlockSpec((None, 1, lanes), lambda b, i, j: (sl, 0, slab(b, i, j)))]
        args += [cvec, w_mod, b_mod]
        per = D // lanes
        out_specs.append(pl.BlockSpec((None, 8, lanes), lambda b, i, j: (slab(b, i, j) // per, 0, slab(b, i, j) % per)))
        out_shape.append(jax.ShapeDtypeStruct((N_MOD, 8, D), F32))
    out = pl.pallas_call(
        functools.partial(_ffn_kernel, nj=nj, row0=mod_row0, final=final_gain is not None, convert=nxt is not None,
                          side=side is not None),
        grid=(nb, ni, nj),
        in_specs=in_specs,
        out_specs=out_specs,
        out_shape=out_shape,
        scratch_shapes=[pltpu.VMEM((tm, D), BF), pltpu.VMEM((tm, 128), F32)],
        compiler_params=_cp(("arbitrary",) * 3 if side is not None else ("parallel", "parallel", "arbitrary")),
        name="ffn",
    )(*args)
    return out if len(out) > 1 else out[0]


def _prenorm_kernel(hl_ref, hc_ref, sh_ref, sc_ref, gain_ref, o_ref, rs_ref):
    b = pl.program_id(0)
    i = pl.program_id(1)

    @pl.when(i < T // TC)
    def _():
        _modnorm_tiles(hl_ref, o_ref, rs_ref, gain_ref[...], _mrow(sh_ref, b), _mrow(sc_ref, b))

    @pl.when(i >= T // TC)
    def _():
        _modnorm_tiles(hc_ref, o_ref, rs_ref, gain_ref[...], _mrow(sh_ref, NB), _mrow(sc_ref, NB))


def _prenorm(h, h_ctx, mod, gain):
    nlat = T // TC
    return pl.pallas_call(
        _prenorm_kernel,
        grid=(NB, TT // TC),
        in_specs=[pl.BlockSpec((None, TC, D), lambda b, i: (b, jnp.minimum(i, nlat - 1), 0)),
                  pl.BlockSpec((None, TC, D), lambda b, i: (b, 0, 0)),
                  _mod_spec(3), _mod_spec(4),
                  pl.BlockSpec((1, D), lambda b, i: (0, 0))],
        out_specs=pl.BlockSpec((None, TC, D), lambda b, i: (b, i, 0)),
        out_shape=jax.ShapeDtypeStruct((NB, TT, D), BF),
        scratch_shapes=[pltpu.VMEM((TC, 128), F32)],
        compiler_params=_cp(("parallel", "arbitrary")),
        name="prenorm",
    )(h, h_ctx, mod, mod, gain.reshape(1, D))


def _mm_kernel(a_ref, b_ref, o_ref):
    o_ref[...] = _dot(a_ref[...], b_ref[...]).astype(o_ref.dtype)


def _mm_split_kernel(a_ref, b_ref, o1_ref, o2_ref):
    r = _dot(a_ref[...], b_ref[...])
    n1 = o1_ref.shape[1]
    o1_ref[...] = r[:, :n1].astype(o1_ref.dtype)
    o2_ref[...] = r[:, n1:]


def _matmul_split(a, w, tm, n1, name):
    m, k = a.shape
    n = w.shape[1]
    return pl.pallas_call(
        _mm_split_kernel,
        grid=(m // tm,),
        in_specs=[pl.BlockSpec((tm, k), lambda i: (i, 0)),
                  pl.BlockSpec((k, n), lambda i: (0, 0))],
        out_specs=[pl.BlockSpec((tm, n1), lambda i: (i, 0)),
                   pl.BlockSpec((tm, n - n1), lambda i: (i, 0))],
        out_shape=[jax.ShapeDtypeStruct((m, n1), BF), jax.ShapeDtypeStruct((m, n - n1), F32)],
        compiler_params=_cp(("parallel",)),
        name=name,
    )(a, w)


def _matmul(a, w, tm, tn, out_dtype, name):
    m, k = a.shape
    n = w.shape[1]
    return pl.pallas_call(
        _mm_kernel,
        grid=(m // tm, n // tn),
        in_specs=[pl.BlockSpec((tm, k), lambda i, j: (i, 0)),
                  pl.BlockSpec((k, tn), lambda i, j: (0, j))],
        out_specs=pl.BlockSpec((tm, tn), lambda i, j: (i, j)),
        out_shape=jax.ShapeDtypeStruct((m, n), out_dtype),
        compiler_params=_cp(("parallel", "arbitrary")),
        name=name,
    )(a, w)


def _conv_kernel(x_ref, w_ref, o_ref, pad_ref):
    j = pl.program_id(1)
    half = CONV_K // 2
    nblk = x_ref.shape[1] // HD
    for c in range(nblk):
        cols = slice(c * HD, (c + 1) * HD)
        for r0, n in ((0, T), (T, TC)):
            pad_ref[0:8, :] = jnp.zeros((8, HD), F32)
            pad_ref[8:8 + n, :] = x_ref[r0:r0 + n, cols].astype(F32)
            pad_ref[8 + n:16 + n, :] = jnp.zeros((8, HD), F32)
            acc = jnp.zeros((n, HD), F32)
            for k in range(CONV_K):
                acc = acc + w_ref[k:k + 1, cols] * pad_ref[8 + k - half:8 + k - half + n, :]
            y = _silu(acc)
            yn = y * lax.rsqrt(jnp.sum(y * y, axis=-1, keepdims=True) + EPS)
            o_ref[r0:r0 + n, cols] = jnp.where(j * nblk + c < 2 * NH, yn, y).astype(o_ref.dtype)


def _gdn_conv(p, conv_w, nblk=3):
    return pl.pallas_call(
        _conv_kernel,
        grid=(NB, 3 * NH // nblk),
        in_specs=[pl.BlockSpec((None, TT, nblk * HD), lambda b, j: (b, 0, j)),
                  pl.BlockSpec((CONV_K, nblk * HD), lambda b, j: (0, j))],
        out_specs=pl.BlockSpec((None, TT, nblk * HD), lambda b, j: (b, 0, j)),
        out_shape=jax.ShapeDtypeStruct((NB, TT, 3 * W_A), BF),
        scratch_shapes=[pltpu.VMEM((T + 16, HD), F32)],
        compiler_params=_cp(("parallel", "parallel")),
        name="gdn_conv",
    )(p, conv_w)


def _split3(x):
    a = x.astype(BF)
    r = x - a.astype(F32)
    b = r.astype(BF)
    return a, b, (r - b.astype(F32)).astype(BF)


def _gdn_kernel(qf_ref, kf_ref, vf_ref, gf_ref, qb_ref, kb_ref, vb_ref, gb_ref, prm_ref, of_ref, ob_ref, s_ref):
    nbs = qf_ref.shape[0]
    @pl.when(pl.program_id(1) == 0)
    def _():
        s_ref[...] = jnp.zeros_like(s_ref)

    ri = lax.broadcasted_iota(jnp.int32, (CS, CS), 0)
    ci = lax.broadcasted_iota(jnp.int32, (CS, CS), 1)
    eye_f = jnp.where(ri == ci, 1.0, 0.0)
    pair = (ri // 2) == (ci // 2)
    levels = [((ri // (2 * bs)) == (ci // (2 * bs))) & ((ri // bs) != (ci // bs)) for bs in (2, 4, 8, 16, 32)]
    prm = prm_ref[...]
    a_neg = -jnp.exp(prm[0:1, :])
    dt_b = prm[1:2, :]

    ch = []
    dir_refs = ((qf_ref, kf_ref, vf_ref, gf_ref, of_ref), (qb_ref, kb_ref, vb_ref, gb_ref, ob_ref))
    for bb, d in [(bb, d) for bb in range(nbs) for d in range(2)]:
        q_ref, k_ref, v_ref, g_ref, o_ref = [r.at[bb] for r in dir_refs[d]]
        dm = (ri - ci) if d == 0 else (ci - ri)
        incl = dm >= 0
        strict = dm > 0
        tri = jnp.where(incl, 1.0, 0.0).astype(BF)
        tri_t = jnp.where(dm <= 0, 1.0, 0.0).astype(BF)
        gts = g_ref[...]
        x = gts + dt_b
        glog = a_neg * (jnp.maximum(x, 0.0) + jnp.log(1.0 + jnp.exp(-jnp.abs(x))))
        beta = jax.nn.sigmoid(gts)
        parts = _split3(glog)
        gc = sum(_dot(tri, g) for g in parts)
        gc_t = sum(_dot_tn(g, tri_t) for g in parts)
        last = CS - 1 if d == 0 else 0
        g_last = gc[last:last + 1, :]
        eg = jnp.exp(gc)
        ek = jnp.exp(g_last - gc)
        g_tot = jnp.exp(g_last)
        for h in range(NH):
            c = d * NH + h
            ch.append(dict(sl=slice(h * HD, (h + 1) * HD), c=bb * 2 * NH + c, incl=incl, strict=strict, o_ref=o_ref,
                           q_ref=q_ref, k_ref=k_ref, v_ref=v_ref,
                           b_c=beta[:, 2 * NH + c:2 * NH + c + 1], gc_c=gc[:, c:c + 1], gc_r=gc_t[c:c + 1, :],
                           eg=eg[:, c:c + 1], ek=ek[:, c:c + 1], g_tot=g_tot[:, c:c + 1]))

    for x in ch:
        kb = x["k_ref"][:, x["sl"]]
        qh = x["q_ref"][:, x["sl"]].astype(F32) * (HD ** -0.5)
        x["kh"], x["qh"] = kb.astype(F32), qh
        kq = _dot_nt(jnp.concatenate([kb, qh.astype(BF)], axis=0), kb)
        x["kk"], x["qk"] = kq[:CS], kq[CS:]
    for x in ch:
        decay = jnp.exp(jnp.where(x["incl"], x["gc_c"] - x["gc_r"], -1e30))
        lower = jnp.where(x["strict"], x["b_c"] * x["kk"] * decay, 0.0)
        x["qk"] = (x["qk"] * decay).astype(BF)
        x["lower"] = lower.astype(BF)
        x["t"] = eye_f - jnp.where(pair, lower, 0.0)
    for off in levels:
        for x in ch:
            x["tb"] = x["t"].astype(BF)
            x["mt"] = _dot(jnp.where(off, x["lower"], jnp.zeros_like(x["lower"])), x["tb"]).astype(BF)
        for x in ch:
            x["t"] = x["t"] - _dot(x["tb"], x["mt"])
    for x in ch:
        vh = x["v_ref"][:, x["sl"]].astype(F32)
        rhs = jnp.concatenate([vh * x["b_c"], x["kh"] * (x["b_c"] * x["eg"])], axis=1).astype(BF)
        x["uw"] = _dot(x["t"].astype(BF), rhs)
        x["q_dec"] = (x["qh"] * x["eg"]).astype(BF)
        x["k_dec"] = (x["kh"] * x["ek"]).astype(BF)
    for x in ch:
        st = s_ref[x["c"]]
        x["st"] = st
        stb = st.astype(BF)
        ws = _dot(jnp.concatenate([x["uw"][:, HD:].astype(BF), x["q_dec"]], axis=0), stb)
        x["vn"] = (x["uw"][:, :HD] - ws[:CS]).astype(BF)
        x["qs"] = ws[CS:]
    for x in ch:
        x["o_ref"][:, x["sl"]] = x["qs"] + _dot(x["qk"], x["vn"])
        s_ref[x["c"]] = x["st"] * x["g_tot"] + _dot_tn(x["k_dec"], x["vn"])


def _gdn_scan(qkv, gates, prm, nbs=4):
    chunk_f = lambda s: (s + NCH_LAT) % NCH
    chunk_b = lambda s: NCH - 1 - s

    def specs(chunk):
        qkv_spec = lambda c: pl.BlockSpec((nbs, CS, W_A), lambda b, s: (b, chunk(s), c))
        return [qkv_spec(0), qkv_spec(1), qkv_spec(2),
                pl.BlockSpec((nbs, CS, 128), lambda b, s: (b, chunk(s), 0))]

    return pl.pallas_call(
        _gdn_kernel,
        grid=(NB // nbs, NCH),
        in_specs=specs(chunk_f) + specs(chunk_b) + [pl.BlockSpec((8, 128), lambda b, s: (0, 0))],
        out_specs=[pl.BlockSpec((nbs, CS, W_A), lambda b, s: (b, chunk_f(s), 0)),
                   pl.BlockSpec((nbs, CS, W_A), lambda b, s: (b, chunk_b(s), 0))],
        out_shape=[jax.ShapeDtypeStruct((NB, TT, W_A), F32)] * 2,
        scratch_shapes=[pltpu.VMEM((nbs * 2 * NH, HD, HD), F32)],
        compiler_params=_cp(("parallel", "arbitrary")),
        name="gdn_scan",
    )(qkv, qkv, qkv, gates, qkv, qkv, qkv, gates, prm)


def _rope_kernel(x_ref, g_ref, cos_ref, sin_ref, o_ref):
    lane = lax.broadcasted_iota(jnp.int32, (TT, HD), 1)
    first = (lane % 64) < 32
    for hd in range(NH + KVH):
        cols = slice(hd * HD, (hd + 1) * HD)
        x = x_ref[:, cols].astype(F32)
        y = x * lax.rsqrt(jnp.mean(x * x, axis=-1, keepdims=True) + EPS) * g_ref[hd]
        partner = jnp.where(first, pltpu.roll(y, 128 - 32, axis=1), pltpu.roll(y, 32, axis=1))
        o_ref[:, cols] = (y * cos_ref[...] + partner * sin_ref[...]).astype(o_ref.dtype)


def _rope_norm(pb, gains, cos_t, sin_t):
    nh = NH + KVH
    return pl.pallas_call(
        _rope_kernel,
        grid=(NB,),
        in_specs=[pl.BlockSpec((None, TT, nh * HD), lambda b: (b, 0, 0)),
                  pl.BlockSpec((nh, 1, HD), lambda b: (0, 0, 0)),
                  pl.BlockSpec((TT, HD), lambda b: (0, 0)),
                  pl.BlockSpec((TT, HD), lambda b: (0, 0))],
        out_specs=pl.BlockSpec((None, TT, nh * HD), lambda b: (b, 0, 0)),
        out_shape=jax.ShapeDtypeStruct((NB, TT, nh * HD), BF),
        compiler_params=_cp(("parallel",)),
        name="rope_norm",
    )(pb, gains, cos_t, sin_t)


def _attn_kernel(q_ref, k_ref, v_ref, o_ref, s_ref, p_ref):
    k = k_ref[...]
    v_aug = jnp.concatenate([v_ref[...], jnp.ones((TT, HD), BF)], axis=1)
    tq = q_ref.shape[0]
    rb = 16
    for g in range(GQ):
        sl = slice(g * HD, (g + 1) * HD)
        buf = g % 2
        s_ref[buf] = _dot_nt(q_ref[:, sl], k)
        for r in range(tq // rb):
            rows = slice(r * rb, (r + 1) * rb)
            m = s_ref[buf, rows, 0:HD]
            for c in range(1, TT // HD):
                m = jnp.maximum(m, s_ref[buf, rows, c * HD:(c + 1) * HD])
            mx = jnp.broadcast_to(jnp.max(m, axis=-1, keepdims=True), (rb, HD))
            for c in range(TT // HD):
                cols = slice(c * HD, (c + 1) * HD)
                p_ref[buf, rows, cols] = jnp.exp(s_ref[buf, rows, cols] - mx).astype(BF)
        o = _dot(p_ref[buf], v_aug)
        o_ref[:, sl] = (o[:, :HD] / o[:, HD:]).astype(o_ref.dtype)


def _attention(qk, pb, tq=512):
    return pl.pallas_call(
        _attn_kernel,
        grid=(NB, KVH, T // tq),
        in_specs=[pl.BlockSpec((None, tq, GQ * HD), lambda b, kv, i: (b, i, kv)),
                  pl.BlockSpec((None, TT, HD), lambda b, kv, i: (b, 0, NH + kv)),
                  pl.BlockSpec((None, TT, HD), lambda b, kv, i: (b, 0, COL_VB // HD + kv))],
        out_specs=pl.BlockSpec((None, tq, GQ * HD), lambda b, kv, i: (b, i, kv)),
        out_shape=jax.ShapeDtypeStruct((NB, T, W_A), BF),
        scratch_shapes=[pltpu.VMEM((2, tq, TT), F32), pltpu.VMEM((2, tq, TT), BF)],
        compiler_params=_cp(("parallel", "parallel", "arbitrary")),
        name="attention",
    )(qk, qk, pb)


def _mixout_kernel(of_ref, ob_ref, z_ref, a_ref, h_ref, gt_ref, gain_ref, w_ref, o_ref):
    parts = []
    for hd in range(NH):
        sl = slice(hd * HD, (hd + 1) * HD)
        o = of_ref[:, sl] + ob_ref[:, sl]
        n = o * lax.rsqrt(jnp.mean(o * o, axis=-1, keepdims=True) + EPS) * gain_ref[...]
        parts.append((n * _silu(z_ref[:, sl].astype(F32))).astype(BF))
    gd = jnp.concatenate(parts, axis=1)
    y = _dot(gd, w_ref[0:W_A, :]) + _dot(a_ref[...], w_ref[W_A:2 * W_A, :])
    o_ref[...] = h_ref[...] + _mrow(gt_ref, pl.program_id(0)) * y


def _mixout(o_f, o_b, pa, attn, h, mod, gain, w_out, tm=256):
    return pl.pallas_call(
        _mixout_kernel,
        grid=(NB, T // tm),
        in_specs=[pl.BlockSpec((None, tm, W_A), lambda b, i: (b, i, 0)),
                  pl.BlockSpec((None, tm, W_A), lambda b, i: (b, i, 0)),
                  pl.BlockSpec((None, tm, W_A), lambda b, i: (b, i, COL_Z // W_A)),
                  pl.BlockSpec((None, tm, W_A), lambda b, i: (b, i, 0)),
                  pl.BlockSpec((None, tm, D), lambda b, i: (b, i, 0)),
                  _mod_spec(5),
                  pl.BlockSpec((1, HD), lambda b, i: (0, 0)),
                  pl.BlockSpec((D, D), lambda b, i: (0, 0))],
        out_specs=pl.BlockSpec((None, tm, D), lambda b, i: (b, i, 0)),
        out_shape=jax.ShapeDtypeStruct((NB, T, D), F32),
        compiler_params=_cp(("parallel", "parallel")),
        name="mix_out",
    )(o_f, o_b, pa, attn, h, mod, gain.reshape(1, HD), w_out)


def _dft_tables():
    def cs(n):
        idx = np.arange(n, dtype=np.int64)
        ang = 2.0 * np.pi * ((idx[:, None] * idx[None, :]) % n).astype(np.float64) / n
        return np.cos(ang) / math.sqrt(n), np.sin(ang) / math.sqrt(n)

    cc, sc = cs(GC)
    ct, st = cs(T)
    ch = np.concatenate([cc, sc], axis=1).astype(np.float32)
    tm = np.concatenate([ct, -st], axis=1).astype(np.float32)
    return jnp.asarray(ch, dtype=BF), jnp.asarray(tm, dtype=BF)


def _fchan_kernel(h_ref, sh_ref, sc_ref, gain_ref, w_ref, a_ref, b_ref, u_ref, rs_ref):
    b = pl.program_id(0)
    _modnorm_tiles(h_ref, u_ref, rs_ref, gain_ref[...], _mrow(sh_ref, b), _mrow(sc_ref, b))
    w = w_ref[...]
    for g in range(FG):
        sl = slice(g * GC, (g + 1) * GC)
        ab = _dot(u_ref[:, sl], w)
        a_ref[:, sl] = ab[:, :GC].astype(a_ref.dtype)
        b_ref[:, sl] = ab[:, GC:].astype(b_ref.dtype)


def _fourier_channels(h, mod, gain, ch_tab, tm=512):
    return pl.pallas_call(
        _fchan_kernel,
        grid=(NB, T // tm),
        in_specs=[pl.BlockSpec((None, tm, D), lambda b, i: (b, i, 0)),
                  _mod_spec(3), _mod_spec(4),
                  pl.BlockSpec((1, D), lambda b, i: (0, 0)),
                  pl.BlockSpec((GC, 2 * GC), lambda b, i: (0, 0))],
        out_specs=[pl.BlockSpec((None, tm, D), lambda b, i: (b, i, 0))] * 2,
        out_shape=[jax.ShapeDtypeStruct((NB, T, D), BF)] * 2,
        scratch_shapes=[pltpu.VMEM((tm, D), BF), pltpu.VMEM((tm, 128), F32)],
        compiler_params=_cp(("parallel", "parallel")),
        name="fourier_channels",
    )(h, mod, mod, gain.reshape(1, D), ch_tab)


def _ftime_kernel(t_ref, a_ref, b_ref, o_ref):
    y = _dot(t_ref[:, :T], a_ref[...]) + _dot(t_ref[:, T:], b_ref[...])
    o_ref[...] = y.astype(o_ref.dtype)


def _fourier_time(tm_tab, a, b, tm=1024, tn=1024):
    return pl.pallas_call(
        _ftime_kernel,
        grid=(NB, D // tn, T // tm),
        in_specs=[pl.BlockSpec((tm, 2 * T), lambda b, j, i: (i, 0)),
                  pl.BlockSpec((None, T, tn), lambda b, j, i: (b, 0, j)),
                  pl.BlockSpec((None, T, tn), lambda b, j, i: (b, 0, j))],
        out_specs=pl.BlockSpec((None, tm, tn), lambda b, j, i: (b, i, j)),
        out_shape=jax.ShapeDtypeStruct((NB, T, D), BF),
        compiler_params=_cp(("parallel", "parallel", "arbitrary")),
        name="fourier_time",
    )(tm_tab, a, b)


def _resmm_kernel(y_ref, w_ref, h_ref, gt_ref, o_ref):
    o_ref[...] = h_ref[...] + _mrow(gt_ref, pl.program_id(0)) * _dot(y_ref[...], w_ref[...])


def _fourier_out(y, w_f, h, mod, tm=512):
    return pl.pallas_call(
        _resmm_kernel,
        grid=(NB, T // tm),
        in_specs=[pl.BlockSpec((None, tm, D), lambda b, i: (b, i, 0)),
                  pl.BlockSpec((D, D), lambda b, i: (0, 0)),
                  pl.BlockSpec((None, tm, D), lambda b, i: (b, i, 0)),
                  _mod_spec(5)],
        out_specs=pl.BlockSpec((None, tm, D), lambda b, i: (b, i, 0)),
        out_shape=jax.ShapeDtypeStruct((NB, T, D), F32),
        compiler_params=_cp(("parallel", "parallel")),
        name="fourier_out",
    )(y, w_f, h, mod)


def _rope_tables():
    half = HD // 4
    inv = ROPE_THETA ** (-np.arange(half, dtype=np.float64) / half)
    t = np.arange(T)
    ang_r = (t // GRID_W)[:, None] * inv[None, :]
    ang_c = (t % GRID_W)[:, None] * inv[None, :]
    cos = np.concatenate([np.cos(ang_r)] * 2 + [np.cos(ang_c)] * 2, axis=1)
    sin = np.concatenate([-np.sin(ang_r), np.sin(ang_r), -np.sin(ang_c), np.sin(ang_c)], axis=1)
    cos = np.concatenate([cos, np.ones((TC, HD))], axis=0).astype(np.float32)
    sin = np.concatenate([sin, np.zeros((TC, HD))], axis=0).astype(np.float32)
    return jnp.asarray(cos), jnp.asarray(sin)


def kernel(x, c, ctx, c_ctx, w_mod, b_mod, norm_w, ffn_w_gu, ffn_w_down, w_in, conv_w, a_log, dt_bias,
           gdn_norm, q_norm, k_norm, w_out, w_fourier, final_norm):
    cvec = jnp.concatenate([c, c_ctx[None, :], jnp.zeros((8 - NB - 1, D), F32)], axis=0)
    b_mod3 = b_mod.reshape(DEPTH, 1, N_MOD * D)
    mod0 = _modulation(cvec, w_mod, b_mod3, 0)
    w_gu = ffn_w_gu[0, 0].astype(BF)
    w_dn = ffn_w_down[0, 0].astype(BF)

    h_ctx = _ffn(ctx.reshape(1, NB * TC, D), mod0, 0, NB, norm_w[0, 0], w_gu, w_dn).reshape(NB, TC, D)
    h, w_gu, w_dn, mod1 = _ffn(x, mod0, 0, 0, norm_w[0, 0], w_gu, w_dn, nxt=(ffn_w_gu, ffn_w_down, 0, 1),
                               side=(cvec, w_mod, b_mod3, 1))

    wi = w_in[0]
    w_a = wi[:, :NPA].astype(BF)
    w_b = jnp.concatenate([wi[:, NPA + 4 * NH:].astype(BF), wi[:, NPA:NPA + 4 * NH].astype(BF),
                           jnp.zeros((D, 128 - 4 * NH), BF)], axis=1)
    u = _prenorm(h, h_ctx, mod0, norm_w[0, 1]).reshape(NB * TT, D)
    pa = _matmul(u, w_a, 1024, 2048, BF, "in_proj_a").reshape(NB, TT, NPA)
    pb, gates = _matmul_split(u, w_b, 1024, NPB, "in_proj_b")
    pb = pb.reshape(NB, TT, NPB)

    qkv = _gdn_conv(pa, conv_w[0])
    prm = jnp.zeros((8, 128), F32)
    prm = prm.at[0, :2 * NH].set(a_log[0].reshape(-1)).at[1, :2 * NH].set(dt_bias[0].reshape(-1))
    o_f, o_b = _gdn_scan(qkv, gates.reshape(NB, TT, 128), prm)

    gains = jnp.concatenate([jnp.broadcast_to(q_norm[0] * (HD ** -0.5), (NH, HD)),
                             jnp.broadcast_to(k_norm[0], (KVH, HD))], axis=0).reshape(NH + KVH, 1, HD)
    cos_t, sin_t = _rope_tables()
    qk = _rope_norm(pb, gains, cos_t, sin_t)
    attn = _attention(qk, pb)

    h = _mixout(o_f, o_b, pa, attn, h, mod0, gdn_norm[0], w_out[0].astype(BF))
    h, w_gu, w_dn = _ffn(h, mod0, 6, 0, norm_w[0, 2], w_gu, w_dn, nxt=(ffn_w_gu, ffn_w_down, 1, 0))

    h, w_gu, w_dn = _ffn(h, mod1, 0, 0, norm_w[1, 0], w_gu, w_dn, nxt=(ffn_w_gu, ffn_w_down, 1, 1))
    ch_tab, tm_tab = _dft_tables()
    fa, fb = _fourier_channels(h, mod1, norm_w[1, 1], ch_tab)
    y = _fourier_time(tm_tab, fa, fb)
    h = _fourier_out(y, w_fourier[0].astype(BF), h, mod1)
    return _ffn(h, mod1, 6, 0, norm_w[1, 2], w_gu, w_dn, final_gain=final_norm)
```

```python
import functools
import math

import numpy as np
import jax
import jax.numpy as jnp
from jax import lax
from jax.experimental import pallas as pl
from jax.experimental.pallas import tpu as pltpu

D = 2048
NB = 4
T = 2048
GRID_W = 64
TC = 256
TT = T + TC
DEPTH = 2
EPS = 1e-6
N_MOD = 9
FH = 5632
NH = 8
HD = 128
KVH = 2
GQ = NH // KVH
CONV_K = 5
CS = 64
NCH = TT // CS
NCH_LAT = T // CS
ROPE_THETA = 10000.0
FG = 4
GC = D // FG
W_A = NH * HD
NPA = 4 * W_A
COL_Z = 3 * W_A
NPB = W_A + 2 * KVH * HD
COL_KB = W_A
COL_VB = COL_KB + KVH * HD

VMEM_LIMIT = 58 * 1024 * 1024
BF = jnp.bfloat16
F32 = jnp.float32


def _cp(sem):
    return pltpu.CompilerParams(dimension_semantics=sem, vmem_limit_bytes=VMEM_LIMIT)


def _dot(a, b):
    return jnp.dot(a, b, preferred_element_type=F32)


def _dot_nt(a, b):
    return lax.dot_general(a, b, (((1,), (1,)), ((), ())), preferred_element_type=F32)


def _dot_tn(a, b):
    return lax.dot_general(a, b, (((0,), (0,)), ((), ())), preferred_element_type=F32)


def _silu(x):
    return x * jax.nn.sigmoid(x)


def _mod_kernel(c_ref, w_ref, b_ref, o_ref):
    a = _silu(c_ref[...]).astype(BF)
    o_ref[...] = _dot(a, w_ref[...].astype(BF)) + b_ref[...]


def _modulation(cvec, w_mod, b_mod, layer):
    tn = 1024
    per = D // tn
    return pl.pallas_call(
        _mod_kernel,
        grid=(N_MOD * per,),
        in_specs=[pl.BlockSpec((8, D), lambda j: (0, 0)),
                  pl.BlockSpec((None, D, tn), lambda j: (layer, 0, j)),
                  pl.BlockSpec((None, 1, tn), lambda j: (layer, 0, j))],
        out_specs=pl.BlockSpec((None, 8, tn), lambda j: (j // per, 0, j % per)),
        out_shape=jax.ShapeDtypeStruct((N_MOD, 8, D), F32),
        compiler_params=_cp(("arbitrary",)),
        name="modulation",
    )(cvec, w_mod, b_mod)


def _mod_spec(k):
    return pl.BlockSpec((None, 8, D), lambda *_: (k, 0, 0))


def _mrow(ref, r):
    return ref[pl.ds(r, 1), :]


def _modnorm_tiles(h_ref, u_ref, rs_ref, gain, shift, scale, rb=32):
    n, d = h_ref.shape
    g = gain * (1.0 + scale)
    lanes = rs_ref.shape[1]
    for c in range(n // rb):
        rows = slice(c * rb, (c + 1) * rb)
        acc = None
        for k in range(d // lanes):
            x = h_ref[rows, k * lanes:(k + 1) * lanes]
            acc = x * x if acc is None else acc + x * x
        ms = jnp.sum(acc, axis=-1, keepdims=True) * (1.0 / d)
        rs_ref[rows, :] = jnp.broadcast_to(lax.rsqrt(ms + EPS), (rb, lanes))
    for c in range(n // rb):
        rows = slice(c * rb, (c + 1) * rb)
        rs = rs_ref[rows, :]
        for k in range(d // lanes):
            cols = slice(k * lanes, (k + 1) * lanes)
            u_ref[rows, cols] = (h_ref[rows, cols] * rs * g[:, cols] + shift[:, cols]).astype(BF)


def _ffn_kernel(h_ref, sh_ref, sc_ref, gt_ref, gain_ref, wg_ref, wu_ref, wd_ref, *rest,
                nj, row0, final, convert, side, raw):
    rest = list(rest)
    fin_ref = rest.pop(0) if final else None
    if convert:
        ngu_ref, ndn_ref = rest.pop(0), rest.pop(0)
    if side:
        c_ref, wm_ref, bm_ref = rest.pop(0), rest.pop(0), rest.pop(0)
    o_ref = rest.pop(0)
    if convert:
        ogu_ref, odn_ref = rest.pop(0), rest.pop(0)
    if side:
        om_ref = rest.pop(0)
    if raw:
        owg_ref, owu_ref, owd_ref = rest.pop(0), rest.pop(0), rest.pop(0)
    u_ref, rs_ref = rest
    r = pl.program_id(0) + row0
    j = pl.program_id(2)

    @pl.when(j == 0)
    def _():
        _modnorm_tiles(h_ref, u_ref, rs_ref, gain_ref[...], _mrow(sh_ref, r), _mrow(sc_ref, r))
        o_ref[...] = jnp.zeros_like(o_ref)

    wg, wu, wd = wg_ref[...], wu_ref[...], wd_ref[...]
    if raw:
        wg, wu, wd = wg.astype(BF), wu.astype(BF), wd.astype(BF)
        owg_ref[...] = wg
        owu_ref[...] = wu
        owd_ref[...] = wd
    u = u_ref[...]
    g = _dot(u, wg)
    up = _dot(u, wu)
    act = (_silu(g) * up).astype(BF)
    o_ref[...] += _dot(act, wd)
    if convert:
        ogu_ref[...] = ngu_ref[...].astype(BF)
        odn_ref[...] = ndn_ref[...].astype(BF)
    if side:
        _mod_kernel(c_ref, wm_ref, bm_ref, om_ref)

    @pl.when(j == nj - 1)
    def _():
        res = h_ref[...] + 0.5 * _mrow(gt_ref, r) * o_ref[...]
        if final:
            res = res * lax.rsqrt(jnp.mean(res * res, axis=-1, keepdims=True) + EPS) * fin_ref[...]
        o_ref[...] = res


def _ffn(h, mod, base, mod_row0, gain, w_gu, w_down, final_gain=None, nxt=None, side=None, raw=None,
         tm=1024, tf=256):
    nb, rows, _ = h.shape
    nj = FH // tf
    ni = rows // tm
    vec = pl.BlockSpec((1, D), lambda b, i, j: (0, 0))
    if raw is not None:
        assert nb * ni == 1
        r_gu, r_dn, rl, rw = raw
        w_specs = [pl.BlockSpec((None, None, D, tf), lambda b, i, j: (rl, rw, 0, j)),
                   pl.BlockSpec((None, None, D, tf), lambda b, i, j: (rl, rw, 0, j + nj)),
                   pl.BlockSpec((None, None, tf, D), lambda b, i, j: (rl, rw, j, 0))]
        w_args = [r_gu, r_gu, r_dn]
        h_spec = pl.BlockSpec((None, tm, D), lambda b, i, j: (b, i, 0), pipeline_mode=pl.Buffered(1))
    else:
        split = isinstance(w_gu, tuple)
        w_specs = [pl.BlockSpec((D, tf), lambda b, i, j: (0, j)),
                   pl.BlockSpec((D, tf), lambda b, i, j: (0, j if split else j + nj)),
                   pl.BlockSpec((tf, D), lambda b, i, j: (j, 0))]
        w_args = [w_gu[0], w_gu[1], w_down] if split else [w_gu, w_gu, w_down]
        h_spec = pl.BlockSpec((None, tm, D), lambda b, i, j: (b, i, 0))
    in_specs = [h_spec, _mod_spec(base), _mod_spec(base + 1), _mod_spec(base + 2), vec] + w_specs
    args = [h, mod, mod, mod, gain.reshape(1, D)] + w_args
    out_specs = [pl.BlockSpec((None, tm, D), lambda b, i, j: (b, i, 0))]
    out_shape = [jax.ShapeDtypeStruct(h.shape, F32)]
    if final_gain is not None:
        in_specs.append(vec)
        args.append(final_gain.reshape(1, D))
    if nxt is not None:
        n_gu, n_dn, nl, nw = nxt
        nr = nb * ni
        gu_blk = (D // nr, 2 * FH // nj)
        dn_blk = (FH // (nr * nj), D)
        gu_idx = lambda b, i, j: (b * ni + i, j)
        dn_idx = lambda b, i, j: ((b * ni + i) * nj + j, 0)
        in_specs += [pl.BlockSpec((None, None) + gu_blk, lambda b, i, j: (nl, nw) + gu_idx(b, i, j)),
                     pl.BlockSpec((None, None) + dn_blk, lambda b, i, j: (nl, nw) + dn_idx(b, i, j))]
        args += [n_gu, n_dn]
        out_specs += [pl.BlockSpec(gu_blk, gu_idx), pl.BlockSpec(dn_blk, dn_idx)]
        out_shape += [jax.ShapeDtypeStruct((D, 2 * FH), BF), jax.ShapeDtypeStruct((FH, D), BF)]
    if side is not None:
        cvec, w_mod, b_mod, sl = side
        lanes = 128
        nslab = N_MOD * D // lanes
        assert nb * ni * nj >= nslab
        slab = lambda b, i, j: jnp.minimum((b * ni + i) * nj + j, nslab - 1)
        in_specs += [pl.BlockSpec((8, D), lambda b, i, j: (0, 0)),
                     pl.BlockSpec((None, D, lanes), lambda b, i, j: (sl, 0, slab(b, i, j))),
                     pl.BlockSpec((None, 1, lanes), lambda b, i, j: (sl, 0, slab(b, i, j)))]
        args += [cvec, w_mod, b_mod]
        per = D // lanes
        out_specs.append(pl.BlockSpec((None, 8, lanes), lambda b, i, j: (slab(b, i, j) // per, 0, slab(b, i, j) % per)))
        out_shape.append(jax.ShapeDtypeStruct((N_MOD, 8, D), F32))
    if raw is not None:
        out_specs += [pl.BlockSpec((D, tf), lambda b, i, j: (0, j)), pl.BlockSpec((D, tf), lambda b, i, j: (0, j)),
                      pl.BlockSpec((tf, D), lambda b, i, j: (j, 0))]
        out_shape += [jax.ShapeDtypeStruct((D, FH), BF), jax.ShapeDtypeStruct((D, FH), BF),
                      jax.ShapeDtypeStruct((FH, D), BF)]
    out = pl.pallas_call(
        functools.partial(_ffn_kernel, nj=nj, row0=mod_row0, final=final_gain is not None, convert=nxt is not None,
                          side=side is not None, raw=raw is not None),
        grid=(nb, ni, nj),
        in_specs=in_specs,
        out_specs=out_specs,
        out_shape=out_shape,
        scratch_shapes=[pltpu.VMEM((tm, D), BF), pltpu.VMEM((tm, 128), F32)],
        compiler_params=_cp(("arbitrary",) * 3 if side is not None else ("parallel", "parallel", "arbitrary")),
        name="ffn",
    )(*args)
    return out if len(out) > 1 else out[0]


def _prenorm_kernel(hl_ref, hc_ref, sh_ref, sc_ref, gain_ref, o_ref, rs_ref):
    b = pl.program_id(0)
    i = pl.program_id(1)

    @pl.when(i < T // TC)
    def _():
        _modnorm_tiles(hl_ref, o_ref, rs_ref, gain_ref[...], _mrow(sh_ref, b), _mrow(sc_ref, b))

    @pl.when(i >= T // TC)
    def _():
        _modnorm_tiles(hc_ref, o_ref, rs_ref, gain_ref[...], _mrow(sh_ref, NB), _mrow(sc_ref, NB))


def _prenorm(h, h_ctx, mod, gain):
    nlat = T // TC
    return pl.pallas_call(
        _prenorm_kernel,
        grid=(NB, TT // TC),
        in_specs=[pl.BlockSpec((None, TC, D), lambda b, i: (b, jnp.minimum(i, nlat - 1), 0)),
                  pl.BlockSpec((None, TC, D), lambda b, i: (b, 0, 0)),
                  _mod_spec(3), _mod_spec(4),
                  pl.BlockSpec((1, D), lambda b, i: (0, 0))],
        out_specs=pl.BlockSpec((None, TC, D), lambda b, i: (b, i, 0)),
        out_shape=jax.ShapeDtypeStruct((NB, TT, D), BF),
        scratch_shapes=[pltpu.VMEM((TC, 128), F32)],
        compiler_params=_cp(("parallel", "arbitrary")),
        name="prenorm",
    )(h, h_ctx, mod, mod, gain.reshape(1, D))


def _mm_kernel(a_ref, b_ref, o_ref):
    o_ref[...] = _dot(a_ref[...], b_ref[...]).astype(o_ref.dtype)


def _mm_split_kernel(a_ref, b_ref, o1_ref, o2_ref):
    r = _dot(a_ref[...], b_ref[...])
    n1 = o1_ref.shape[1]
    o1_ref[...] = r[:, :n1].astype(o1_ref.dtype)
    o2_ref[...] = r[:, n1:]


def _matmul_split(a, w, tm, n1, name):
    m, k = a.shape
    n = w.shape[1]
    return pl.pallas_call(
        _mm_split_kernel,
        grid=(m // tm,),
        in_specs=[pl.BlockSpec((tm, k), lambda i: (i, 0)),
                  pl.BlockSpec((k, n), lambda i: (0, 0))],
        out_specs=[pl.BlockSpec((tm, n1), lambda i: (i, 0)),
                   pl.BlockSpec((tm, n - n1), lambda i: (i, 0))],
        out_shape=[jax.ShapeDtypeStruct((m, n1), BF), jax.ShapeDtypeStruct((m, n - n1), F32)],
        compiler_params=_cp(("parallel",)),
        name=name,
    )(a, w)


def _matmul(a, w, tm, tn, out_dtype, name):
    m, k = a.shape
    n = w.shape[1]
    return pl.pallas_call(
        _mm_kernel,
        grid=(m // tm, n // tn),
        in_specs=[pl.BlockSpec((tm, k), lambda i, j: (i, 0)),
                  pl.BlockSpec((k, tn), lambda i, j: (0, j))],
        out_specs=pl.BlockSpec((tm, tn), lambda i, j: (i, j)),
        out_shape=jax.ShapeDtypeStruct((m, n), out_dtype),
        compiler_params=_cp(("parallel", "arbitrary")),
        name=name,
    )(a, w)


def _conv_kernel(x_ref, w_ref, o_ref, pad_ref):
    j = pl.program_id(1)
    half = CONV_K // 2
    nblk = x_ref.shape[1] // HD
    for c in range(nblk):
        cols = slice(c * HD, (c + 1) * HD)
        for r0, n in ((0, T), (T, TC)):
            pad_ref[0:8, :] = jnp.zeros((8, HD), F32)
            pad_ref[8:8 + n, :] = x_ref[r0:r0 + n, cols].astype(F32)
            pad_ref[8 + n:16 + n, :] = jnp.zeros((8, HD), F32)
            acc = jnp.zeros((n, HD), F32)
            for k in range(CONV_K):
                acc = acc + w_ref[k:k + 1, cols] * pad_ref[8 + k - half:8 + k - half + n, :]
            y = _silu(acc)
            yn = y * lax.rsqrt(jnp.sum(y * y, axis=-1, keepdims=True) + EPS)
            o_ref[r0:r0 + n, cols] = jnp.where(j * nblk + c < 2 * NH, yn, y).astype(o_ref.dtype)


def _gdn_conv(p, conv_w, nblk=3):
    return pl.pallas_call(
        _conv_kernel,
        grid=(NB, 3 * NH // nblk),
        in_specs=[pl.BlockSpec((None, TT, nblk * HD), lambda b, j: (b, 0, j)),
                  pl.BlockSpec((CONV_K, nblk * HD), lambda b, j: (0, j))],
        out_specs=pl.BlockSpec((None, TT, nblk * HD), lambda b, j: (b, 0, j)),
        out_shape=jax.ShapeDtypeStruct((NB, TT, 3 * W_A), BF),
        scratch_shapes=[pltpu.VMEM((T + 16, HD), F32)],
        compiler_params=_cp(("parallel", "parallel")),
        name="gdn_conv",
    )(p, conv_w)


def _split3(x):
    a = x.astype(BF)
    r = x - a.astype(F32)
    b = r.astype(BF)
    return a, b, (r - b.astype(F32)).astype(BF)


def _gdn_kernel(qf_ref, kf_ref, vf_ref, gf_ref, qb_ref, kb_ref, vb_ref, gb_ref, prm_ref, of_ref, ob_ref, s_ref):
    nbs = qf_ref.shape[0]
    @pl.when(pl.program_id(1) == 0)
    def _():
        s_ref[...] = jnp.zeros_like(s_ref)

    ri = lax.broadcasted_iota(jnp.int32, (CS, CS), 0)
    ci = lax.broadcasted_iota(jnp.int32, (CS, CS), 1)
    eye_f = jnp.where(ri == ci, 1.0, 0.0)
    pair = (ri // 2) == (ci // 2)
    levels = [((ri // (2 * bs)) == (ci // (2 * bs))) & ((ri // bs) != (ci // bs)) for bs in (2, 4, 8, 16, 32)]
    prm = prm_ref[...]
    a_neg = -jnp.exp(prm[0:1, :])
    dt_b = prm[1:2, :]

    ch = []
    dir_refs = ((qf_ref, kf_ref, vf_ref, gf_ref, of_ref), (qb_ref, kb_ref, vb_ref, gb_ref, ob_ref))
    for bb, d in [(bb, d) for bb in range(nbs) for d in range(2)]:
        q_ref, k_ref, v_ref, g_ref, o_ref = [r.at[bb] for r in dir_refs[d]]
        dm = (ri - ci) if d == 0 else (ci - ri)
        incl = dm >= 0
        strict = dm > 0
        tri = jnp.where(incl, 1.0, 0.0).astype(BF)
        tri_t = jnp.where(dm <= 0, 1.0, 0.0).astype(BF)
        gts = g_ref[...]
        x = gts + dt_b
        glog = a_neg * (jnp.maximum(x, 0.0) + jnp.log(1.0 + jnp.exp(-jnp.abs(x))))
        beta = jax.nn.sigmoid(gts)
        parts = _split3(glog)
        gc = sum(_dot(tri, g) for g in parts)
        gc_t = sum(_dot_tn(g, tri_t) for g in parts)
        last = CS - 1 if d == 0 else 0
        g_last = gc[last:last + 1, :]
        eg = jnp.exp(gc)
        ek = jnp.exp(g_last - gc)
        g_tot = jnp.exp(g_last)
        for h in range(NH):
            c = d * NH + h
            ch.append(dict(sl=slice(h * HD, (h + 1) * HD), c=bb * 2 * NH + c, incl=incl, strict=strict, o_ref=o_ref,
                           q_ref=q_ref, k_ref=k_ref, v_ref=v_ref,
                           b_c=beta[:, 2 * NH + c:2 * NH + c + 1], gc_c=gc[:, c:c + 1], gc_r=gc_t[c:c + 1, :],
                           eg=eg[:, c:c + 1], ek=ek[:, c:c + 1], g_tot=g_tot[:, c:c + 1]))

    for x in ch:
        kb = x["k_ref"][:, x["sl"]]
        qh = x["q_ref"][:, x["sl"]].astype(F32) * (HD ** -0.5)
        x["kh"], x["qh"] = kb.astype(F32), qh
        kq = _dot_nt(jnp.concatenate([kb, qh.astype(BF)], axis=0), kb)
        x["kk"], x["qk"] = kq[:CS], kq[CS:]
    for x in ch:
        decay = jnp.exp(jnp.where(x["incl"], x["gc_c"] - x["gc_r"], -1e30))
        lower = jnp.where(x["strict"], x["b_c"] * x["kk"] * decay, 0.0)
        x["qk"] = (x["qk"] * decay).astype(BF)
        x["lower"] = lower.astype(BF)
        x["t"] = eye_f - jnp.where(pair, lower, 0.0)
    for off in levels:
        for x in ch:
            x["tb"] = x["t"].astype(BF)
            x["mt"] = _dot(jnp.where(off, x["lower"], jnp.zeros_like(x["lower"])), x["tb"]).astype(BF)
        for x in ch:
            x["t"] = x["t"] - _dot(x["tb"], x["mt"])
    for x in ch:
        vh = x["v_ref"][:, x["sl"]].astype(F32)
        rhs = jnp.concatenate([vh * x["b_c"], x["kh"] * (x["b_c"] * x["eg"])], axis=1).astype(BF)
        x["uw"] = _dot(x["t"].astype(BF), rhs)
        x["q_dec"] = (x["qh"] * x["eg"]).astype(BF)
        x["k_dec"] = (x["kh"] * x["ek"]).astype(BF)
    for x in ch:
        st = s_ref[x["c"]]
        x["st"] = st
        stb = st.astype(BF)
        ws = _dot(jnp.concatenate([x["uw"][:, HD:].astype(BF), x["q_dec"]], axis=0), stb)
        x["vn"] = (x["uw"][:, :HD] - ws[:CS]).astype(BF)
        x["qs"] = ws[CS:]
    for x in ch:
        x["o_ref"][:, x["sl"]] = x["qs"] + _dot(x["qk"], x["vn"])
        s_ref[x["c"]] = x["st"] * x["g_tot"] + _dot_tn(x["k_dec"], x["vn"])


def _gdn_scan(qkv, gates, prm, nbs=4):
    chunk_f = lambda s: (s + NCH_LAT) % NCH
    chunk_b = lambda s: NCH - 1 - s

    def specs(chunk):
        qkv_spec = lambda c: pl.BlockSpec((nbs, CS, W_A), lambda b, s: (b, chunk(s), c))
        return [qkv_spec(0), qkv_spec(1), qkv_spec(2),
                pl.BlockSpec((nbs, CS, 128), lambda b, s: (b, chunk(s), 0))]

    return pl.pallas_call(
        _gdn_kernel,
        grid=(NB // nbs, NCH),
        in_specs=specs(chunk_f) + specs(chunk_b) + [pl.BlockSpec((8, 128), lambda b, s: (0, 0))],
        out_specs=[pl.BlockSpec((nbs, CS, W_A), lambda b, s: (b, chunk_f(s), 0)),
                   pl.BlockSpec((nbs, CS, W_A), lambda b, s: (b, chunk_b(s), 0))],
        out_shape=[jax.ShapeDtypeStruct((NB, TT, W_A), F32)] * 2,
        scratch_shapes=[pltpu.VMEM((nbs * 2 * NH, HD, HD), F32)],
        compiler_params=_cp(("parallel", "arbitrary")),
        name="gdn_scan",
    )(qkv, qkv, qkv, gates, qkv, qkv, qkv, gates, prm)


def _rope_kernel(x_ref, g_ref, cos_ref, sin_ref, o_ref):
    lane = lax.broadcasted_iota(jnp.int32, (TT, HD), 1)
    first = (lane % 64) < 32
    for hd in range(NH + KVH):
        cols = slice(hd * HD, (hd + 1) * HD)
        x = x_ref[:, cols].astype(F32)
        y = x * lax.rsqrt(jnp.mean(x * x, axis=-1, keepdims=True) + EPS) * g_ref[hd]
        partner = jnp.where(first, pltpu.roll(y, 128 - 32, axis=1), pltpu.roll(y, 32, axis=1))
        o_ref[:, cols] = (y * cos_ref[...] + partner * sin_ref[...]).astype(o_ref.dtype)


def _rope_norm(pb, gains, cos_t, sin_t):
    nh = NH + KVH
    return pl.pallas_call(
        _rope_kernel,
        grid=(NB,),
        in_specs=[pl.BlockSpec((None, TT, nh * HD), lambda b: (b, 0, 0)),
                  pl.BlockSpec((nh, 1, HD), lambda b: (0, 0, 0)),
                  pl.BlockSpec((TT, HD), lambda b: (0, 0)),
                  pl.BlockSpec((TT, HD), lambda b: (0, 0))],
        out_specs=pl.BlockSpec((None, TT, nh * HD), lambda b: (b, 0, 0)),
        out_shape=jax.ShapeDtypeStruct((NB, TT, nh * HD), BF),
        compiler_params=_cp(("parallel",)),
        name="rope_norm",
    )(pb, gains, cos_t, sin_t)


def _attn_kernel(q_ref, k_ref, v_ref, o_ref, s_ref, p_ref):
    k = k_ref[...]
    v_aug = jnp.concatenate([v_ref[...], jnp.ones((TT, HD), BF)], axis=1)
    tq = q_ref.shape[0]
    rb = 16
    for g in range(GQ):
        sl = slice(g * HD, (g + 1) * HD)
        buf = g % 2
        s_ref[buf] = _dot_nt(q_ref[:, sl], k)
        for r in range(tq // rb):
            rows = slice(r * rb, (r + 1) * rb)
            m = s_ref[buf, rows, 0:HD]
            for c in range(1, TT // HD):
                m = jnp.maximum(m, s_ref[buf, rows, c * HD:(c + 1) * HD])
            mx = jnp.broadcast_to(jnp.max(m, axis=-1, keepdims=True), (rb, HD))
            for c in range(TT // HD):
                cols = slice(c * HD, (c + 1) * HD)
                p_ref[buf, rows, cols] = jnp.exp(s_ref[buf, rows, cols] - mx).astype(BF)
        o = _dot(p_ref[buf], v_aug)
        o_ref[:, sl] = (o[:, :HD] / o[:, HD:]).astype(o_ref.dtype)


def _attention(qk, pb, tq=512):
    return pl.pallas_call(
        _attn_kernel,
        grid=(NB, KVH, T // tq),
        in_specs=[pl.BlockSpec((None, tq, GQ * HD), lambda b, kv, i: (b, i, kv)),
                  pl.BlockSpec((None, TT, HD), lambda b, kv, i: (b, 0, NH + kv)),
                  pl.BlockSpec((None, TT, HD), lambda b, kv, i: (b, 0, COL_VB // HD + kv))],
        out_specs=pl.BlockSpec((None, tq, GQ * HD), lambda b, kv, i: (b, i, kv)),
        out_shape=jax.ShapeDtypeStruct((NB, T, W_A), BF),
        scratch_shapes=[pltpu.VMEM((2, tq, TT), F32), pltpu.VMEM((2, tq, TT), BF)],
        compiler_params=_cp(("parallel", "parallel", "arbitrary")),
        name="attention",
    )(qk, qk, pb)


def _mixout_kernel(of_ref, ob_ref, z_ref, a_ref, h_ref, gt_ref, gain_ref, w_ref, o_ref):
    parts = []
    for hd in range(NH):
        sl = slice(hd * HD, (hd + 1) * HD)
        o = of_ref[:, sl] + ob_ref[:, sl]
        n = o * lax.rsqrt(jnp.mean(o * o, axis=-1, keepdims=True) + EPS) * gain_ref[...]
        parts.append((n * _silu(z_ref[:, sl].astype(F32))).astype(BF))
    gd = jnp.concatenate(parts, axis=1)
    y = _dot(gd, w_ref[0:W_A, :]) + _dot(a_ref[...], w_ref[W_A:2 * W_A, :])
    o_ref[...] = h_ref[...] + _mrow(gt_ref, pl.program_id(0)) * y


def _mixout(o_f, o_b, pa, attn, h, mod, gain, w_out, tm=256):
    return pl.pallas_call(
        _mixout_kernel,
        grid=(NB, T // tm),
        in_specs=[pl.BlockSpec((None, tm, W_A), lambda b, i: (b, i, 0)),
                  pl.BlockSpec((None, tm, W_A), lambda b, i: (b, i, 0)),
                  pl.BlockSpec((None, tm, W_A), lambda b, i: (b, i, COL_Z // W_A)),
                  pl.BlockSpec((None, tm, W_A), lambda b, i: (b, i, 0)),
                  pl.BlockSpec((None, tm, D), lambda b, i: (b, i, 0)),
                  _mod_spec(5),
                  pl.BlockSpec((1, HD), lambda b, i: (0, 0)),
                  pl.BlockSpec((D, D), lambda b, i: (0, 0))],
        out_specs=pl.BlockSpec((None, tm, D), lambda b, i: (b, i, 0)),
        out_shape=jax.ShapeDtypeStruct((NB, T, D), F32),
        compiler_params=_cp(("parallel", "parallel")),
        name="mix_out",
    )(o_f, o_b, pa, attn, h, mod, gain.reshape(1, HD), w_out)


def _dft_tables():
    def cs(n):
        idx = np.arange(n, dtype=np.int64)
        ang = 2.0 * np.pi * ((idx[:, None] * idx[None, :]) % n).astype(np.float64) / n
        return np.cos(ang) / math.sqrt(n), np.sin(ang) / math.sqrt(n)

    cc, sc = cs(GC)
    ct, st = cs(T)
    ch = np.concatenate([cc, sc], axis=1).astype(np.float32)
    tm = np.concatenate([ct, -st], axis=1).astype(np.float32)
    return jnp.asarray(ch, dtype=BF), jnp.asarray(tm, dtype=BF)


def _fchan_kernel(h_ref, sh_ref, sc_ref, gain_ref, w_ref, a_ref, b_ref, u_ref, rs_ref):
    b = pl.program_id(0)
    _modnorm_tiles(h_ref, u_ref, rs_ref, gain_ref[...], _mrow(sh_ref, b), _mrow(sc_ref, b))
    w = w_ref[...]
    for g in range(FG):
        sl = slice(g * GC, (g + 1) * GC)
        ab = _dot(u_ref[:, sl], w)
        a_ref[:, sl] = ab[:, :GC].astype(a_ref.dtype)
        b_ref[:, sl] = ab[:, GC:].astype(b_ref.dtype)


def _fourier_channels(h, mod, gain, ch_tab, tm=512):
    return pl.pallas_call(
        _fchan_kernel,
        grid=(NB, T // tm),
        in_specs=[pl.BlockSpec((None, tm, D), lambda b, i: (b, i, 0)),
                  _mod_spec(3), _mod_spec(4),
                  pl.BlockSpec((1, D), lambda b, i: (0, 0)),
                  pl.BlockSpec((GC, 2 * GC), lambda b, i: (0, 0))],
        out_specs=[pl.BlockSpec((None, tm, D), lambda b, i: (b, i, 0))] * 2,
        out_shape=[jax.ShapeDtypeStruct((NB, T, D), BF)] * 2,
        scratch_shapes=[pltpu.VMEM((tm, D), BF), pltpu.VMEM((tm, 128), F32)],
        compiler_params=_cp(("parallel", "parallel")),
        name="fourier_channels",
    )(h, mod, mod, gain.reshape(1, D), ch_tab)


def _ftime_kernel(t_ref, a_ref, b_ref, o_ref):
    y = _dot(t_ref[:, :T], a_ref[...]) + _dot(t_ref[:, T:], b_ref[...])
    o_ref[...] = y.astype(o_ref.dtype)


def _fourier_time(tm_tab, a, b, tm=1024, tn=1024):
    return pl.pallas_call(
        _ftime_kernel,
        grid=(NB, D // tn, T // tm),
        in_specs=[pl.BlockSpec((tm, 2 * T), lambda b, j, i: (i, 0)),
                  pl.BlockSpec((None, T, tn), lambda b, j, i: (b, 0, j)),
                  pl.BlockSpec((None, T, tn), lambda b, j, i: (b, 0, j))],
        out_specs=pl.BlockSpec((None, tm, tn), lambda b, j, i: (b, i, j)),
        out_shape=jax.ShapeDtypeStruct((NB, T, D), BF),
        compiler_params=_cp(("parallel", "parallel", "arbitrary")),
        name="fourier_time",
    )(tm_tab, a, b)


def _resmm_kernel(y_ref, w_ref, h_ref, gt_ref, o_ref):
    o_ref[...] = h_ref[...] + _mrow(gt_ref, pl.program_id(0)) * _dot(y_ref[...], w_ref[...])


def _fourier_out(y, w_f, h, mod, tm=512):
    return pl.pallas_call(
        _resmm_kernel,
        grid=(NB, T // tm),
        in_specs=[pl.BlockSpec((None, tm, D), lambda b, i: (b, i, 0)),
                  pl.BlockSpec((D, D), lambda b, i: (0, 0)),
                  pl.BlockSpec((None, tm, D), lambda b, i: (b, i, 0)),
                  _mod_spec(5)],
        out_specs=pl.BlockSpec((None, tm, D), lambda b, i: (b, i, 0)),
        out_shape=jax.ShapeDtypeStruct((NB, T, D), F32),
        compiler_params=_cp(("parallel", "parallel")),
        name="fourier_out",
    )(y, w_f, h, mod)


def _rope_tables():
    half = HD // 4
    inv = ROPE_THETA ** (-np.arange(half, dtype=np.float64) / half)
    t = np.arange(T)
    ang_r = (t // GRID_W)[:, None] * inv[None, :]
    ang_c = (t % GRID_W)[:, None] * inv[None, :]
    cos = np.concatenate([np.cos(ang_r)] * 2 + [np.cos(ang_c)] * 2, axis=1)
    sin = np.concatenate([-np.sin(ang_r), np.sin(ang_r), -np.sin(ang_c), np.sin(ang_c)], axis=1)
    cos = np.concatenate([cos, np.ones((TC, HD))], axis=0).astype(np.float32)
    sin = np.concatenate([sin, np.zeros((TC, HD))], axis=0).astype(np.float32)
    return jnp.asarray(cos), jnp.asarray(sin)


def kernel(x, c, ctx, c_ctx, w_mod, b_mod, norm_w, ffn_w_gu, ffn_w_down, w_in, conv_w, a_log, dt_bias,
           gdn_norm, q_norm, k_norm, w_out, w_fourier, final_norm):
    cvec = jnp.concatenate([c, c_ctx[None, :], jnp.zeros((8 - NB - 1, D), F32)], axis=0)
    b_mod3 = b_mod.reshape(DEPTH, 1, N_MOD * D)
    mod0 = _modulation(cvec, w_mod, b_mod3, 0)

    h_ctx, w_g, w_u, w_dn = _ffn(ctx.reshape(1, NB * TC, D), mod0, 0, NB, norm_w[0, 0], None, None,
                                 raw=(ffn_w_gu, ffn_w_down, 0, 0))
    h_ctx = h_ctx.reshape(NB, TC, D)
    h, w_gu, w_dn, mod1 = _ffn(x, mod0, 0, 0, norm_w[0, 0], (w_g, w_u), w_dn, nxt=(ffn_w_gu, ffn_w_down, 0, 1),
                               side=(cvec, w_mod, b_mod3, 1))

    wi = w_in[0]
    w_a = wi[:, :NPA].astype(BF)
    w_b = jnp.concatenate([wi[:, NPA + 4 * NH:].astype(BF), wi[:, NPA:NPA + 4 * NH].astype(BF),
                           jnp.zeros((D, 128 - 4 * NH), BF)], axis=1)
    u = _prenorm(h, h_ctx, mod0, norm_w[0, 1]).reshape(NB * TT, D)
    pa = _matmul(u, w_a, 1024, 2048, BF, "in_proj_a").reshape(NB, TT, NPA)
    pb, gates = _matmul_split(u, w_b, 1024, NPB, "in_proj_b")
    pb = pb.reshape(NB, TT, NPB)

    qkv = _gdn_conv(pa, conv_w[0])
    prm = jnp.zeros((8, 128), F32)
    prm = prm.at[0, :2 * NH].set(a_log[0].reshape(-1)).at[1, :2 * NH].set(dt_bias[0].reshape(-1))
    o_f, o_b = _gdn_scan(qkv, gates.reshape(NB, TT, 128), prm)

    gains = jnp.concatenate([jnp.broadcast_to(q_norm[0] * (HD ** -0.5), (NH, HD)),
                             jnp.broadcast_to(k_norm[0], (KVH, HD))], axis=0).reshape(NH + KVH, 1, HD)
    cos_t, sin_t = _rope_tables()
    qk = _rope_norm(pb, gains, cos_t, sin_t)
    attn = _attention(qk, pb)

    h = _mixout(o_f, o_b, pa, attn, h, mod0, gdn_norm[0], w_out[0].astype(BF))
    h, w_gu, w_dn = _ffn(h, mod0, 6, 0, norm_w[0, 2], w_gu, w_dn, nxt=(ffn_w_gu, ffn_w_down, 1, 0))

    h, w_gu, w_dn = _ffn(h, mod1, 0, 0, norm_w[1, 0], w_gu, w_dn, nxt=(ffn_w_gu, ffn_w_down, 1, 1))
    ch_tab, tm_tab = _dft_tables()
    fa, fb = _fourier_channels(h, mod1, norm_w[1, 1], ch_tab)
    y = _fourier_time(tm_tab, fa, fb)
    h = _fourier_out(y, w_fourier[0].astype(BF), h, mod1)
    return _ffn(h, mod1, 6, 0, norm_w[1, 2], w_gu, w_dn, final_gain=final_norm)
```

```python
import functools
import math

import numpy as np
import jax
import jax.numpy as jnp
from jax import lax
from jax.experimental import pallas as pl
from jax.experimental.pallas import tpu as pltpu

D = 2048
NB = 4
T = 2048
GRID_W = 64
TC = 256
TT = T + TC
DEPTH = 2
EPS = 1e-6
N_MOD = 9
FH = 5632
NH = 8
HD = 128
KVH = 2
GQ = NH // KVH
CONV_K = 5
CS = 64
NCH = TT // CS
NCH_LAT = T // CS
ROPE_THETA = 10000.0
FG = 4
GC = D // FG
W_A = NH * HD
NPA = 4 * W_A
COL_Z = 3 * W_A
NPB = W_A + 2 * KVH * HD
COL_KB = W_A
COL_VB = COL_KB + KVH * HD

VMEM_LIMIT = 58 * 1024 * 1024
BF = jnp.bfloat16
F32 = jnp.float32


def _cp(sem):
    return pltpu.CompilerParams(dimension_semantics=sem, vmem_limit_bytes=VMEM_LIMIT)


def _dot(a, b):
    return jnp.dot(a, b, preferred_element_type=F32)


def _dot_nt(a, b):
    return lax.dot_general(a, b, (((1,), (1,)), ((), ())), preferred_element_type=F32)


def _dot_tn(a, b):
    return lax.dot_general(a, b, (((0,), (0,)), ((), ())), preferred_element_type=F32)


def _silu(x):
    return x * jax.nn.sigmoid(x)


def _mod_kernel(c_ref, w_ref, b_ref, o_ref):
    a = _silu(c_ref[...]).astype(BF)
    o_ref[...] = _dot(a, w_ref[...].astype(BF)) + b_ref[...]


def _modulation(cvec, w_mod, b_mod, layer):
    tn = 1024
    per = D // tn
    return pl.pallas_call(
        _mod_kernel,
        grid=(N_MOD * per,),
        in_specs=[pl.BlockSpec((8, D), lambda j: (0, 0)),
                  pl.BlockSpec((None, D, tn), lambda j: (layer, 0, j)),
                  pl.BlockSpec((None, 1, tn), lambda j: (layer, 0, j))],
        out_specs=pl.BlockSpec((None, 8, tn), lambda j: (j // per, 0, j % per)),
        out_shape=jax.ShapeDtypeStruct((N_MOD, 8, D), F32),
        compiler_params=_cp(("arbitrary",)),
        name="modulation",
    )(cvec, w_mod, b_mod)


def _mod_spec(k):
    return pl.BlockSpec((None, 8, D), lambda *_: (k, 0, 0))


def _mrow(ref, r):
    return ref[pl.ds(r, 1), :]


def _modnorm_tiles(h_ref, u_ref, rs_ref, gain, shift, scale, rb=32):
    n, d = h_ref.shape
    g = gain * (1.0 + scale)
    lanes = rs_ref.shape[1]
    for c in range(n // rb):
        rows = slice(c * rb, (c + 1) * rb)
        acc = None
        for k in range(d // lanes):
            x = h_ref[rows, k * lanes:(k + 1) * lanes]
            acc = x * x if acc is None else acc + x * x
        ms = jnp.sum(acc, axis=-1, keepdims=True) * (1.0 / d)
        rs_ref[rows, :] = jnp.broadcast_to(lax.rsqrt(ms + EPS), (rb, lanes))
    for c in range(n // rb):
        rows = slice(c * rb, (c + 1) * rb)
        rs = rs_ref[rows, :]
        for k in range(d // lanes):
            cols = slice(k * lanes, (k + 1) * lanes)
            u_ref[rows, cols] = (h_ref[rows, cols] * rs * g[:, cols] + shift[:, cols]).astype(BF)


def _ffn_kernel(h_ref, sh_ref, sc_ref, gt_ref, gain_ref, wg_ref, wu_ref, wd_ref, *rest,
                nj, row0, final, convert, side, raw):
    rest = list(rest)
    fin_ref = rest.pop(0) if final else None
    if convert:
        ngu_ref, ndn_ref = rest.pop(0), rest.pop(0)
    if side:
        c_ref, wm_ref, bm_ref = rest.pop(0), rest.pop(0), rest.pop(0)
    o_ref = rest.pop(0)
    if convert:
        ogu_ref, odn_ref = rest.pop(0), rest.pop(0)
    if side:
        om_ref = rest.pop(0)
    if raw:
        owg_ref, owu_ref, owd_ref = rest.pop(0), rest.pop(0), rest.pop(0)
    u_ref, rs_ref = rest
    r = pl.program_id(0) + row0
    j = pl.program_id(2)

    @pl.when(j == 0)
    def _():
        _modnorm_tiles(h_ref, u_ref, rs_ref, gain_ref[...], _mrow(sh_ref, r), _mrow(sc_ref, r))
        o_ref[...] = jnp.zeros_like(o_ref)

    wg, wu, wd = wg_ref[...], wu_ref[...], wd_ref[...]
    if raw:
        wg, wu, wd = wg.astype(BF), wu.astype(BF), wd.astype(BF)
        owg_ref[...] = wg
        owu_ref[...] = wu
        owd_ref[...] = wd
    u = u_ref[...]
    g = _dot(u, wg)
    up = _dot(u, wu)
    act = (_silu(g) * up).astype(BF)
    o_ref[...] += _dot(act, wd)
    if convert:
        ogu_ref[...] = ngu_ref[...].astype(BF)
        odn_ref[...] = ndn_ref[...].astype(BF)
    if side:
        _mod_kernel(c_ref, wm_ref, bm_ref, om_ref)

    @pl.when(j == nj - 1)
    def _():
        res = h_ref[...] + 0.5 * _mrow(gt_ref, r) * o_ref[...]
        if final:
            res = res * lax.rsqrt(jnp.mean(res * res, axis=-1, keepdims=True) + EPS) * fin_ref[...]
        o_ref[...] = res


def _ffn(h, mod, base, mod_row0, gain, w_gu, w_down, final_gain=None, nxt=None, side=None, raw=None,
         tm=1024, tf=256):
    nb, rows, _ = h.shape
    nj = FH // tf
    ni = rows // tm
    vec = pl.BlockSpec((1, D), lambda b, i, j: (0, 0))
    if raw is not None:
        assert nb * ni == 1
        r_gu, r_dn, rl, rw = raw
        w_specs = [pl.BlockSpec((None, None, D, tf), lambda b, i, j: (rl, rw, 0, j)),
                   pl.BlockSpec((None, None, D, tf), lambda b, i, j: (rl, rw, 0, j + nj)),
                   pl.BlockSpec((None, None, tf, D), lambda b, i, j: (rl, rw, j, 0))]
        w_args = [r_gu, r_gu, r_dn]
        h_spec = pl.BlockSpec((None, tm, D), lambda b, i, j: (b, i, 0), pipeline_mode=pl.Buffered(1))
    else:
        split = isinstance(w_gu, tuple)
        w_specs = [pl.BlockSpec((D, tf), lambda b, i, j: (0, j)),
                   pl.BlockSpec((D, tf), lambda b, i, j: (0, j if split else j + nj)),
                   pl.BlockSpec((tf, D), lambda b, i, j: (j, 0))]
        w_args = [w_gu[0], w_gu[1], w_down] if split else [w_gu, w_gu, w_down]
        h_spec = pl.BlockSpec((None, tm, D), lambda b, i, j: (b, i, 0))
    in_specs = [h_spec, _mod_spec(base), _mod_spec(base + 1), _mod_spec(base + 2), vec] + w_specs
    args = [h, mod, mod, mod, gain.reshape(1, D)] + w_args
    out_specs = [pl.BlockSpec((None, tm, D), lambda b, i, j: (b, i, 0))]
    out_shape = [jax.ShapeDtypeStruct(h.shape, F32)]
    if final_gain is not None:
        in_specs.append(vec)
        args.append(final_gain.reshape(1, D))
    if nxt is not None:
        n_gu, n_dn, nl, nw = nxt
        nr = nb * ni
        gu_blk = (D // nr, 2 * FH // nj)
        dn_blk = (FH // (nr * nj), D)
        gu_idx = lambda b, i, j: (b * ni + i, j)
        dn_idx = lambda b, i, j: ((b * ni + i) * nj + j, 0)
        in_specs += [pl.BlockSpec((None, None) + gu_blk, lambda b, i, j: (nl, nw) + gu_idx(b, i, j)),
                     pl.BlockSpec((None, None) + dn_blk, lambda b, i, j: (nl, nw) + dn_idx(b, i, j))]
        args += [n_gu, n_dn]
        out_specs += [pl.BlockSpec(gu_blk, gu_idx), pl.BlockSpec(dn_blk, dn_idx)]
        out_shape += [jax.ShapeDtypeStruct((D, 2 * FH), BF), jax.ShapeDtypeStruct((FH, D), BF)]
    if side is not None:
        cvec, w_mod, b_mod, sl = side
        lanes = 128
        nslab = N_MOD * D // lanes
        assert nb * ni * nj >= nslab
        slab = lambda b, i, j: jnp.minimum((b * ni + i) * nj + j, nslab - 1)
        in_specs += [pl.BlockSpec((8, D), lambda b, i, j: (0, 0)),
                     pl.BlockSpec((None, D, lanes), lambda b, i, j: (sl, 0, slab(b, i, j))),
                     pl.BlockSpec((None, 1, lanes), lambda b, i, j: (sl, 0, slab(b, i, j)))]
        args += [cvec, w_mod, b_mod]
        per = D // lanes
        out_specs.append(pl.BlockSpec((None, 8, lanes), lambda b, i, j: (slab(b, i, j) // per, 0, slab(b, i, j) % per)))
        out_shape.append(jax.ShapeDtypeStruct((N_MOD, 8, D), F32))
    if raw is not None:
        out_specs += [pl.BlockSpec((D, tf), lambda b, i, j: (0, j)), pl.BlockSpec((D, tf), lambda b, i, j: (0, j)),
                      pl.BlockSpec((tf, D), lambda b, i, j: (j, 0))]
        out_shape += [jax.ShapeDtypeStruct((D, FH), BF), jax.ShapeDtypeStruct((D, FH), BF),
                      jax.ShapeDtypeStruct((FH, D), BF)]
    out = pl.pallas_call(
        functools.partial(_ffn_kernel, nj=nj, row0=mod_row0, final=final_gain is not None, convert=nxt is not None,
                          side=side is not None, raw=raw is not None),
        grid=(nb, ni, nj),
        in_specs=in_specs,
        out_specs=out_specs,
        out_shape=out_shape,
        scratch_shapes=[pltpu.VMEM((tm, D), BF), pltpu.VMEM((tm, 128), F32)],
        compiler_params=_cp(("arbitrary",) * 3 if side is not None else ("parallel", "parallel", "arbitrary")),
        name="ffn",
    )(*args)
    return out if len(out) > 1 else out[0]


def _prenorm_kernel(hl_ref, hc_ref, sh_ref, sc_ref, gain_ref, o_ref, rs_ref):
    b = pl.program_id(0)
    i = pl.program_id(1)

    @pl.when(i < T // TC)
    def _():
        _modnorm_tiles(hl_ref, o_ref, rs_ref, gain_ref[...], _mrow(sh_ref, b), _mrow(sc_ref, b))

    @pl.when(i >= T // TC)
    def _():
        _modnorm_tiles(hc_ref, o_ref, rs_ref, gain_ref[...], _mrow(sh_ref, NB), _mrow(sc_ref, NB))


def _prenorm(h, h_ctx, mod, gain):
    nlat = T // TC
    return pl.pallas_call(
        _prenorm_kernel,
        grid=(NB, TT // TC),
        in_specs=[pl.BlockSpec((None, TC, D), lambda b, i: (b, jnp.minimum(i, nlat - 1), 0)),
                  pl.BlockSpec((None, TC, D), lambda b, i: (b, 0, 0)),
                  _mod_spec(3), _mod_spec(4),
                  pl.BlockSpec((1, D), lambda b, i: (0, 0))],
        out_specs=pl.BlockSpec((None, TC, D), lambda b, i: (b, i, 0)),
        out_shape=jax.ShapeDtypeStruct((NB, TT, D), BF),
        scratch_shapes=[pltpu.VMEM((TC, 128), F32)],
        compiler_params=_cp(("parallel", "arbitrary")),
        name="prenorm",
    )(h, h_ctx, mod, mod, gain.reshape(1, D))


def _mm_kernel(a_ref, b_ref, o_ref):
    o_ref[...] = _dot(a_ref[...], b_ref[...]).astype(o_ref.dtype)


def _mm_split_kernel(a_ref, b_ref, o1_ref, o2_ref):
    r = _dot(a_ref[...], b_ref[...])
    n1 = o1_ref.shape[1]
    o1_ref[...] = r[:, :n1].astype(o1_ref.dtype)
    o2_ref[...] = r[:, n1:]


def _matmul_split(a, w, tm, n1, name):
    m, k = a.shape
    n = w.shape[1]
    return pl.pallas_call(
        _mm_split_kernel,
        grid=(m // tm,),
        in_specs=[pl.BlockSpec((tm, k), lambda i: (i, 0)),
                  pl.BlockSpec((k, n), lambda i: (0, 0))],
        out_specs=[pl.BlockSpec((tm, n1), lambda i: (i, 0)),
                   pl.BlockSpec((tm, n - n1), lambda i: (i, 0))],
        out_shape=[jax.ShapeDtypeStruct((m, n1), BF), jax.ShapeDtypeStruct((m, n - n1), F32)],
        compiler_params=_cp(("parallel",)),
        name=name,
    )(a, w)


def _matmul(a, w, tm, tn, out_dtype, name):
    m, k = a.shape
    n = w.shape[1]
    return pl.pallas_call(
        _mm_kernel,
        grid=(m // tm, n // tn),
        in_specs=[pl.BlockSpec((tm, k), lambda i, j: (i, 0)),
                  pl.BlockSpec((k, tn), lambda i, j: (0, j))],
        out_specs=pl.BlockSpec((tm, tn), lambda i, j: (i, j)),
        out_shape=jax.ShapeDtypeStruct((m, n), out_dtype),
        compiler_params=_cp(("parallel", "arbitrary")),
        name=name,
    )(a, w)


def _conv_kernel(x_ref, w_ref, o_ref, pad_ref):
    j = pl.program_id(1)
    half = CONV_K // 2
    nblk = x_ref.shape[1] // HD
    for c in range(nblk):
        cols = slice(c * HD, (c + 1) * HD)
        for r0, n in ((0, T), (T, TC)):
            pad_ref[0:8, :] = jnp.zeros((8, HD), F32)
            pad_ref[8:8 + n, :] = x_ref[r0:r0 + n, cols].astype(F32)
            pad_ref[8 + n:16 + n, :] = jnp.zeros((8, HD), F32)
            acc = jnp.zeros((n, HD), F32)
            for k in range(CONV_K):
                acc = acc + w_ref[k:k + 1, cols] * pad_ref[8 + k - half:8 + k - half + n, :]
            y = _silu(acc)
            yn = y * lax.rsqrt(jnp.sum(y * y, axis=-1, keepdims=True) + EPS)
            o_ref[r0:r0 + n, cols] = jnp.where(j * nblk + c < 2 * NH, yn, y).astype(o_ref.dtype)


def _gdn_conv(p, conv_w, nblk=3):
    return pl.pallas_call(
        _conv_kernel,
        grid=(NB, 3 * NH // nblk),
        in_specs=[pl.BlockSpec((None, TT, nblk * HD), lambda b, j: (b, 0, j)),
                  pl.BlockSpec((CONV_K, nblk * HD), lambda b, j: (0, j))],
        out_specs=pl.BlockSpec((None, TT, nblk * HD), lambda b, j: (b, 0, j)),
        out_shape=jax.ShapeDtypeStruct((NB, TT, 3 * W_A), BF),
        scratch_shapes=[pltpu.VMEM((T + 16, HD), F32)],
        compiler_params=_cp(("parallel", "parallel")),
        name="gdn_conv",
    )(p, conv_w)


def _split3(x):
    a = x.astype(BF)
    r = x - a.astype(F32)
    b = r.astype(BF)
    return a, b, (r - b.astype(F32)).astype(BF)


def _gdn_kernel(qf_ref, kf_ref, vf_ref, gf_ref, qb_ref, kb_ref, vb_ref, gb_ref, prm_ref, of_ref, ob_ref, s_ref):
    nbs = qf_ref.shape[0]
    @pl.when(pl.program_id(1) == 0)
    def _():
        s_ref[...] = jnp.zeros_like(s_ref)

    ri = lax.broadcasted_iota(jnp.int32, (CS, CS), 0)
    ci = lax.broadcasted_iota(jnp.int32, (CS, CS), 1)
    eye_f = jnp.where(ri == ci, 1.0, 0.0)
    pair = (ri // 2) == (ci // 2)
    levels = [((ri // (2 * bs)) == (ci // (2 * bs))) & ((ri // bs) != (ci // bs)) for bs in (2, 4, 8, 16, 32)]
    prm = prm_ref[...]
    a_neg = -jnp.exp(prm[0:1, :])
    dt_b = prm[1:2, :]

    ch = []
    dir_refs = ((qf_ref, kf_ref, vf_ref, gf_ref, of_ref), (qb_ref, kb_ref, vb_ref, gb_ref, ob_ref))
    for bb, d in [(bb, d) for bb in range(nbs) for d in range(2)]:
        q_ref, k_ref, v_ref, g_ref, o_ref = [r.at[bb] for r in dir_refs[d]]
        dm = (ri - ci) if d == 0 else (ci - ri)
        incl = dm >= 0
        strict = dm > 0
        tri = jnp.where(incl, 1.0, 0.0).astype(BF)
        tri_t = jnp.where(dm <= 0, 1.0, 0.0).astype(BF)
        gts = g_ref[...]
        x = gts + dt_b
        glog = a_neg * (jnp.maximum(x, 0.0) + jnp.log(1.0 + jnp.exp(-jnp.abs(x))))
        beta = jax.nn.sigmoid(gts)
        parts = _split3(glog)
        gc = sum(_dot(tri, g) for g in parts)
        gc_t = sum(_dot_tn(g, tri_t) for g in parts)
        last = CS - 1 if d == 0 else 0
        g_last = gc[last:last + 1, :]
        eg = jnp.exp(gc)
        ek = jnp.exp(g_last - gc)
        g_tot = jnp.exp(g_last)
        for h in range(NH):
            c = d * NH + h
            ch.append(dict(sl=slice(h * HD, (h + 1) * HD), c=bb * 2 * NH + c, incl=incl, strict=strict, o_ref=o_ref,
                           q_ref=q_ref, k_ref=k_ref, v_ref=v_ref,
                           b_c=beta[:, 2 * NH + c:2 * NH + c + 1], gc_c=gc[:, c:c + 1], gc_r=gc_t[c:c + 1, :],
                           eg=eg[:, c:c + 1], ek=ek[:, c:c + 1], g_tot=g_tot[:, c:c + 1]))

    for x in ch:
        kb = x["k_ref"][:, x["sl"]]
        qh = x["q_ref"][:, x["sl"]].astype(F32) * (HD ** -0.5)
        x["kh"], x["qh"] = kb.astype(F32), qh
        kq = _dot_nt(jnp.concatenate([kb, qh.astype(BF)], axis=0), kb)
        x["kk"], x["qk"] = kq[:CS], kq[CS:]
    for x in ch:
        decay = jnp.exp(jnp.where(x["incl"], x["gc_c"] - x["gc_r"], -1e30))
        lower = jnp.where(x["strict"], x["b_c"] * x["kk"] * decay, 0.0)
        x["qk"] = (x["qk"] * decay).astype(BF)
        x["lower"] = lower.astype(BF)
        x["t"] = eye_f - jnp.where(pair, lower, 0.0)
    for off in levels:
        for x in ch:
            x["tb"] = x["t"].astype(BF)
            x["mt"] = _dot(jnp.where(off, x["lower"], jnp.zeros_like(x["lower"])), x["tb"]).astype(BF)
        for x in ch:
            x["t"] = x["t"] - _dot(x["tb"], x["mt"])
    for x in ch:
        vh = x["v_ref"][:, x["sl"]].astype(F32)
        rhs = jnp.concatenate([vh * x["b_c"], x["kh"] * (x["b_c"] * x["eg"])], axis=1).astype(BF)
        x["uw"] = _dot(x["t"].astype(BF), rhs)
        x["q_dec"] = (x["qh"] * x["eg"]).astype(BF)
        x["k_dec"] = (x["kh"] * x["ek"]).astype(BF)
    for x in ch:
        st = s_ref[x["c"]]
        x["st"] = st
        stb = st.astype(BF)
        ws = _dot(jnp.concatenate([x["uw"][:, HD:].astype(BF), x["q_dec"]], axis=0), stb)
        x["vn"] = (x["uw"][:, :HD] - ws[:CS]).astype(BF)
        x["qs"] = ws[CS:]
    for x in ch:
        x["o_ref"][:, x["sl"]] = x["qs"] + _dot(x["qk"], x["vn"])
        s_ref[x["c"]] = x["st"] * x["g_tot"] + _dot_tn(x["k_dec"], x["vn"])


def _gdn_scan(qkv, gates, prm, nbs=4):
    chunk_f = lambda s: (s + NCH_LAT) % NCH
    chunk_b = lambda s: NCH - 1 - s

    def specs(chunk):
        qkv_spec = lambda c: pl.BlockSpec((nbs, CS, W_A), lambda b, s: (b, chunk(s), c))
        return [qkv_spec(0), qkv_spec(1), qkv_spec(2),
                pl.BlockSpec((nbs, CS, 128), lambda b, s: (b, chunk(s), 0))]

    return pl.pallas_call(
        _gdn_kernel,
        grid=(NB // nbs, NCH),
        in_specs=specs(chunk_f) + specs(chunk_b) + [pl.BlockSpec((8, 128), lambda b, s: (0, 0))],
        out_specs=[pl.BlockSpec((nbs, CS, W_A), lambda b, s: (b, chunk_f(s), 0)),
                   pl.BlockSpec((nbs, CS, W_A), lambda b, s: (b, chunk_b(s), 0))],
        out_shape=[jax.ShapeDtypeStruct((NB, TT, W_A), F32)] * 2,
        scratch_shapes=[pltpu.VMEM((nbs * 2 * NH, HD, HD), F32)],
        compiler_params=_cp(("parallel", "arbitrary")),
        name="gdn_scan",
    )(qkv, qkv, qkv, gates, qkv, qkv, qkv, gates, prm)


def _rope_kernel(x_ref, g_ref, cos_ref, sin_ref, o_ref):
    lane = lax.broadcasted_iota(jnp.int32, (TT, HD), 1)
    first = (lane % 64) < 32
    for hd in range(NH + KVH):
        cols = slice(hd * HD, (hd + 1) * HD)
        x = x_ref[:, cols].astype(F32)
        y = x * lax.rsqrt(jnp.mean(x * x, axis=-1, keepdims=True) + EPS) * g_ref[hd]
        partner = jnp.where(first, pltpu.roll(y, 128 - 32, axis=1), pltpu.roll(y, 32, axis=1))
        o_ref[:, cols] = (y * cos_ref[...] + partner * sin_ref[...]).astype(o_ref.dtype)


def _rope_norm(pb, gains, cos_t, sin_t):
    nh = NH + KVH
    return pl.pallas_call(
        _rope_kernel,
        grid=(NB,),
        in_specs=[pl.BlockSpec((None, TT, nh * HD), lambda b: (b, 0, 0)),
                  pl.BlockSpec((nh, 1, HD), lambda b: (0, 0, 0)),
                  pl.BlockSpec((TT, HD), lambda b: (0, 0)),
                  pl.BlockSpec((TT, HD), lambda b: (0, 0))],
        out_specs=pl.BlockSpec((None, TT, nh * HD), lambda b: (b, 0, 0)),
        out_shape=jax.ShapeDtypeStruct((NB, TT, nh * HD), BF),
        compiler_params=_cp(("parallel",)),
        name="rope_norm",
    )(pb, gains, cos_t, sin_t)


def _attn_kernel(q_ref, k_ref, v_ref, o_ref, s_ref, p_ref):
    k = k_ref[...]
    v_aug = jnp.concatenate([v_ref[...], jnp.ones((TT, HD), BF)], axis=1)
    tq = q_ref.shape[0]
    rb = 16
    for g in range(GQ):
        s_ref[g] = _dot_nt(q_ref[:, g * HD:(g + 1) * HD], k)
    for g in range(GQ):
        sl = slice(g * HD, (g + 1) * HD)
        for r in range(tq // rb):
            rows = slice(r * rb, (r + 1) * rb)
            m = s_ref[g, rows, 0:HD]
            for c in range(1, TT // HD):
                m = jnp.maximum(m, s_ref[g, rows, c * HD:(c + 1) * HD])
            mx = jnp.broadcast_to(jnp.max(m, axis=-1, keepdims=True), (rb, HD))
            for c in range(TT // HD):
                cols = slice(c * HD, (c + 1) * HD)
                p_ref[g, rows, cols] = jnp.exp(s_ref[g, rows, cols] - mx).astype(BF)
        o = _dot(p_ref[g], v_aug)
        o_ref[:, sl] = (o[:, :HD] / o[:, HD:]).astype(o_ref.dtype)


def _attention(qk, pb, tq=512):
    return pl.pallas_call(
        _attn_kernel,
        grid=(NB, KVH, T // tq),
        in_specs=[pl.BlockSpec((None, tq, GQ * HD), lambda b, kv, i: (b, i, kv)),
                  pl.BlockSpec((None, TT, HD), lambda b, kv, i: (b, 0, NH + kv)),
                  pl.BlockSpec((None, TT, HD), lambda b, kv, i: (b, 0, COL_VB // HD + kv))],
        out_specs=pl.BlockSpec((None, tq, GQ * HD), lambda b, kv, i: (b, i, kv)),
        out_shape=jax.ShapeDtypeStruct((NB, T, W_A), BF),
        scratch_shapes=[pltpu.VMEM((GQ, tq, TT), F32), pltpu.VMEM((GQ, tq, TT), BF)],
        compiler_params=_cp(("parallel", "parallel", "arbitrary")),
        name="attention",
    )(qk, qk, pb)


def _mixout_kernel(of_ref, ob_ref, z_ref, a_ref, h_ref, gt_ref, gain_ref, w_ref, o_ref):
    parts = []
    for hd in range(NH):
        sl = slice(hd * HD, (hd + 1) * HD)
        o = of_ref[:, sl] + ob_ref[:, sl]
        n = o * lax.rsqrt(jnp.mean(o * o, axis=-1, keepdims=True) + EPS) * gain_ref[...]
        parts.append((n * _silu(z_ref[:, sl].astype(F32))).astype(BF))
    gd = jnp.concatenate(parts, axis=1)
    y = _dot(gd, w_ref[0:W_A, :]) + _dot(a_ref[...], w_ref[W_A:2 * W_A, :])
    o_ref[...] = h_ref[...] + _mrow(gt_ref, pl.program_id(0)) * y


def _mixout(o_f, o_b, pa, attn, h, mod, gain, w_out, tm=256):
    return pl.pallas_call(
        _mixout_kernel,
        grid=(NB, T // tm),
        in_specs=[pl.BlockSpec((None, tm, W_A), lambda b, i: (b, i, 0)),
                  pl.BlockSpec((None, tm, W_A), lambda b, i: (b, i, 0)),
                  pl.BlockSpec((None, tm, W_A), lambda b, i: (b, i, COL_Z // W_A)),
                  pl.BlockSpec((None, tm, W_A), lambda b, i: (b, i, 0)),
                  pl.BlockSpec((None, tm, D), lambda b, i: (b, i, 0)),
                  _mod_spec(5),
                  pl.BlockSpec((1, HD), lambda b, i: (0, 0)),
                  pl.BlockSpec((D, D), lambda b, i: (0, 0))],
        out_specs=pl.BlockSpec((None, tm, D), lambda b, i: (b, i, 0)),
        out_shape=jax.ShapeDtypeStruct((NB, T, D), F32),
        compiler_params=_cp(("parallel", "parallel")),
        name="mix_out",
    )(o_f, o_b, pa, attn, h, mod, gain.reshape(1, HD), w_out)


def _dft_tables():
    def cs(n):
        idx = np.arange(n, dtype=np.int64)
        ang = 2.0 * np.pi * ((idx[:, None] * idx[None, :]) % n).astype(np.float64) / n
        return np.cos(ang) / math.sqrt(n), np.sin(ang) / math.sqrt(n)

    cc, sc = cs(GC)
    ct, st = cs(T)
    ch = np.concatenate([cc, sc], axis=1).astype(np.float32)
    tm = np.concatenate([ct, -st], axis=1).astype(np.float32)
    return jnp.asarray(ch, dtype=BF), jnp.asarray(tm, dtype=BF)


def _fchan_kernel(h_ref, sh_ref, sc_ref, gain_ref, w_ref, a_ref, b_ref, u_ref, rs_ref):
    b = pl.program_id(0)
    _modnorm_tiles(h_ref, u_ref, rs_ref, gain_ref[...], _mrow(sh_ref, b), _mrow(sc_ref, b))
    w = w_ref[...]
    for g in range(FG):
        sl = slice(g * GC, (g + 1) * GC)
        ab = _dot(u_ref[:, sl], w)
        a_ref[:, sl] = ab[:, :GC].astype(a_ref.dtype)
        b_ref[:, sl] = ab[:, GC:].astype(b_ref.dtype)


def _fourier_channels(h, mod, gain, ch_tab, tm=512):
    return pl.pallas_call(
        _fchan_kernel,
        grid=(NB, T // tm),
        in_specs=[pl.BlockSpec((None, tm, D), lambda b, i: (b, i, 0)),
                  _mod_spec(3), _mod_spec(4),
                  pl.BlockSpec((1, D), lambda b, i: (0, 0)),
                  pl.BlockSpec((GC, 2 * GC), lambda b, i: (0, 0))],
        out_specs=[pl.BlockSpec((None, tm, D), lambda b, i: (b, i, 0))] * 2,
        out_shape=[jax.ShapeDtypeStruct((NB, T, D), BF)] * 2,
        scratch_shapes=[pltpu.VMEM((tm, D), BF), pltpu.VMEM((tm, 128), F32)],
        compiler_params=_cp(("parallel", "parallel")),
        name="fourier_channels",
    )(h, mod, mod, gain.reshape(1, D), ch_tab)


def _ftime_kernel(t_ref, a_ref, b_ref, o_ref):
    y = _dot(t_ref[:, :T], a_ref[...]) + _dot(t_ref[:, T:], b_ref[...])
    o_ref[...] = y.astype(o_ref.dtype)


def _fourier_time(tm_tab, a, b, tm=1024, tn=1024):
    return pl.pallas_call(
        _ftime_kernel,
        grid=(NB, D // tn, T // tm),
        in_specs=[pl.BlockSpec((tm, 2 * T), lambda b, j, i: (i, 0)),
                  pl.BlockSpec((None, T, tn), lambda b, j, i: (b, 0, j)),
                  pl.BlockSpec((None, T, tn), lambda b, j, i: (b, 0, j))],
        out_specs=pl.BlockSpec((None, tm, tn), lambda b, j, i: (b, i, j)),
        out_shape=jax.ShapeDtypeStruct((NB, T, D), BF),
        compiler_params=_cp(("parallel", "parallel", "arbitrary")),
        name="fourier_time",
    )(tm_tab, a, b)


def _resmm_kernel(y_ref, w_ref, h_ref, gt_ref, o_ref):
    o_ref[...] = h_ref[...] + _mrow(gt_ref, pl.program_id(0)) * _dot(y_ref[...], w_ref[...])


def _fourier_out(y, w_f, h, mod, tm=512):
    return pl.pallas_call(
        _resmm_kernel,
        grid=(NB, T // tm),
        in_specs=[pl.BlockSpec((None, tm, D), lambda b, i: (b, i, 0)),
                  pl.BlockSpec((D, D), lambda b, i: (0, 0)),
                  pl.BlockSpec((None, tm, D), lambda b, i: (b, i, 0)),
                  _mod_spec(5)],
        out_specs=pl.BlockSpec((None, tm, D), lambda b, i: (b, i, 0)),
        out_shape=jax.ShapeDtypeStruct((NB, T, D), F32),
        compiler_params=_cp(("parallel", "parallel")),
        name="fourier_out",
    )(y, w_f, h, mod)


def _rope_tables():
    half = HD // 4
    inv = ROPE_THETA ** (-np.arange(half, dtype=np.float64) / half)
    t = np.arange(T)
    ang_r = (t // GRID_W)[:, None] * inv[None, :]
    ang_c = (t % GRID_W)[:, None] * inv[None, :]
    cos = np.concatenate([np.cos(ang_r)] * 2 + [np.cos(ang_c)] * 2, axis=1)
    sin = np.concatenate([-np.sin(ang_r), np.sin(ang_r), -np.sin(ang_c), np.sin(ang_c)], axis=1)
    cos = np.concatenate([cos, np.ones((TC, HD))], axis=0).astype(np.float32)
    sin = np.concatenate([sin, np.zeros((TC, HD))], axis=0).astype(np.float32)
    return jnp.asarray(cos), jnp.asarray(sin)


def kernel(x, c, ctx, c_ctx, w_mod, b_mod, norm_w, ffn_w_gu, ffn_w_down, w_in, conv_w, a_log, dt_bias,
           gdn_norm, q_norm, k_norm, w_out, w_fourier, final_norm):
    cvec = jnp.concatenate([c, c_ctx[None, :], jnp.zeros((8 - NB - 1, D), F32)], axis=0)
    b_mod3 = b_mod.reshape(DEPTH, 1, N_MOD * D)
    mod0 = _modulation(cvec, w_mod, b_mod3, 0)

    h_ctx, w_g, w_u, w_dn = _ffn(ctx.reshape(1, NB * TC, D), mod0, 0, NB, norm_w[0, 0], None, None,
                                 raw=(ffn_w_gu, ffn_w_down, 0, 0))
    h_ctx = h_ctx.reshape(NB, TC, D)
    h, w_gu, w_dn, mod1 = _ffn(x, mod0, 0, 0, norm_w[0, 0], (w_g, w_u), w_dn, nxt=(ffn_w_gu, ffn_w_down, 0, 1),
                               side=(cvec, w_mod, b_mod3, 1))

    wi = w_in[0]
    w_a = wi[:, :NPA].astype(BF)
    w_b = jnp.concatenate([wi[:, NPA + 4 * NH:].astype(BF), wi[:, NPA:NPA + 4 * NH].astype(BF),
                           jnp.zeros((D, 128 - 4 * NH), BF)], axis=1)
    u = _prenorm(h, h_ctx, mod0, norm_w[0, 1]).reshape(NB * TT, D)
    pa = _matmul(u, w_a, 1024, 2048, BF, "in_proj_a").reshape(NB, TT, NPA)
    pb, gates = _matmul_split(u, w_b, 1024, NPB, "in_proj_b")
    pb = pb.reshape(NB, TT, NPB)

    qkv = _gdn_conv(pa, conv_w[0])
    prm = jnp.zeros((8, 128), F32)
    prm = prm.at[0, :2 * NH].set(a_log[0].reshape(-1)).at[1, :2 * NH].set(dt_bias[0].reshape(-1))
    o_f, o_b = _gdn_scan(qkv, gates.reshape(NB, TT, 128), prm)

    gains = jnp.concatenate([jnp.broadcast_to(q_norm[0] * (HD ** -0.5), (NH, HD)),
                             jnp.broadcast_to(k_norm[0], (KVH, HD))], axis=0).reshape(NH + KVH, 1, HD)
    cos_t, sin_t = _rope_tables()
    qk = _rope_norm(pb, gains, cos_t, sin_t)
    attn = _attention(qk, pb)

    h = _mixout(o_f, o_b, pa, attn, h, mod0, gdn_norm[0], w_out[0].astype(BF))
    h, w_gu, w_dn = _ffn(h, mod0, 6, 0, norm_w[0, 2], w_gu, w_dn, nxt=(ffn_w_gu, ffn_w_down, 1, 0))

    h, w_gu, w_dn = _ffn(h, mod1, 0, 0, norm_w[1, 0], w_gu, w_dn, nxt=(ffn_w_gu, ffn_w_down, 1, 1))
    ch_tab, tm_tab = _dft_tables()
    fa, fb = _fourier_channels(h, mod1, norm_w[1, 1], ch_tab)
    y = _fourier_time(tm_tab, fa, fb)
    h = _fourier_out(y, w_fourier[0].astype(BF), h, mod1)
    return _ffn(h, mod1, 6, 0, norm_w[1, 2], w_gu, w_dn, final_gain=final_norm)
```

```python
import functools
import math

import numpy as np
import jax
import jax.numpy as jnp
from jax import lax
from jax.experimental import pallas as pl
from jax.experimental.pallas import tpu as pltpu

D = 2048
NB = 4
T = 2048
GRID_W = 64
TC = 256
TT = T + TC
DEPTH = 2
EPS = 1e-6
N_MOD = 9
FH = 5632
NH = 8
HD = 128
KVH = 2
GQ = NH // KVH
CONV_K = 5
CS = 64
NCH = TT // CS
NCH_LAT = T // CS
ROPE_THETA = 10000.0
FG = 4
GC = D // FG
W_A = NH * HD
NPA = 4 * W_A
COL_Z = 3 * W_A
NPB = W_A + 2 * KVH * HD
COL_KB = W_A
COL_VB = COL_KB + KVH * HD

VMEM_LIMIT = 58 * 1024 * 1024
BF = jnp.bfloat16
F32 = jnp.float32


def _cp(sem):
    return pltpu.CompilerParams(dimension_semantics=sem, vmem_limit_bytes=VMEM_LIMIT)


def _dot(a, b):
    return jnp.dot(a, b, preferred_element_type=F32)


def _dot_nt(a, b):
    return lax.dot_general(a, b, (((1,), (1,)), ((), ())), preferred_element_type=F32)


def _dot_tn(a, b):
    return lax.dot_general(a, b, (((0,), (0,)), ((), ())), preferred_element_type=F32)


def _silu(x):
    return x * jax.nn.sigmoid(x)


def _mod_kernel(c_ref, w_ref, b_ref, o_ref):
    a = _silu(c_ref[...]).astype(BF)
    o_ref[...] = _dot(a, w_ref[...].astype(BF)) + b_ref[...]


def _modulation(cvec, w_mod, b_mod, layer):
    tn = 1024
    per = D // tn
    return pl.pallas_call(
        _mod_kernel,
        grid=(N_MOD * per,),
        in_specs=[pl.BlockSpec((8, D), lambda j: (0, 0)),
                  pl.BlockSpec((None, D, tn), lambda j: (layer, 0, j)),
                  pl.BlockSpec((None, 1, tn), lambda j: (layer, 0, j))],
        out_specs=pl.BlockSpec((None, 8, tn), lambda j: (j // per, 0, j % per)),
        out_shape=jax.ShapeDtypeStruct((N_MOD, 8, D), F32),
        compiler_params=_cp(("arbitrary",)),
        name="modulation",
    )(cvec, w_mod, b_mod)


def _mod_spec(k):
    return pl.BlockSpec((None, 8, D), lambda *_: (k, 0, 0))


def _mrow(ref, r):
    return ref[pl.ds(r, 1), :]


def _modnorm_tiles(h_ref, u_ref, rs_ref, gain, shift, scale, rb=32):
    n, d = h_ref.shape
    g = gain * (1.0 + scale)
    lanes = rs_ref.shape[1]
    for c in range(n // rb):
        rows = slice(c * rb, (c + 1) * rb)
        acc = None
        for k in range(d // lanes):
            x = h_ref[rows, k * lanes:(k + 1) * lanes]
            acc = x * x if acc is None else acc + x * x
        ms = jnp.sum(acc, axis=-1, keepdims=True) * (1.0 / d)
        rs_ref[rows, :] = jnp.broadcast_to(lax.rsqrt(ms + EPS), (rb, lanes))
    for c in range(n // rb):
        rows = slice(c * rb, (c + 1) * rb)
        rs = rs_ref[rows, :]
        for k in range(d // lanes):
            cols = slice(k * lanes, (k + 1) * lanes)
            u_ref[rows, cols] = (h_ref[rows, cols] * rs * g[:, cols] + shift[:, cols]).astype(BF)


def _ffn_kernel(h_ref, sh_ref, sc_ref, gt_ref, gain_ref, wg_ref, wu_ref, wd_ref, *rest,
                nj, row0, final, convert, side, raw):
    rest = list(rest)
    fin_ref = rest.pop(0) if final else None
    if convert:
        ngu_ref, ndn_ref = rest.pop(0), rest.pop(0)
    if side:
        c_ref, wm_ref, bm_ref = rest.pop(0), rest.pop(0), rest.pop(0)
    o_ref = rest.pop(0)
    if convert:
        ogu_ref, odn_ref = rest.pop(0), rest.pop(0)
    if side:
        om_ref = rest.pop(0)
    if raw:
        owg_ref, owu_ref, owd_ref = rest.pop(0), rest.pop(0), rest.pop(0)
    u_ref, rs_ref = rest
    r = pl.program_id(0) + row0
    j = pl.program_id(2)

    @pl.when(j == 0)
    def _():
        _modnorm_tiles(h_ref, u_ref, rs_ref, gain_ref[...], _mrow(sh_ref, r), _mrow(sc_ref, r))
        o_ref[...] = jnp.zeros_like(o_ref)

    wg, wu, wd = wg_ref[...], wu_ref[...], wd_ref[...]
    if raw:
        wg, wu, wd = wg.astype(BF), wu.astype(BF), wd.astype(BF)
        owg_ref[...] = wg
        owu_ref[...] = wu
        owd_ref[...] = wd
    u = u_ref[...]
    g = _dot(u, wg)
    up = _dot(u, wu)
    act = (_silu(g) * up).astype(BF)
    o_ref[...] += _dot(act, wd)
    if convert:
        ogu_ref[...] = ngu_ref[...].astype(BF)
        odn_ref[...] = ndn_ref[...].astype(BF)
    if side:
        _mod_kernel(c_ref, wm_ref, bm_ref, om_ref)

    @pl.when(j == nj - 1)
    def _():
        res = h_ref[...] + 0.5 * _mrow(gt_ref, r) * o_ref[...]
        if final:
            res = res * lax.rsqrt(jnp.mean(res * res, axis=-1, keepdims=True) + EPS) * fin_ref[...]
        o_ref[...] = res


def _ffn(h, mod, base, mod_row0, gain, w_gu, w_down, final_gain=None, nxt=None, side=None, raw=None,
         tm=1024, tf=256):
    nb, rows, _ = h.shape
    nj = FH // tf
    ni = rows // tm
    vec = pl.BlockSpec((1, D), lambda b, i, j: (0, 0))
    if raw is not None:
        assert nb * ni == 1
        r_gu, r_dn, rl, rw = raw
        w_specs = [pl.BlockSpec((None, None, D, tf), lambda b, i, j: (rl, rw, 0, j)),
                   pl.BlockSpec((None, None, D, tf), lambda b, i, j: (rl, rw, 0, j + nj)),
                   pl.BlockSpec((None, None, tf, D), lambda b, i, j: (rl, rw, j, 0))]
        w_args = [r_gu, r_gu, r_dn]
        h_spec = pl.BlockSpec((None, tm, D), lambda b, i, j: (b, i, 0), pipeline_mode=pl.Buffered(1))
    else:
        split = isinstance(w_gu, tuple)
        w_specs = [pl.BlockSpec((D, tf), lambda b, i, j: (0, j)),
                   pl.BlockSpec((D, tf), lambda b, i, j: (0, j if split else j + nj)),
                   pl.BlockSpec((tf, D), lambda b, i, j: (j, 0))]
        w_args = [w_gu[0], w_gu[1], w_down] if split else [w_gu, w_gu, w_down]
        h_spec = pl.BlockSpec((None, tm, D), lambda b, i, j: (b, i, 0))
    in_specs = [h_spec, _mod_spec(base), _mod_spec(base + 1), _mod_spec(base + 2), vec] + w_specs
    args = [h, mod, mod, mod, gain.reshape(1, D)] + w_args
    out_specs = [pl.BlockSpec((None, tm, D), lambda b, i, j: (b, i, 0))]
    out_shape = [jax.ShapeDtypeStruct(h.shape, F32)]
    if final_gain is not None:
        in_specs.append(vec)
        args.append(final_gain.reshape(1, D))
    if nxt is not None:
        n_gu, n_dn, nl, nw = nxt
        nr = nb * ni
        gu_blk = (D // nr, 2 * FH // nj)
        dn_blk = (FH // (nr * nj), D)
        gu_idx = lambda b, i, j: (b * ni + i, j)
        dn_idx = lambda b, i, j: ((b * ni + i) * nj + j, 0)
        in_specs += [pl.BlockSpec((None, None) + gu_blk, lambda b, i, j: (nl, nw) + gu_idx(b, i, j)),
                     pl.BlockSpec((None, None) + dn_blk, lambda b, i, j: (nl, nw) + dn_idx(b, i, j))]
        args += [n_gu, n_dn]
        out_specs += [pl.BlockSpec(gu_blk, gu_idx), pl.BlockSpec(dn_blk, dn_idx)]
        out_shape += [jax.ShapeDtypeStruct((D, 2 * FH), BF), jax.ShapeDtypeStruct((FH, D), BF)]
    if side is not None:
        cvec, w_mod, b_mod, sl = side
        lanes = 128
        nslab = N_MOD * D // lanes
        assert nb * ni * nj >= nslab
        slab = lambda b, i, j: jnp.minimum((b * ni + i) * nj + j, nslab - 1)
        in_specs += [pl.BlockSpec((8, D), lambda b, i, j: (0, 0)),
                     pl.BlockSpec((None, D, lanes), lambda b, i, j: (sl, 0, slab(b, i, j))),
                     pl.BlockSpec((None, 1, lanes), lambda b, i, j: (sl, 0, slab(b, i, j)))]
        args += [cvec, w_mod, b_mod]
        per = D // lanes
        out_specs.append(pl.BlockSpec((None, 8, lanes), lambda b, i, j: (slab(b, i, j) // per, 0, slab(b, i, j) % per)))
        out_shape.append(jax.ShapeDtypeStruct((N_MOD, 8, D), F32))
    if raw is not None:
        out_specs += [pl.BlockSpec((D, tf), lambda b, i, j: (0, j)), pl.BlockSpec((D, tf), lambda b, i, j: (0, j)),
                      pl.BlockSpec((tf, D), lambda b, i, j: (j, 0))]
        out_shape += [jax.ShapeDtypeStruct((D, FH), BF), jax.ShapeDtypeStruct((D, FH), BF),
                      jax.ShapeDtypeStruct((FH, D), BF)]
    out = pl.pallas_call(
        functools.partial(_ffn_kernel, nj=nj, row0=mod_row0, final=final_gain is not None, convert=nxt is not None,
                          side=side is not None, raw=raw is not None),
        grid=(nb, ni, nj),
        in_specs=in_specs,
        out_specs=out_specs,
        out_shape=out_shape,
        scratch_shapes=[pltpu.VMEM((tm, D), BF), pltpu.VMEM((tm, 128), F32)],
        compiler_params=_cp(("arbitrary",) * 3 if side is not None else ("parallel", "parallel", "arbitrary")),
        name="ffn",
    )(*args)
    return out if len(out) > 1 else out[0]


def _prenorm_kernel(hl_ref, hc_ref, sh_ref, sc_ref, gain_ref, o_ref, rs_ref):
    b = pl.program_id(0)
    i = pl.program_id(1)

    @pl.when(i < T // TC)
    def _():
        _modnorm_tiles(hl_ref, o_ref, rs_ref, gain_ref[...], _mrow(sh_ref, b), _mrow(sc_ref, b))

    @pl.when(i >= T // TC)
    def _():
        _modnorm_tiles(hc_ref, o_ref, rs_ref, gain_ref[...], _mrow(sh_ref, NB), _mrow(sc_ref, NB))


def _prenorm(h, h_ctx, mod, gain):
    nlat = T // TC
    return pl.pallas_call(
        _prenorm_kernel,
        grid=(NB, TT // TC),
        in_specs=[pl.BlockSpec((None, TC, D), lambda b, i: (b, jnp.minimum(i, nlat - 1), 0)),
                  pl.BlockSpec((None, TC, D), lambda b, i: (b, 0, 0)),
                  _mod_spec(3), _mod_spec(4),
                  pl.BlockSpec((1, D), lambda b, i: (0, 0))],
        out_specs=pl.BlockSpec((None, TC, D), lambda b, i: (b, i, 0)),
        out_shape=jax.ShapeDtypeStruct((NB, TT, D), BF),
        scratch_shapes=[pltpu.VMEM((TC, 128), F32)],
        compiler_params=_cp(("parallel", "arbitrary")),
        name="prenorm",
    )(h, h_ctx, mod, mod, gain.reshape(1, D))


def _mm_kernel(a_ref, b_ref, o_ref):
    o_ref[...] = _dot(a_ref[...], b_ref[...]).astype(o_ref.dtype)


def _mm_split_kernel(a_ref, b_ref, o1_ref, o2_ref):
    r = _dot(a_ref[...], b_ref[...])
    n1 = o1_ref.shape[1]
    o1_ref[...] = r[:, :n1].astype(o1_ref.dtype)
    o2_ref[...] = r[:, n1:]


def _matmul_split(a, w, tm, n1, name):
    m, k = a.shape
    n = w.shape[1]
    return pl.pallas_call(
        _mm_split_kernel,
        grid=(m // tm,),
        in_specs=[pl.BlockSpec((tm, k), lambda i: (i, 0)),
                  pl.BlockSpec((k, n), lambda i: (0, 0))],
        out_specs=[pl.BlockSpec((tm, n1), lambda i: (i, 0)),
                   pl.BlockSpec((tm, n - n1), lambda i: (i, 0))],
        out_shape=[jax.ShapeDtypeStruct((m, n1), BF), jax.ShapeDtypeStruct((m, n - n1), F32)],
        compiler_params=_cp(("parallel",)),
        name=name,
    )(a, w)


def _matmul(a, w, n, tm, tn, out_dtype, name):
    m, k = a.shape
    return pl.pallas_call(
        _mm_kernel,
        grid=(m // tm, n // tn),
        in_specs=[pl.BlockSpec((tm, k), lambda i, j: (i, 0)),
                  pl.BlockSpec((k, tn), lambda i, j: (0, j))],
        out_specs=pl.BlockSpec((tm, tn), lambda i, j: (i, j)),
        out_shape=jax.ShapeDtypeStruct((m, n), out_dtype),
        compiler_params=_cp(("parallel", "arbitrary")),
        name=name,
    )(a, w)


def _conv_kernel(x_ref, w_ref, o_ref, pad_ref):
    j = pl.program_id(1)
    half = CONV_K // 2
    nblk = x_ref.shape[1] // HD
    for c in range(nblk):
        cols = slice(c * HD, (c + 1) * HD)
        for r0, n in ((0, T), (T, TC)):
            pad_ref[0:8, :] = jnp.zeros((8, HD), F32)
            pad_ref[8:8 + n, :] = x_ref[r0:r0 + n, cols].astype(F32)
            pad_ref[8 + n:16 + n, :] = jnp.zeros((8, HD), F32)
            acc = jnp.zeros((n, HD), F32)
            for k in range(CONV_K):
                acc = acc + w_ref[k:k + 1, cols] * pad_ref[8 + k - half:8 + k - half + n, :]
            y = _silu(acc)
            yn = y * lax.rsqrt(jnp.sum(y * y, axis=-1, keepdims=True) + EPS)
            o_ref[r0:r0 + n, cols] = jnp.where(j * nblk + c < 2 * NH, yn, y).astype(o_ref.dtype)


def _gdn_conv(p, conv_w, nblk=3):
    return pl.pallas_call(
        _conv_kernel,
        grid=(NB, 3 * NH // nblk),
        in_specs=[pl.BlockSpec((None, TT, nblk * HD), lambda b, j: (b, 0, j)),
                  pl.BlockSpec((CONV_K, nblk * HD), lambda b, j: (0, j))],
        out_specs=pl.BlockSpec((None, TT, nblk * HD), lambda b, j: (b, 0, j)),
        out_shape=jax.ShapeDtypeStruct((NB, TT, 3 * W_A), BF),
        scratch_shapes=[pltpu.VMEM((T + 16, HD), F32)],
        compiler_params=_cp(("parallel", "parallel")),
        name="gdn_conv",
    )(p, conv_w)


def _split3(x):
    a = x.astype(BF)
    r = x - a.astype(F32)
    b = r.astype(BF)
    return a, b, (r - b.astype(F32)).astype(BF)


def _gdn_kernel(qf_ref, kf_ref, vf_ref, gf_ref, qb_ref, kb_ref, vb_ref, gb_ref, prm_ref, of_ref, ob_ref, s_ref):
    nbs = qf_ref.shape[0]
    @pl.when(pl.program_id(1) == 0)
    def _():
        s_ref[...] = jnp.zeros_like(s_ref)

    ri = lax.broadcasted_iota(jnp.int32, (CS, CS), 0)
    ci = lax.broadcasted_iota(jnp.int32, (CS, CS), 1)
    eye_f = jnp.where(ri == ci, 1.0, 0.0)
    pair = (ri // 2) == (ci // 2)
    levels = [((ri // (2 * bs)) == (ci // (2 * bs))) & ((ri // bs) != (ci // bs)) for bs in (2, 4, 8, 16, 32)]
    prm = prm_ref[...]
    a_neg = -jnp.exp(prm[0:1, :])
    dt_b = prm[1:2, :]

    ch = []
    dir_refs = ((qf_ref, kf_ref, vf_ref, gf_ref, of_ref), (qb_ref, kb_ref, vb_ref, gb_ref, ob_ref))
    for bb, d in [(bb, d) for bb in range(nbs) for d in range(2)]:
        q_ref, k_ref, v_ref, g_ref, o_ref = [r.at[bb] for r in dir_refs[d]]
        dm = (ri - ci) if d == 0 else (ci - ri)
        incl = dm >= 0
        strict = dm > 0
        tri = jnp.where(incl, 1.0, 0.0).astype(BF)
        tri_t = jnp.where(dm <= 0, 1.0, 0.0).astype(BF)
        gts = g_ref[...]
        x = gts + dt_b
        glog = a_neg * (jnp.maximum(x, 0.0) + jnp.log(1.0 + jnp.exp(-jnp.abs(x))))
        beta = jax.nn.sigmoid(gts)
        parts = _split3(glog)
        gc = sum(_dot(tri, g) for g in parts)
        gc_t = sum(_dot_tn(g, tri_t) for g in parts)
        last = CS - 1 if d == 0 else 0
        g_last = gc[last:last + 1, :]
        eg = jnp.exp(gc)
        ek = jnp.exp(g_last - gc)
        g_tot = jnp.exp(g_last)
        for h in range(NH):
            c = d * NH + h
            ch.append(dict(sl=slice(h * HD, (h + 1) * HD), c=bb * 2 * NH + c, incl=incl, strict=strict, o_ref=o_ref,
                           q_ref=q_ref, k_ref=k_ref, v_ref=v_ref,
                           b_c=beta[:, 2 * NH + c:2 * NH + c + 1], gc_c=gc[:, c:c + 1], gc_r=gc_t[c:c + 1, :],
                           eg=eg[:, c:c + 1], ek=ek[:, c:c + 1], g_tot=g_tot[:, c:c + 1]))

    for x in ch:
        kb = x["k_ref"][:, x["sl"]]
        qh = x["q_ref"][:, x["sl"]].astype(F32) * (HD ** -0.5)
        x["kh"], x["qh"] = kb.astype(F32), qh
        kq = _dot_nt(jnp.concatenate([kb, qh.astype(BF)], axis=0), kb)
        x["kk"], x["qk"] = kq[:CS], kq[CS:]
    for x in ch:
        decay = jnp.exp(jnp.where(x["incl"], x["gc_c"] - x["gc_r"], -1e30))
        lower = jnp.where(x["strict"], x["b_c"] * x["kk"] * decay, 0.0)
        x["qk"] = (x["qk"] * decay).astype(BF)
        x["lower"] = lower.astype(BF)
        x["t"] = eye_f - jnp.where(pair, lower, 0.0)
    for off in levels:
        for x in ch:
            x["tb"] = x["t"].astype(BF)
            x["mt"] = _dot(jnp.where(off, x["lower"], jnp.zeros_like(x["lower"])), x["tb"]).astype(BF)
        for x in ch:
            x["t"] = x["t"] - _dot(x["tb"], x["mt"])
    for x in ch:
        vh = x["v_ref"][:, x["sl"]].astype(F32)
        rhs = jnp.concatenate([vh * x["b_c"], x["kh"] * (x["b_c"] * x["eg"])], axis=1).astype(BF)
        x["uw"] = _dot(x["t"].astype(BF), rhs)
        x["q_dec"] = (x["qh"] * x["eg"]).astype(BF)
        x["k_dec"] = (x["kh"] * x["ek"]).astype(BF)
    for x in ch:
        st = s_ref[x["c"]]
        x["st"] = st
        stb = st.astype(BF)
        ws = _dot(jnp.concatenate([x["uw"][:, HD:].astype(BF), x["q_dec"]], axis=0), stb)
        x["vn"] = (x["uw"][:, :HD] - ws[:CS]).astype(BF)
        x["qs"] = ws[CS:]
    for x in ch:
        x["o_ref"][:, x["sl"]] = x["qs"] + _dot(x["qk"], x["vn"])
        s_ref[x["c"]] = x["st"] * x["g_tot"] + _dot_tn(x["k_dec"], x["vn"])


def _gdn_scan(qkv, gates, prm, nbs=4):
    chunk_f = lambda s: (s + NCH_LAT) % NCH
    chunk_b = lambda s: NCH - 1 - s

    def specs(chunk):
        qkv_spec = lambda c: pl.BlockSpec((nbs, CS, W_A), lambda b, s: (b, chunk(s), c))
        return [qkv_spec(0), qkv_spec(1), qkv_spec(2),
                pl.BlockSpec((nbs, CS, 128), lambda b, s: (b, chunk(s), 0))]

    return pl.pallas_call(
        _gdn_kernel,
        grid=(NB // nbs, NCH),
        in_specs=specs(chunk_f) + specs(chunk_b) + [pl.BlockSpec((8, 128), lambda b, s: (0, 0))],
        out_specs=[pl.BlockSpec((nbs, CS, W_A), lambda b, s: (b, chunk_f(s), 0)),
                   pl.BlockSpec((nbs, CS, W_A), lambda b, s: (b, chunk_b(s), 0))],
        out_shape=[jax.ShapeDtypeStruct((NB, TT, W_A), F32)] * 2,
        scratch_shapes=[pltpu.VMEM((nbs * 2 * NH, HD, HD), F32)],
        compiler_params=_cp(("parallel", "arbitrary")),
        name="gdn_scan",
    )(qkv, qkv, qkv, gates, qkv, qkv, qkv, gates, prm)


def _rope_kernel(x_ref, g_ref, cos_ref, sin_ref, o_ref):
    lane = lax.broadcasted_iota(jnp.int32, (TT, HD), 1)
    first = (lane % 64) < 32
    for hd in range(NH + KVH):
        cols = slice(hd * HD, (hd + 1) * HD)
        x = x_ref[:, cols].astype(F32)
        y = x * lax.rsqrt(jnp.mean(x * x, axis=-1, keepdims=True) + EPS) * g_ref[hd]
        partner = jnp.where(first, pltpu.roll(y, 128 - 32, axis=1), pltpu.roll(y, 32, axis=1))
        o_ref[:, cols] = (y * cos_ref[...] + partner * sin_ref[...]).astype(o_ref.dtype)


def _rope_norm(pb, gains, cos_t, sin_t):
    nh = NH + KVH
    return pl.pallas_call(
        _rope_kernel,
        grid=(NB,),
        in_specs=[pl.BlockSpec((None, TT, nh * HD), lambda b: (b, 0, 0)),
                  pl.BlockSpec((nh, 1, HD), lambda b: (0, 0, 0)),
                  pl.BlockSpec((TT, HD), lambda b: (0, 0)),
                  pl.BlockSpec((TT, HD), lambda b: (0, 0))],
        out_specs=pl.BlockSpec((None, TT, nh * HD), lambda b: (b, 0, 0)),
        out_shape=jax.ShapeDtypeStruct((NB, TT, nh * HD), BF),
        compiler_params=_cp(("parallel",)),
        name="rope_norm",
    )(pb, gains, cos_t, sin_t)


def _attn_kernel(q_ref, k_ref, v_ref, o_ref, s_ref, p_ref):
    k = k_ref[...]
    v_aug = jnp.concatenate([v_ref[...], jnp.ones((TT, HD), BF)], axis=1)
    tq = q_ref.shape[0]
    rb = 16
    for g in range(GQ):
        s_ref[g] = _dot_nt(q_ref[:, g * HD:(g + 1) * HD], k)
    for g in range(GQ):
        sl = slice(g * HD, (g + 1) * HD)
        for r in range(tq // rb):
            rows = slice(r * rb, (r + 1) * rb)
            m = s_ref[g, rows, 0:HD]
            for c in range(1, TT // HD):
                m = jnp.maximum(m, s_ref[g, rows, c * HD:(c + 1) * HD])
            mx = jnp.broadcast_to(jnp.max(m, axis=-1, keepdims=True), (rb, HD))
            for c in range(TT // HD):
                cols = slice(c * HD, (c + 1) * HD)
                p_ref[g, rows, cols] = jnp.exp(s_ref[g, rows, cols] - mx).astype(BF)
        o = _dot(p_ref[g], v_aug)
        o_ref[:, sl] = (o[:, :HD] / o[:, HD:]).astype(o_ref.dtype)


def _attention(qk, pb, tq=512):
    return pl.pallas_call(
        _attn_kernel,
        grid=(NB, KVH, T // tq),
        in_specs=[pl.BlockSpec((None, tq, GQ * HD), lambda b, kv, i: (b, i, kv)),
                  pl.BlockSpec((None, TT, HD), lambda b, kv, i: (b, 0, NH + kv)),
                  pl.BlockSpec((None, TT, HD), lambda b, kv, i: (b, 0, COL_VB // HD + kv))],
        out_specs=pl.BlockSpec((None, tq, GQ * HD), lambda b, kv, i: (b, i, kv)),
        out_shape=jax.ShapeDtypeStruct((NB, T, W_A), BF),
        scratch_shapes=[pltpu.VMEM((GQ, tq, TT), F32), pltpu.VMEM((GQ, tq, TT), BF)],
        compiler_params=_cp(("parallel", "parallel", "arbitrary")),
        name="attention",
    )(qk, qk, pb)


def _mixout_kernel(of_ref, ob_ref, z_ref, a_ref, h_ref, gt_ref, gain_ref, w_ref, o_ref):
    parts = []
    for hd in range(NH):
        sl = slice(hd * HD, (hd + 1) * HD)
        o = of_ref[:, sl] + ob_ref[:, sl]
        n = o * lax.rsqrt(jnp.mean(o * o, axis=-1, keepdims=True) + EPS) * gain_ref[...]
        parts.append((n * _silu(z_ref[:, sl].astype(F32))).astype(BF))
    gd = jnp.concatenate(parts, axis=1)
    y = _dot(gd, w_ref[0:W_A, :]) + _dot(a_ref[...], w_ref[W_A:2 * W_A, :])
    o_ref[...] = h_ref[...] + _mrow(gt_ref, pl.program_id(0)) * y


def _mixout(o_f, o_b, pa, attn, h, mod, gain, w_out, tm=256):
    return pl.pallas_call(
        _mixout_kernel,
        grid=(NB, T // tm),
        in_specs=[pl.BlockSpec((None, tm, W_A), lambda b, i: (b, i, 0)),
                  pl.BlockSpec((None, tm, W_A), lambda b, i: (b, i, 0)),
                  pl.BlockSpec((None, tm, W_A), lambda b, i: (b, i, COL_Z // W_A)),
                  pl.BlockSpec((None, tm, W_A), lambda b, i: (b, i, 0)),
                  pl.BlockSpec((None, tm, D), lambda b, i: (b, i, 0)),
                  _mod_spec(5),
                  pl.BlockSpec((1, HD), lambda b, i: (0, 0)),
                  pl.BlockSpec((D, D), lambda b, i: (0, 0))],
        out_specs=pl.BlockSpec((None, tm, D), lambda b, i: (b, i, 0)),
        out_shape=jax.ShapeDtypeStruct((NB, T, D), F32),
        compiler_params=_cp(("parallel", "parallel")),
        name="mix_out",
    )(o_f, o_b, pa, attn, h, mod, gain.reshape(1, HD), w_out)


def _dft_tables():
    def cs(n):
        idx = np.arange(n, dtype=np.int64)
        ang = 2.0 * np.pi * ((idx[:, None] * idx[None, :]) % n).astype(np.float64) / n
        return np.cos(ang) / math.sqrt(n), np.sin(ang) / math.sqrt(n)

    cc, sc = cs(GC)
    ct, st = cs(T)
    ch = np.concatenate([cc, sc], axis=1).astype(np.float32)
    tm = np.concatenate([ct, -st], axis=1).astype(np.float32)
    return jnp.asarray(ch, dtype=BF), jnp.asarray(tm, dtype=BF)


def _fchan_kernel(h_ref, sh_ref, sc_ref, gain_ref, w_ref, a_ref, b_ref, u_ref, rs_ref):
    b = pl.program_id(0)
    _modnorm_tiles(h_ref, u_ref, rs_ref, gain_ref[...], _mrow(sh_ref, b), _mrow(sc_ref, b))
    w = w_ref[...]
    for g in range(FG):
        sl = slice(g * GC, (g + 1) * GC)
        ab = _dot(u_ref[:, sl], w)
        a_ref[:, sl] = ab[:, :GC].astype(a_ref.dtype)
        b_ref[:, sl] = ab[:, GC:].astype(b_ref.dtype)


def _fourier_channels(h, mod, gain, ch_tab, tm=512):
    return pl.pallas_call(
        _fchan_kernel,
        grid=(NB, T // tm),
        in_specs=[pl.BlockSpec((None, tm, D), lambda b, i: (b, i, 0)),
                  _mod_spec(3), _mod_spec(4),
                  pl.BlockSpec((1, D), lambda b, i: (0, 0)),
                  pl.BlockSpec((GC, 2 * GC), lambda b, i: (0, 0))],
        out_specs=[pl.BlockSpec((None, tm, D), lambda b, i: (b, i, 0))] * 2,
        out_shape=[jax.ShapeDtypeStruct((NB, T, D), BF)] * 2,
        scratch_shapes=[pltpu.VMEM((tm, D), BF), pltpu.VMEM((tm, 128), F32)],
        compiler_params=_cp(("parallel", "parallel")),
        name="fourier_channels",
    )(h, mod, mod, gain.reshape(1, D), ch_tab)


def _ftime_kernel(t_ref, a_ref, b_ref, o_ref):
    y = _dot(t_ref[:, :T], a_ref[...]) + _dot(t_ref[:, T:], b_ref[...])
    o_ref[...] = y.astype(o_ref.dtype)


def _fourier_time(tm_tab, a, b, tm=1024, tn=1024):
    return pl.pallas_call(
        _ftime_kernel,
        grid=(NB, D // tn, T // tm),
        in_specs=[pl.BlockSpec((tm, 2 * T), lambda b, j, i: (i, 0)),
                  pl.BlockSpec((None, T, tn), lambda b, j, i: (b, 0, j)),
                  pl.BlockSpec((None, T, tn), lambda b, j, i: (b, 0, j))],
        out_specs=pl.BlockSpec((None, tm, tn), lambda b, j, i: (b, i, j)),
        out_shape=jax.ShapeDtypeStruct((NB, T, D), BF),
        compiler_params=_cp(("parallel", "parallel", "arbitrary")),
        name="fourier_time",
    )(tm_tab, a, b)


def _resmm_kernel(y_ref, w_ref, h_ref, gt_ref, o_ref):
    o_ref[...] = h_ref[...] + _mrow(gt_ref, pl.program_id(0)) * _dot(y_ref[...], w_ref[...])


def _fourier_out(y, w_f, h, mod, tm=512):
    return pl.pallas_call(
        _resmm_kernel,
        grid=(NB, T // tm),
        in_specs=[pl.BlockSpec((None, tm, D), lambda b, i: (b, i, 0)),
                  pl.BlockSpec((D, D), lambda b, i: (0, 0)),
                  pl.BlockSpec((None, tm, D), lambda b, i: (b, i, 0)),
                  _mod_spec(5)],
        out_specs=pl.BlockSpec((None, tm, D), lambda b, i: (b, i, 0)),
        out_shape=jax.ShapeDtypeStruct((NB, T, D), F32),
        compiler_params=_cp(("parallel", "parallel")),
        name="fourier_out",
    )(y, w_f, h, mod)


def _rope_tables():
    half = HD // 4
    inv = ROPE_THETA ** (-np.arange(half, dtype=np.float64) / half)
    t = np.arange(T)
    ang_r = (t // GRID_W)[:, None] * inv[None, :]
    ang_c = (t % GRID_W)[:, None] * inv[None, :]
    cos = np.concatenate([np.cos(ang_r)] * 2 + [np.cos(ang_c)] * 2, axis=1)
    sin = np.concatenate([-np.sin(ang_r), np.sin(ang_r), -np.sin(ang_c), np.sin(ang_c)], axis=1)
    cos = np.concatenate([cos, np.ones((TC, HD))], axis=0).astype(np.float32)
    sin = np.concatenate([sin, np.zeros((TC, HD))], axis=0).astype(np.float32)
    return jnp.asarray(cos), jnp.asarray(sin)


def kernel(x, c, ctx, c_ctx, w_mod, b_mod, norm_w, ffn_w_gu, ffn_w_down, w_in, conv_w, a_log, dt_bias,
           gdn_norm, q_norm, k_norm, w_out, w_fourier, final_norm):
    cvec = jnp.concatenate([c, c_ctx[None, :], jnp.zeros((8 - NB - 1, D), F32)], axis=0)
    b_mod3 = b_mod.reshape(DEPTH, 1, N_MOD * D)
    mod0 = _modulation(cvec, w_mod, b_mod3, 0)

    h_ctx, w_g, w_u, w_dn = _ffn(ctx.reshape(1, NB * TC, D), mod0, 0, NB, norm_w[0, 0], None, None,
                                 raw=(ffn_w_gu, ffn_w_down, 0, 0))
    h_ctx = h_ctx.reshape(NB, TC, D)
    h, w_gu, w_dn, mod1 = _ffn(x, mod0, 0, 0, norm_w[0, 0], (w_g, w_u), w_dn, nxt=(ffn_w_gu, ffn_w_down, 0, 1),
                               side=(cvec, w_mod, b_mod3, 1))

    wi = w_in[0]
    wi = wi.astype(BF)
    w_b = jnp.concatenate([wi[:, NPA + 4 * NH:], wi[:, NPA:NPA + 4 * NH], jnp.zeros((D, 128 - 4 * NH), BF)], axis=1)
    u = _prenorm(h, h_ctx, mod0, norm_w[0, 1]).reshape(NB * TT, D)
    pa = _matmul(u, wi, NPA, 1024, 2048, BF, "in_proj_a").reshape(NB, TT, NPA)
    pb, gates = _matmul_split(u, w_b, 1024, NPB, "in_proj_b")
    pb = pb.reshape(NB, TT, NPB)

    qkv = _gdn_conv(pa, conv_w[0])
    prm = jnp.zeros((8, 128), F32)
    prm = prm.at[0, :2 * NH].set(a_log[0].reshape(-1)).at[1, :2 * NH].set(dt_bias[0].reshape(-1))
    o_f, o_b = _gdn_scan(qkv, gates.reshape(NB, TT, 128), prm)

    gains = jnp.concatenate([jnp.broadcast_to(q_norm[0] * (HD ** -0.5), (NH, HD)),
                             jnp.broadcast_to(k_norm[0], (KVH, HD))], axis=0).reshape(NH + KVH, 1, HD)
    cos_t, sin_t = _rope_tables()
    qk = _rope_norm(pb, gains, cos_t, sin_t)
    attn = _attention(qk, pb)

    h = _mixout(o_f, o_b, pa, attn, h, mod0, gdn_norm[0], w_out[0].astype(BF))
    h, w_gu, w_dn = _ffn(h, mod0, 6, 0, norm_w[0, 2], w_gu, w_dn, nxt=(ffn_w_gu, ffn_w_down, 1, 0))

    h, w_gu, w_dn = _ffn(h, mod1, 0, 0, norm_w[1, 0], w_gu, w_dn, nxt=(ffn_w_gu, ffn_w_down, 1, 1))
    ch_tab, tm_tab = _dft_tables()
    fa, fb = _fourier_channels(h, mod1, norm_w[1, 1], ch_tab)
    y = _fourier_time(tm_tab, fa, fb)
    h = _fourier_out(y, w_fourier[0].astype(BF), h, mod1)
    return _ffn(h, mod1, 6, 0, norm_w[1, 2], w_gu, w_dn, final_gain=final_norm)
```

```python
import functools
import math

import numpy as np
import jax
import jax.numpy as jnp
from jax import lax
from jax.experimental import pallas as pl
from jax.experimental.pallas import tpu as pltpu

D = 2048
NB = 4
T = 2048
GRID_W = 64
TC = 256
TT = T + TC
DEPTH = 2
EPS = 1e-6
N_MOD = 9
FH = 5632
NH = 8
HD = 128
KVH = 2
GQ = NH // KVH
CONV_K = 5
CS = 64
NCH = TT // CS
NCH_LAT = T // CS
ROPE_THETA = 10000.0
FG = 4
GC = D // FG
W_A = NH * HD
NPA = 4 * W_A
COL_Z = 3 * W_A
NPB = W_A + 2 * KVH * HD
COL_KB = W_A
COL_VB = COL_KB + KVH * HD

VMEM_LIMIT = 58 * 1024 * 1024
BF = jnp.bfloat16
F32 = jnp.float32


def _cp(sem):
    return pltpu.CompilerParams(dimension_semantics=sem, vmem_limit_bytes=VMEM_LIMIT)


def _dot(a, b):
    return jnp.dot(a, b, preferred_element_type=F32)


def _dot_nt(a, b):
    return lax.dot_general(a, b, (((1,), (1,)), ((), ())), preferred_element_type=F32)


def _dot_tn(a, b):
    return lax.dot_general(a, b, (((0,), (0,)), ((), ())), preferred_element_type=F32)


def _silu(x):
    return x * jax.nn.sigmoid(x)


def _mod_kernel(c_ref, w_ref, b_ref, o_ref):
    a = _silu(c_ref[...]).astype(BF)
    o_ref[...] = _dot(a, w_ref[...].astype(BF)) + b_ref[...]


def _modulation(cvec, w_mod, b_mod, layer):
    tn = 1024
    per = D // tn
    return pl.pallas_call(
        _mod_kernel,
        grid=(N_MOD * per,),
        in_specs=[pl.BlockSpec((8, D), lambda j: (0, 0)),
                  pl.BlockSpec((None, D, tn), lambda j: (layer, 0, j)),
                  pl.BlockSpec((None, 1, tn), lambda j: (layer, 0, j))],
        out_specs=pl.BlockSpec((None, 8, tn), lambda j: (j // per, 0, j % per)),
        out_shape=jax.ShapeDtypeStruct((N_MOD, 8, D), F32),
        compiler_params=_cp(("arbitrary",)),
        name="modulation",
    )(cvec, w_mod, b_mod)


def _mod_spec(k):
    return pl.BlockSpec((None, 8, D), lambda *_: (k, 0, 0))


def _mrow(ref, r):
    return ref[pl.ds(r, 1), :]


def _modnorm_tiles(h_ref, u_ref, rs_ref, gain, shift, scale, rb=32):
    n, d = h_ref.shape
    g = gain * (1.0 + scale)
    lanes = rs_ref.shape[1]
    for c in range(n // rb):
        rows = slice(c * rb, (c + 1) * rb)
        acc = None
        for k in range(d // lanes):
            x = h_ref[rows, k * lanes:(k + 1) * lanes]
            acc = x * x if acc is None else acc + x * x
        ms = jnp.sum(acc, axis=-1, keepdims=True) * (1.0 / d)
        rs_ref[rows, :] = jnp.broadcast_to(lax.rsqrt(ms + EPS), (rb, lanes))
    for c in range(n // rb):
        rows = slice(c * rb, (c + 1) * rb)
        rs = rs_ref[rows, :]
        for k in range(d // lanes):
            cols = slice(k * lanes, (k + 1) * lanes)
            u_ref[rows, cols] = (h_ref[rows, cols] * rs * g[:, cols] + shift[:, cols]).astype(BF)


def _ffn_kernel(h_ref, sh_ref, sc_ref, gt_ref, gain_ref, wg_ref, wu_ref, wd_ref, *rest,
                nj, row0, final, convert, side, raw):
    rest = list(rest)
    fin_ref = rest.pop(0) if final else None
    if convert:
        ngu_ref, ndn_ref = rest.pop(0), rest.pop(0)
    if side:
        c_ref, wm_ref, bm_ref = rest.pop(0), rest.pop(0), rest.pop(0)
    o_ref = rest.pop(0)
    if convert:
        ogu_ref, odn_ref = rest.pop(0), rest.pop(0)
    if side:
        om_ref = rest.pop(0)
    if raw:
        owg_ref, owu_ref, owd_ref = rest.pop(0), rest.pop(0), rest.pop(0)
    u_ref, rs_ref = rest
    r = pl.program_id(0) + row0
    j = pl.program_id(2)

    @pl.when(j == 0)
    def _():
        _modnorm_tiles(h_ref, u_ref, rs_ref, gain_ref[...], _mrow(sh_ref, r), _mrow(sc_ref, r))
        o_ref[...] = jnp.zeros_like(o_ref)

    wg, wu, wd = wg_ref[...], wu_ref[...], wd_ref[...]
    if raw:
        wg, wu, wd = wg.astype(BF), wu.astype(BF), wd.astype(BF)
        owg_ref[...] = wg
        owu_ref[...] = wu
        owd_ref[...] = wd
    u = u_ref[...]
    g = _dot(u, wg)
    up = _dot(u, wu)
    act = (_silu(g) * up).astype(BF)
    o_ref[...] += _dot(act, wd)
    if convert:
        ogu_ref[...] = ngu_ref[...].astype(BF)
        odn_ref[...] = ndn_ref[...].astype(BF)
    if side:
        _mod_kernel(c_ref, wm_ref, bm_ref, om_ref)

    @pl.when(j == nj - 1)
    def _():
        res = h_ref[...] + 0.5 * _mrow(gt_ref, r) * o_ref[...]
        if final:
            res = res * lax.rsqrt(jnp.mean(res * res, axis=-1, keepdims=True) + EPS) * fin_ref[...]
        o_ref[...] = res


def _ffn(h, mod, base, mod_row0, gain, w_gu, w_down, final_gain=None, nxt=None, side=None, raw=None,
         tm=1024, tf=256):
    nb, rows, _ = h.shape
    nj = FH // tf
    ni = rows // tm
    vec = pl.BlockSpec((1, D), lambda b, i, j: (0, 0))
    if raw is not None:
        assert nb * ni == 1
        r_gu, r_dn, rl, rw = raw
        w_specs = [pl.BlockSpec((None, None, D, tf), lambda b, i, j: (rl, rw, 0, j)),
                   pl.BlockSpec((None, None, D, tf), lambda b, i, j: (rl, rw, 0, j + nj)),
                   pl.BlockSpec((None, None, tf, D), lambda b, i, j: (rl, rw, j, 0))]
        w_args = [r_gu, r_gu, r_dn]
        h_spec = pl.BlockSpec((None, tm, D), lambda b, i, j: (b, i, 0), pipeline_mode=pl.Buffered(1))
    else:
        split = isinstance(w_gu, tuple)
        w_specs = [pl.BlockSpec((D, tf), lambda b, i, j: (0, j)),
                   pl.BlockSpec((D, tf), lambda b, i, j: (0, j if split else j + nj)),
                   pl.BlockSpec((tf, D), lambda b, i, j: (j, 0))]
        w_args = [w_gu[0], w_gu[1], w_down] if split else [w_gu, w_gu, w_down]
        h_spec = pl.BlockSpec((None, tm, D), lambda b, i, j: (b, i, 0))
    in_specs = [h_spec, _mod_spec(base), _mod_spec(base + 1), _mod_spec(base + 2), vec] + w_specs
    args = [h, mod, mod, mod, gain.reshape(1, D)] + w_args
    out_specs = [pl.BlockSpec((None, tm, D), lambda b, i, j: (b, i, 0))]
    out_shape = [jax.ShapeDtypeStruct(h.shape, F32)]
    if final_gain is not None:
        in_specs.append(vec)
        args.append(final_gain.reshape(1, D))
    if nxt is not None:
        n_gu, n_dn, nl, nw = nxt
        nr = nb * ni
        gu_blk = (D // nr, 2 * FH // nj)
        dn_blk = (FH // (nr * nj), D)
        gu_idx = lambda b, i, j: (b * ni + i, j)
        dn_idx = lambda b, i, j: ((b * ni + i) * nj + j, 0)
        in_specs += [pl.BlockSpec((None, None) + gu_blk, lambda b, i, j: (nl, nw) + gu_idx(b, i, j)),
                     pl.BlockSpec((None, None) + dn_blk, lambda b, i, j: (nl, nw) + dn_idx(b, i, j))]
        args += [n_gu, n_dn]
        out_specs += [pl.BlockSpec(gu_blk, gu_idx), pl.BlockSpec(dn_blk, dn_idx)]
        out_shape += [jax.ShapeDtypeStruct((D, 2 * FH), BF), jax.ShapeDtypeStruct((FH, D), BF)]
    if side is not None:
        cvec, w_mod, b_mod, sl = side
        lanes = 128
        nslab = N_MOD * D // lanes
        assert nb * ni * nj >= nslab
        slab = lambda b, i, j: jnp.minimum((b * ni + i) * nj + j, nslab - 1)
        in_specs += [pl.BlockSpec((8, D), lambda b, i, j: (0, 0)),
                     pl.BlockSpec((None, D, lanes), lambda b, i, j: (sl, 0, slab(b, i, j))),
                     pl.BlockSpec((None, 1, lanes), lambda b, i, j: (sl, 0, slab(b, i, j)))]
        args += [cvec, w_mod, b_mod]
        per = D // lanes
        out_specs.append(pl.BlockSpec((None, 8, lanes), lambda b, i, j: (slab(b, i, j) // per, 0, slab(b, i, j) % per)))
        out_shape.append(jax.ShapeDtypeStruct((N_MOD, 8, D), F32))
    if raw is not None:
        out_specs += [pl.BlockSpec((D, tf), lambda b, i, j: (0, j)), pl.BlockSpec((D, tf), lambda b, i, j: (0, j)),
                      pl.BlockSpec((tf, D), lambda b, i, j: (j, 0))]
        out_shape += [jax.ShapeDtypeStruct((D, FH), BF), jax.ShapeDtypeStruct((D, FH), BF),
                      jax.ShapeDtypeStruct((FH, D), BF)]
    out = pl.pallas_call(
        functools.partial(_ffn_kernel, nj=nj, row0=mod_row0, final=final_gain is not None, convert=nxt is not None,
                          side=side is not None, raw=raw is not None),
        grid=(nb, ni, nj),
        in_specs=in_specs,
        out_specs=out_specs,
        out_shape=out_shape,
        scratch_shapes=[pltpu.VMEM((tm, D), BF), pltpu.VMEM((tm, 128), F32)],
        compiler_params=_cp(("arbitrary",) * 3 if side is not None else ("parallel", "parallel", "arbitrary")),
        name="ffn",
    )(*args)
    return out if len(out) > 1 else out[0]


def _prenorm_kernel(hl_ref, hc_ref, sh_ref, sc_ref, gain_ref, o_ref, rs_ref):
    b = pl.program_id(0)
    i = pl.program_id(1)

    @pl.when(i < T // TC)
    def _():
        _modnorm_tiles(hl_ref, o_ref, rs_ref, gain_ref[...], _mrow(sh_ref, b), _mrow(sc_ref, b))

    @pl.when(i >= T // TC)
    def _():
        _modnorm_tiles(hc_ref, o_ref, rs_ref, gain_ref[...], _mrow(sh_ref, NB), _mrow(sc_ref, NB))


def _prenorm(h, h_ctx, mod, gain):
    nlat = T // TC
    return pl.pallas_call(
        _prenorm_kernel,
        grid=(NB, TT // TC),
        in_specs=[pl.BlockSpec((None, TC, D), lambda b, i: (b, jnp.minimum(i, nlat - 1), 0)),
                  pl.BlockSpec((None, TC, D), lambda b, i: (b, 0, 0)),
                  _mod_spec(3), _mod_spec(4),
                  pl.BlockSpec((1, D), lambda b, i: (0, 0))],
        out_specs=pl.BlockSpec((None, TC, D), lambda b, i: (b, i, 0)),
        out_shape=jax.ShapeDtypeStruct((NB, TT, D), BF),
        scratch_shapes=[pltpu.VMEM((TC, 128), F32)],
        compiler_params=_cp(("parallel", "arbitrary")),
        name="prenorm",
    )(h, h_ctx, mod, mod, gain.reshape(1, D))


def _mm_kernel(a_ref, b_ref, o_ref):
    o_ref[...] = _dot(a_ref[...], b_ref[...]).astype(o_ref.dtype)


def _mm_split_kernel(a_ref, b_ref, o1_ref, o2_ref):
    r = _dot(a_ref[...], b_ref[...])
    n1 = o1_ref.shape[1]
    o1_ref[...] = r[:, :n1].astype(o1_ref.dtype)
    o2_ref[...] = r[:, n1:]


def _matmul_split(a, w, tm, n1, name):
    m, k = a.shape
    n = w.shape[1]
    return pl.pallas_call(
        _mm_split_kernel,
        grid=(m // tm,),
        in_specs=[pl.BlockSpec((tm, k), lambda i: (i, 0)),
                  pl.BlockSpec((k, n), lambda i: (0, 0))],
        out_specs=[pl.BlockSpec((tm, n1), lambda i: (i, 0)),
                   pl.BlockSpec((tm, n - n1), lambda i: (i, 0))],
        out_shape=[jax.ShapeDtypeStruct((m, n1), BF), jax.ShapeDtypeStruct((m, n - n1), F32)],
        compiler_params=_cp(("parallel",)),
        name=name,
    )(a, w)


def _matmul(a, w, n, tm, tn, out_dtype, name):
    m, k = a.shape
    return pl.pallas_call(
        _mm_kernel,
        grid=(m // tm, n // tn),
        in_specs=[pl.BlockSpec((tm, k), lambda i, j: (i, 0)),
                  pl.BlockSpec((k, tn), lambda i, j: (0, j))],
        out_specs=pl.BlockSpec((tm, tn), lambda i, j: (i, j)),
        out_shape=jax.ShapeDtypeStruct((m, n), out_dtype),
        compiler_params=_cp(("parallel", "arbitrary")),
        name=name,
    )(a, w)


def _conv_kernel(x_ref, w_ref, o_ref, pad_ref):
    j = pl.program_id(1)
    half = CONV_K // 2
    nblk = x_ref.shape[1] // HD
    for c in range(nblk):
        cols = slice(c * HD, (c + 1) * HD)
        for r0, n in ((0, T), (T, TC)):
            pad_ref[0:8, :] = jnp.zeros((8, HD), F32)
            pad_ref[8:8 + n, :] = x_ref[r0:r0 + n, cols].astype(F32)
            pad_ref[8 + n:16 + n, :] = jnp.zeros((8, HD), F32)
            acc = jnp.zeros((n, HD), F32)
            for k in range(CONV_K):
                acc = acc + w_ref[k:k + 1, cols] * pad_ref[8 + k - half:8 + k - half + n, :]
            y = _silu(acc)
            yn = y * lax.rsqrt(jnp.sum(y * y, axis=-1, keepdims=True) + EPS)
            o_ref[r0:r0 + n, cols] = jnp.where(j * nblk + c < 2 * NH, yn, y).astype(o_ref.dtype)


def _gdn_conv(p, conv_w, nblk=6):
    return pl.pallas_call(
        _conv_kernel,
        grid=(NB, 3 * NH // nblk),
        in_specs=[pl.BlockSpec((None, TT, nblk * HD), lambda b, j: (b, 0, j)),
                  pl.BlockSpec((CONV_K, nblk * HD), lambda b, j: (0, j))],
        out_specs=pl.BlockSpec((None, TT, nblk * HD), lambda b, j: (b, 0, j)),
        out_shape=jax.ShapeDtypeStruct((NB, TT, 3 * W_A), BF),
        scratch_shapes=[pltpu.VMEM((T + 16, HD), F32)],
        compiler_params=_cp(("parallel", "parallel")),
        name="gdn_conv",
    )(p, conv_w)


def _split3(x):
    a = x.astype(BF)
    r = x - a.astype(F32)
    b = r.astype(BF)
    return a, b, (r - b.astype(F32)).astype(BF)


def _gdn_kernel(qf_ref, kf_ref, vf_ref, gf_ref, qb_ref, kb_ref, vb_ref, gb_ref, prm_ref, of_ref, ob_ref, s_ref):
    nbs = qf_ref.shape[0]
    @pl.when(pl.program_id(1) == 0)
    def _():
        s_ref[...] = jnp.zeros_like(s_ref)

    ri = lax.broadcasted_iota(jnp.int32, (CS, CS), 0)
    ci = lax.broadcasted_iota(jnp.int32, (CS, CS), 1)
    eye_f = jnp.where(ri == ci, 1.0, 0.0)
    pair = (ri // 2) == (ci // 2)
    levels = [((ri // (2 * bs)) == (ci // (2 * bs))) & ((ri // bs) != (ci // bs)) for bs in (2, 4, 8, 16, 32)]
    prm = prm_ref[...]
    a_neg = -jnp.exp(prm[0:1, :])
    dt_b = prm[1:2, :]

    ch = []
    dir_refs = ((qf_ref, kf_ref, vf_ref, gf_ref, of_ref), (qb_ref, kb_ref, vb_ref, gb_ref, ob_ref))
    for bb, d in [(bb, d) for bb in range(nbs) for d in range(2)]:
        q_ref, k_ref, v_ref, g_ref, o_ref = [r.at[bb] for r in dir_refs[d]]
        dm = (ri - ci) if d == 0 else (ci - ri)
        incl = dm >= 0
        strict = dm > 0
        tri = jnp.where(incl, 1.0, 0.0).astype(BF)
        tri_t = jnp.where(dm <= 0, 1.0, 0.0).astype(BF)
        gts = g_ref[...]
        x = gts + dt_b
        glog = a_neg * (jnp.maximum(x, 0.0) + jnp.log(1.0 + jnp.exp(-jnp.abs(x))))
        beta = jax.nn.sigmoid(gts)
        parts = _split3(glog)
        gc = sum(_dot(tri, g) for g in parts)
        gc_t = sum(_dot_tn(g, tri_t) for g in parts)
        last = CS - 1 if d == 0 else 0
        g_last = gc[last:last + 1, :]
        eg = jnp.exp(gc)
        ek = jnp.exp(g_last - gc)
        g_tot = jnp.exp(g_last)
        for h in range(NH):
            c = d * NH + h
            ch.append(dict(sl=slice(h * HD, (h + 1) * HD), c=bb * 2 * NH + c, incl=incl, strict=strict, o_ref=o_ref,
                           q_ref=q_ref, k_ref=k_ref, v_ref=v_ref,
                           b_c=beta[:, 2 * NH + c:2 * NH + c + 1], gc_c=gc[:, c:c + 1], gc_r=gc_t[c:c + 1, :],
                           eg=eg[:, c:c + 1], ek=ek[:, c:c + 1], g_tot=g_tot[:, c:c + 1]))

    for x in ch:
        kb = x["k_ref"][:, x["sl"]]
        qh = x["q_ref"][:, x["sl"]].astype(F32) * (HD ** -0.5)
        x["kh"], x["qh"] = kb.astype(F32), qh
        kq = _dot_nt(jnp.concatenate([kb, qh.astype(BF)], axis=0), kb)
        x["kk"], x["qk"] = kq[:CS], kq[CS:]
    for x in ch:
        decay = jnp.exp(jnp.where(x["incl"], x["gc_c"] - x["gc_r"], -1e30))
        lower = jnp.where(x["strict"], x["b_c"] * x["kk"] * decay, 0.0)
        x["qk"] = (x["qk"] * decay).astype(BF)
        x["lower"] = lower.astype(BF)
        x["t"] = eye_f - jnp.where(pair, lower, 0.0)
    for off in levels:
        for x in ch:
            x["tb"] = x["t"].astype(BF)
            x["mt"] = _dot(jnp.where(off, x["lower"], jnp.zeros_like(x["lower"])), x["tb"]).astype(BF)
        for x in ch:
            x["t"] = x["t"] - _dot(x["tb"], x["mt"])
    for x in ch:
        vh = x["v_ref"][:, x["sl"]].astype(F32)
        rhs = jnp.concatenate([vh * x["b_c"], x["kh"] * (x["b_c"] * x["eg"])], axis=1).astype(BF)
        x["uw"] = _dot(x["t"].astype(BF), rhs)
        x["q_dec"] = (x["qh"] * x["eg"]).astype(BF)
        x["k_dec"] = (x["kh"] * x["ek"]).astype(BF)
    for x in ch:
        st = s_ref[x["c"]]
        x["st"] = st
        stb = st.astype(BF)
        ws = _dot(jnp.concatenate([x["uw"][:, HD:].astype(BF), x["q_dec"]], axis=0), stb)
        x["vn"] = (x["uw"][:, :HD] - ws[:CS]).astype(BF)
        x["qs"] = ws[CS:]
    for x in ch:
        x["o_ref"][:, x["sl"]] = x["qs"] + _dot(x["qk"], x["vn"])
        s_ref[x["c"]] = x["st"] * x["g_tot"] + _dot_tn(x["k_dec"], x["vn"])


def _gdn_scan(qkv, gates, prm, nbs=4):
    chunk_f = lambda s: (s + NCH_LAT) % NCH
    chunk_b = lambda s: NCH - 1 - s

    def specs(chunk):
        qkv_spec = lambda c: pl.BlockSpec((nbs, CS, W_A), lambda b, s: (b, chunk(s), c))
        return [qkv_spec(0), qkv_spec(1), qkv_spec(2),
                pl.BlockSpec((nbs, CS, 128), lambda b, s: (b, chunk(s), 0))]

    return pl.pallas_call(
        _gdn_kernel,
        grid=(NB // nbs, NCH),
        in_specs=specs(chunk_f) + specs(chunk_b) + [pl.BlockSpec((8, 128), lambda b, s: (0, 0))],
        out_specs=[pl.BlockSpec((nbs, CS, W_A), lambda b, s: (b, chunk_f(s), 0)),
                   pl.BlockSpec((nbs, CS, W_A), lambda b, s: (b, chunk_b(s), 0))],
        out_shape=[jax.ShapeDtypeStruct((NB, TT, W_A), F32)] * 2,
        scratch_shapes=[pltpu.VMEM((nbs * 2 * NH, HD, HD), F32)],
        compiler_params=_cp(("parallel", "arbitrary")),
        name="gdn_scan",
    )(qkv, qkv, qkv, gates, qkv, qkv, qkv, gates, prm)


def _rope_kernel(x_ref, g_ref, cos_ref, sin_ref, o_ref):
    lane = lax.broadcasted_iota(jnp.int32, (TT, HD), 1)
    first = (lane % 64) < 32
    for hd in range(NH + KVH):
        cols = slice(hd * HD, (hd + 1) * HD)
        x = x_ref[:, cols].astype(F32)
        y = x * lax.rsqrt(jnp.mean(x * x, axis=-1, keepdims=True) + EPS) * g_ref[hd]
        partner = jnp.where(first, pltpu.roll(y, 128 - 32, axis=1), pltpu.roll(y, 32, axis=1))
        o_ref[:, cols] = (y * cos_ref[...] + partner * sin_ref[...]).astype(o_ref.dtype)


def _rope_norm(pb, gains, cos_t, sin_t):
    nh = NH + KVH
    return pl.pallas_call(
        _rope_kernel,
        grid=(NB,),
        in_specs=[pl.BlockSpec((None, TT, nh * HD), lambda b: (b, 0, 0)),
                  pl.BlockSpec((nh, 1, HD), lambda b: (0, 0, 0)),
                  pl.BlockSpec((TT, HD), lambda b: (0, 0)),
                  pl.BlockSpec((TT, HD), lambda b: (0, 0))],
        out_specs=pl.BlockSpec((None, TT, nh * HD), lambda b: (b, 0, 0)),
        out_shape=jax.ShapeDtypeStruct((NB, TT, nh * HD), BF),
        compiler_params=_cp(("parallel",)),
        name="rope_norm",
    )(pb, gains, cos_t, sin_t)


def _attn_kernel(q_ref, k_ref, v_ref, o_ref, s_ref, p_ref):
    k = k_ref[...]
    v_aug = jnp.concatenate([v_ref[...], jnp.ones((TT, HD), BF)], axis=1)
    tq = q_ref.shape[0]
    rb = 16
    for g in range(GQ):
        s_ref[g] = _dot_nt(q_ref[:, g * HD:(g + 1) * HD], k)
    for g in range(GQ):
        sl = slice(g * HD, (g + 1) * HD)
        for r in range(tq // rb):
            rows = slice(r * rb, (r + 1) * rb)
            m = s_ref[g, rows, 0:HD]
            for c in range(1, TT // HD):
                m = jnp.maximum(m, s_ref[g, rows, c * HD:(c + 1) * HD])
            mx = jnp.broadcast_to(jnp.max(m, axis=-1, keepdims=True), (rb, HD))
            for c in range(TT // HD):
                cols = slice(c * HD, (c + 1) * HD)
                p_ref[g, rows, cols] = jnp.exp(s_ref[g, rows, cols] - mx).astype(BF)
        o = _dot(p_ref[g], v_aug)
        o_ref[:, sl] = (o[:, :HD] / o[:, HD:]).astype(o_ref.dtype)


def _attention(qk, pb, tq=512):
    return pl.pallas_call(
        _attn_kernel,
        grid=(NB, KVH, T // tq),
        in_specs=[pl.BlockSpec((None, tq, GQ * HD), lambda b, kv, i: (b, i, kv)),
                  pl.BlockSpec((None, TT, HD), lambda b, kv, i: (b, 0, NH + kv)),
                  pl.BlockSpec((None, TT, HD), lambda b, kv, i: (b, 0, COL_VB // HD + kv))],
        out_specs=pl.BlockSpec((None, tq, GQ * HD), lambda b, kv, i: (b, i, kv)),
        out_shape=jax.ShapeDtypeStruct((NB, T, W_A), BF),
        scratch_shapes=[pltpu.VMEM((GQ, tq, TT), F32), pltpu.VMEM((GQ, tq, TT), BF)],
        compiler_params=_cp(("parallel", "parallel", "arbitrary")),
        name="attention",
    )(qk, qk, pb)


def _mixout_kernel(of_ref, ob_ref, z_ref, a_ref, h_ref, gt_ref, gain_ref, w_ref, o_ref):
    parts = []
    for hd in range(NH):
        sl = slice(hd * HD, (hd + 1) * HD)
        o = of_ref[:, sl] + ob_ref[:, sl]
        n = o * lax.rsqrt(jnp.mean(o * o, axis=-1, keepdims=True) + EPS) * gain_ref[...]
        parts.append((n * _silu(z_ref[:, sl].astype(F32))).astype(BF))
    gd = jnp.concatenate(parts, axis=1)
    y = _dot(gd, w_ref[0:W_A, :]) + _dot(a_ref[...], w_ref[W_A:2 * W_A, :])
    o_ref[...] = h_ref[...] + _mrow(gt_ref, pl.program_id(0)) * y


def _mixout(o_f, o_b, pa, attn, h, mod, gain, w_out, tm=256):
    return pl.pallas_call(
        _mixout_kernel,
        grid=(NB, T // tm),
        in_specs=[pl.BlockSpec((None, tm, W_A), lambda b, i: (b, i, 0)),
                  pl.BlockSpec((None, tm, W_A), lambda b, i: (b, i, 0)),
                  pl.BlockSpec((None, tm, W_A), lambda b, i: (b, i, COL_Z // W_A)),
                  pl.BlockSpec((None, tm, W_A), lambda b, i: (b, i, 0)),
                  pl.BlockSpec((None, tm, D), lambda b, i: (b, i, 0)),
                  _mod_spec(5),
                  pl.BlockSpec((1, HD), lambda b, i: (0, 0)),
                  pl.BlockSpec((D, D), lambda b, i: (0, 0))],
        out_specs=pl.BlockSpec((None, tm, D), lambda b, i: (b, i, 0)),
        out_shape=jax.ShapeDtypeStruct((NB, T, D), F32),
        compiler_params=_cp(("parallel", "parallel")),
        name="mix_out",
    )(o_f, o_b, pa, attn, h, mod, gain.reshape(1, HD), w_out)


def _dft_tables():
    def cs(n):
        idx = np.arange(n, dtype=np.int64)
        ang = 2.0 * np.pi * ((idx[:, None] * idx[None, :]) % n).astype(np.float64) / n
        return np.cos(ang) / math.sqrt(n), np.sin(ang) / math.sqrt(n)

    cc, sc = cs(GC)
    ct, st = cs(T)
    ch = np.concatenate([cc, sc], axis=1).astype(np.float32)
    tm = np.concatenate([ct, -st], axis=1).astype(np.float32)
    return jnp.asarray(ch, dtype=BF), jnp.asarray(tm, dtype=BF)


def _fchan_kernel(h_ref, sh_ref, sc_ref, gain_ref, w_ref, a_ref, b_ref, u_ref, rs_ref):
    b = pl.program_id(0)
    _modnorm_tiles(h_ref, u_ref, rs_ref, gain_ref[...], _mrow(sh_ref, b), _mrow(sc_ref, b))
    w = w_ref[...]
    for g in range(FG):
        sl = slice(g * GC, (g + 1) * GC)
        ab = _dot(u_ref[:, sl], w)
        a_ref[:, sl] = ab[:, :GC].astype(a_ref.dtype)
        b_ref[:, sl] = ab[:, GC:].astype(b_ref.dtype)


def _fourier_channels(h, mod, gain, ch_tab, tm=1024):
    return pl.pallas_call(
        _fchan_kernel,
        grid=(NB, T // tm),
        in_specs=[pl.BlockSpec((None, tm, D), lambda b, i: (b, i, 0)),
                  _mod_spec(3), _mod_spec(4),
                  pl.BlockSpec((1, D), lambda b, i: (0, 0)),
                  pl.BlockSpec((GC, 2 * GC), lambda b, i: (0, 0))],
        out_specs=[pl.BlockSpec((None, tm, D), lambda b, i: (b, i, 0))] * 2,
        out_shape=[jax.ShapeDtypeStruct((NB, T, D), BF)] * 2,
        scratch_shapes=[pltpu.VMEM((tm, D), BF), pltpu.VMEM((tm, 128), F32)],
        compiler_params=_cp(("parallel", "parallel")),
        name="fourier_channels",
    )(h, mod, mod, gain.reshape(1, D), ch_tab)


def _ftime_kernel(t_ref, a_ref, b_ref, o_ref):
    y = _dot(t_ref[:, :T], a_ref[...]) + _dot(t_ref[:, T:], b_ref[...])
    o_ref[...] = y.astype(o_ref.dtype)


def _fourier_time(tm_tab, a, b, tm=1024, tn=1024):
    return pl.pallas_call(
        _ftime_kernel,
        grid=(NB, D // tn, T // tm),
        in_specs=[pl.BlockSpec((tm, 2 * T), lambda b, j, i: (i, 0)),
                  pl.BlockSpec((None, T, tn), lambda b, j, i: (b, 0, j)),
                  pl.BlockSpec((None, T, tn), lambda b, j, i: (b, 0, j))],
        out_specs=pl.BlockSpec((None, tm, tn), lambda b, j, i: (b, i, j)),
        out_shape=jax.ShapeDtypeStruct((NB, T, D), BF),
        compiler_params=_cp(("parallel", "parallel", "arbitrary")),
        name="fourier_time",
    )(tm_tab, a, b)


def _resmm_kernel(y_ref, w_ref, h_ref, gt_ref, o_ref):
    o_ref[...] = h_ref[...] + _mrow(gt_ref, pl.program_id(0)) * _dot(y_ref[...], w_ref[...])


def _fourier_out(y, w_f, h, mod, tm=512):
    return pl.pallas_call(
        _resmm_kernel,
        grid=(NB, T // tm),
        in_specs=[pl.BlockSpec((None, tm, D), lambda b, i: (b, i, 0)),
                  pl.BlockSpec((D, D), lambda b, i: (0, 0)),
                  pl.BlockSpec((None, tm, D), lambda b, i: (b, i, 0)),
                  _mod_spec(5)],
        out_specs=pl.BlockSpec((None, tm, D), lambda b, i: (b, i, 0)),
        out_shape=jax.ShapeDtypeStruct((NB, T, D), F32),
        compiler_params=_cp(("parallel", "parallel")),
        name="fourier_out",
    )(y, w_f, h, mod)


def _rope_tables():
    half = HD // 4
    inv = ROPE_THETA ** (-np.arange(half, dtype=np.float64) / half)
    t = np.arange(T)
    ang_r = (t // GRID_W)[:, None] * inv[None, :]
    ang_c = (t % GRID_W)[:, None] * inv[None, :]
    cos = np.concatenate([np.cos(ang_r)] * 2 + [np.cos(ang_c)] * 2, axis=1)
    sin = np.concatenate([-np.sin(ang_r), np.sin(ang_r), -np.sin(ang_c), np.sin(ang_c)], axis=1)
    cos = np.concatenate([cos, np.ones((TC, HD))], axis=0).astype(np.float32)
    sin = np.concatenate([sin, np.zeros((TC, HD))], axis=0).astype(np.float32)
    return jnp.asarray(cos), jnp.asarray(sin)


def kernel(x, c, ctx, c_ctx, w_mod, b_mod, norm_w, ffn_w_gu, ffn_w_down, w_in, conv_w, a_log, dt_bias,
           gdn_norm, q_norm, k_norm, w_out, w_fourier, final_norm):
    cvec = jnp.concatenate([c, c_ctx[None, :], jnp.zeros((8 - NB - 1, D), F32)], axis=0)
    b_mod3 = b_mod.reshape(DEPTH, 1, N_MOD * D)
    mod0 = _modulation(cvec, w_mod, b_mod3, 0)

    h_ctx, w_g, w_u, w_dn = _ffn(ctx.reshape(1, NB * TC, D), mod0, 0, NB, norm_w[0, 0], None, None,
                                 raw=(ffn_w_gu, ffn_w_down, 0, 0))
    h_ctx = h_ctx.reshape(NB, TC, D)
    h, w_gu, w_dn, mod1 = _ffn(x, mod0, 0, 0, norm_w[0, 0], (w_g, w_u), w_dn, nxt=(ffn_w_gu, ffn_w_down, 0, 1),
                               side=(cvec, w_mod, b_mod3, 1))

    wi = w_in[0]
    wi = wi.astype(BF)
    w_b = jnp.concatenate([wi[:, NPA + 4 * NH:], wi[:, NPA:NPA + 4 * NH], jnp.zeros((D, 128 - 4 * NH), BF)], axis=1)
    u = _prenorm(h, h_ctx, mod0, norm_w[0, 1]).reshape(NB * TT, D)
    pa = _matmul(u, wi, NPA, 1024, 2048, BF, "in_proj_a").reshape(NB, TT, NPA)
    pb, gates = _matmul_split(u, w_b, 1024, NPB, "in_proj_b")
    pb = pb.reshape(NB, TT, NPB)

    qkv = _gdn_conv(pa, conv_w[0])
    prm = jnp.zeros((8, 128), F32)
    prm = prm.at[0, :2 * NH].set(a_log[0].reshape(-1)).at[1, :2 * NH].set(dt_bias[0].reshape(-1))
    o_f, o_b = _gdn_scan(qkv, gates.reshape(NB, TT, 128), prm)

    gains = jnp.concatenate([jnp.broadcast_to(q_norm[0] * (HD ** -0.5), (NH, HD)),
                             jnp.broadcast_to(k_norm[0], (KVH, HD))], axis=0).reshape(NH + KVH, 1, HD)
    cos_t, sin_t = _rope_tables()
    qk = _rope_norm(pb, gains, cos_t, sin_t)
    attn = _attention(qk, pb)

    h = _mixout(o_f, o_b, pa, attn, h, mod0, gdn_norm[0], w_out[0].astype(BF))
    h, w_gu, w_dn = _ffn(h, mod0, 6, 0, norm_w[0, 2], w_gu, w_dn, nxt=(ffn_w_gu, ffn_w_down, 1, 0))

    h, w_gu, w_dn = _ffn(h, mod1, 0, 0, norm_w[1, 0], w_gu, w_dn, nxt=(ffn_w_gu, ffn_w_down, 1, 1))
    ch_tab, tm_tab = _dft_tables()
    fa, fb = _fourier_channels(h, mod1, norm_w[1, 1], ch_tab)
    y = _fourier_time(tm_tab, fa, fb)
    h = _fourier_out(y, w_fourier[0].astype(BF), h, mod1)
    return _ffn(h, mod1, 6, 0, norm_w[1, 2], w_gu, w_dn, final_gain=final_norm)
```

```python
import functools
import math

import numpy as np
import jax
import jax.numpy as jnp
from jax import lax
from jax.experimental import pallas as pl
from jax.experimental.pallas import tpu as pltpu

D = 2048
NB = 4
T = 2048
GRID_W = 64
TC = 256
TT = T + TC
DEPTH = 2
EPS = 1e-6
N_MOD = 9
FH = 5632
NH = 8
HD = 128
KVH = 2
GQ = NH // KVH
CONV_K = 5
CS = 64
NCH = TT // CS
NCH_LAT = T // CS
ROPE_THETA = 10000.0
FG = 4
GC = D // FG
W_A = NH * HD
NPA = 4 * W_A
COL_Z = 3 * W_A
NPB = W_A + 2 * KVH * HD
COL_KB = W_A
COL_VB = COL_KB + KVH * HD

VMEM_LIMIT = 58 * 1024 * 1024
BF = jnp.bfloat16
F32 = jnp.float32


def _cp(sem):
    return pltpu.CompilerParams(dimension_semantics=sem, vmem_limit_bytes=VMEM_LIMIT)


def _dot(a, b):
    return jnp.dot(a, b, preferred_element_type=F32)


def _dot_nt(a, b):
    return lax.dot_general(a, b, (((1,), (1,)), ((), ())), preferred_element_type=F32)


def _dot_tn(a, b):
    return lax.dot_general(a, b, (((0,), (0,)), ((), ())), preferred_element_type=F32)


def _silu(x):
    return x * jax.nn.sigmoid(x)


def _mod_kernel(c_ref, w_ref, b_ref, o_ref):
    a = _silu(c_ref[...]).astype(BF)
    o_ref[...] = _dot(a, w_ref[...].astype(BF)) + b_ref[...]


def _modulation(cvec, w_mod, b_mod, layer):
    tn = 1024
    per = D // tn
    return pl.pallas_call(
        _mod_kernel,
        grid=(N_MOD * per,),
        in_specs=[pl.BlockSpec((8, D), lambda j: (0, 0)),
                  pl.BlockSpec((None, D, tn), lambda j: (layer, 0, j)),
                  pl.BlockSpec((None, 1, tn), lambda j: (layer, 0, j))],
        out_specs=pl.BlockSpec((None, 8, tn), lambda j: (j // per, 0, j % per)),
        out_shape=jax.ShapeDtypeStruct((N_MOD, 8, D), F32),
        compiler_params=_cp(("arbitrary",)),
        name="modulation",
    )(cvec, w_mod, b_mod)


def _mod_spec(k):
    return pl.BlockSpec((None, 8, D), lambda *_: (k, 0, 0))


def _mrow(ref, r):
    return ref[pl.ds(r, 1), :]


def _modnorm_tiles(h_ref, u_ref, rs_ref, gain, shift, scale, rb=32):
    n, d = h_ref.shape
    g = gain * (1.0 + scale)
    lanes = rs_ref.shape[1]
    for c in range(n // rb):
        rows = slice(c * rb, (c + 1) * rb)
        acc = None
        for k in range(d // lanes):
            x = h_ref[rows, k * lanes:(k + 1) * lanes]
            acc = x * x if acc is None else acc + x * x
        ms = jnp.sum(acc, axis=-1, keepdims=True) * (1.0 / d)
        rs_ref[rows, :] = jnp.broadcast_to(lax.rsqrt(ms + EPS), (rb, lanes))
    for c in range(n // rb):
        rows = slice(c * rb, (c + 1) * rb)
        rs = rs_ref[rows, :]
        for k in range(d // lanes):
            cols = slice(k * lanes, (k + 1) * lanes)
            u_ref[rows, cols] = (h_ref[rows, cols] * rs * g[:, cols] + shift[:, cols]).astype(BF)


def _ffn_kernel(h_ref, sh_ref, sc_ref, gt_ref, gain_ref, wg_ref, wu_ref, wd_ref, *rest,
                nj, row0, final, convert, side, raw):
    rest = list(rest)
    fin_ref = rest.pop(0) if final else None
    if convert:
        ngu_ref, ndn_ref = rest.pop(0), rest.pop(0)
    if side:
        c_ref, wm_ref, bm_ref = rest.pop(0), rest.pop(0), rest.pop(0)
    o_ref = rest.pop(0)
    if convert:
        ogu_ref, odn_ref = rest.pop(0), rest.pop(0)
    if side:
        om_ref = rest.pop(0)
    if raw:
        owg_ref, owu_ref, owd_ref = rest.pop(0), rest.pop(0), rest.pop(0)
    u_ref, rs_ref = rest
    r = pl.program_id(0) + row0
    j = pl.program_id(2)

    @pl.when(j == 0)
    def _():
        _modnorm_tiles(h_ref, u_ref, rs_ref, gain_ref[...], _mrow(sh_ref, r), _mrow(sc_ref, r))
        o_ref[...] = jnp.zeros_like(o_ref)

    wg, wu, wd = wg_ref[...], wu_ref[...], wd_ref[...]
    if raw:
        wg, wu, wd = wg.astype(BF), wu.astype(BF), wd.astype(BF)
        owg_ref[...] = wg
        owu_ref[...] = wu
        owd_ref[...] = wd
    u = u_ref[...]
    g = _dot(u, wg)
    up = _dot(u, wu)
    act = (_silu(g) * up).astype(BF)
    o_ref[...] += _dot(act, wd)
    if convert:
        ogu_ref[...] = ngu_ref[...].astype(BF)
        odn_ref[...] = ndn_ref[...].astype(BF)
    if side:
        _mod_kernel(c_ref, wm_ref, bm_ref, om_ref)

    @pl.when(j == nj - 1)
    def _():
        res = h_ref[...] + 0.5 * _mrow(gt_ref, r) * o_ref[...]
        if final:
            res = res * lax.rsqrt(jnp.mean(res * res, axis=-1, keepdims=True) + EPS) * fin_ref[...]
        o_ref[...] = res


def _ffn(h, mod, base, mod_row0, gain, w_gu, w_down, final_gain=None, nxt=None, side=None, raw=None,
         tm=1024, tf=256):
    nb, rows, _ = h.shape
    nj = FH // tf
    ni = rows // tm
    vec = pl.BlockSpec((1, D), lambda b, i, j: (0, 0))
    if raw is not None:
        assert nb * ni == 1
        r_gu, r_dn, rl, rw = raw
        w_specs = [pl.BlockSpec((None, None, D, tf), lambda b, i, j: (rl, rw, 0, j)),
                   pl.BlockSpec((None, None, D, tf), lambda b, i, j: (rl, rw, 0, j + nj)),
                   pl.BlockSpec((None, None, tf, D), lambda b, i, j: (rl, rw, j, 0))]
        w_args = [r_gu, r_gu, r_dn]
        h_spec = pl.BlockSpec((None, tm, D), lambda b, i, j: (b, i, 0), pipeline_mode=pl.Buffered(1))
    else:
        split = isinstance(w_gu, tuple)
        w_specs = [pl.BlockSpec((D, tf), lambda b, i, j: (0, j)),
                   pl.BlockSpec((D, tf), lambda b, i, j: (0, j if split else j + nj)),
                   pl.BlockSpec((tf, D), lambda b, i, j: (j, 0))]
        w_args = [w_gu[0], w_gu[1], w_down] if split else [w_gu, w_gu, w_down]
        h_spec = pl.BlockSpec((None, tm, D), lambda b, i, j: (b, i, 0))
    in_specs = [h_spec, _mod_spec(base), _mod_spec(base + 1), _mod_spec(base + 2), vec] + w_specs
    args = [h, mod, mod, mod, gain.reshape(1, D)] + w_args
    out_specs = [pl.BlockSpec((None, tm, D), lambda b, i, j: (b, i, 0))]
    out_shape = [jax.ShapeDtypeStruct(h.shape, F32)]
    if final_gain is not None:
        in_specs.append(vec)
        args.append(final_gain.reshape(1, D))
    if nxt is not None:
        n_gu, n_dn, nl, nw = nxt
        nr = nb * ni
        gu_blk = (D // nr, 2 * FH // nj)
        dn_blk = (FH // (nr * nj), D)
        gu_idx = lambda b, i, j: (b * ni + i, j)
        dn_idx = lambda b, i, j: ((b * ni + i) * nj + j, 0)
        in_specs += [pl.BlockSpec((None, None) + gu_blk, lambda b, i, j: (nl, nw) + gu_idx(b, i, j)),
                     pl.BlockSpec((None, None) + dn_blk, lambda b, i, j: (nl, nw) + dn_idx(b, i, j))]
        args += [n_gu, n_dn]
        out_specs += [pl.BlockSpec(gu_blk, gu_idx), pl.BlockSpec(dn_blk, dn_idx)]
        out_shape += [jax.ShapeDtypeStruct((D, 2 * FH), BF), jax.ShapeDtypeStruct((FH, D), BF)]
    if side is not None:
        cvec, w_mod, b_mod, sl = side
        lanes = 128
        nslab = N_MOD * D // lanes
        assert nb * ni * nj >= nslab
        slab = lambda b, i, j: jnp.minimum((b * ni + i) * nj + j, nslab - 1)
        in_specs += [pl.BlockSpec((8, D), lambda b, i, j: (0, 0)),
                     pl.BlockSpec((None, D, lanes), lambda b, i, j: (sl, 0, slab(b, i, j))),
                     pl.BlockSpec((None, 1, lanes), lambda b, i, j: (sl, 0, slab(b, i, j)))]
        args += [cvec, w_mod, b_mod]
        per = D // lanes
        out_specs.append(pl.BlockSpec((None, 8, lanes), lambda b, i, j: (slab(b, i, j) // per, 0, slab(b, i, j) % per)))
        out_shape.append(jax.ShapeDtypeStruct((N_MOD, 8, D), F32))
    if raw is not None:
        out_specs += [pl.BlockSpec((D, tf), lambda b, i, j: (0, j)), pl.BlockSpec((D, tf), lambda b, i, j: (0, j)),
                      pl.BlockSpec((tf, D), lambda b, i, j: (j, 0))]
        out_shape += [jax.ShapeDtypeStruct((D, FH), BF), jax.ShapeDtypeStruct((D, FH), BF),
                      jax.ShapeDtypeStruct((FH, D), BF)]
    out = pl.pallas_call(
        functools.partial(_ffn_kernel, nj=nj, row0=mod_row0, final=final_gain is not None, convert=nxt is not None,
                          side=side is not None, raw=raw is not None),
        grid=(nb, ni, nj),
        in_specs=in_specs,
        out_specs=out_specs,
        out_shape=out_shape,
        scratch_shapes=[pltpu.VMEM((tm, D), BF), pltpu.VMEM((tm, 128), F32)],
        compiler_params=_cp(("arbitrary",) * 3 if side is not None else ("parallel", "parallel", "arbitrary")),
        name="ffn",
    )(*args)
    return out if len(out) > 1 else out[0]


def _prenorm_kernel(hl_ref, hc_ref, sh_ref, sc_ref, gain_ref, o_ref, rs_ref):
    b = pl.program_id(0)
    i = pl.program_id(1)

    @pl.when(i < T // TC)
    def _():
        _modnorm_tiles(hl_ref, o_ref, rs_ref, gain_ref[...], _mrow(sh_ref, b), _mrow(sc_ref, b))

    @pl.when(i >= T // TC)
    def _():
        _modnorm_tiles(hc_ref, o_ref, rs_ref, gain_ref[...], _mrow(sh_ref, NB), _mrow(sc_ref, NB))


def _prenorm(h, h_ctx, mod, gain):
    nlat = T // TC
    return pl.pallas_call(
        _prenorm_kernel,
        grid=(NB, TT // TC),
        in_specs=[pl.BlockSpec((None, TC, D), lambda b, i: (b, jnp.minimum(i, nlat - 1), 0)),
                  pl.BlockSpec((None, TC, D), lambda b, i: (b, 0, 0)),
                  _mod_spec(3), _mod_spec(4),
                  pl.BlockSpec((1, D), lambda b, i: (0, 0))],
        out_specs=pl.BlockSpec((None, TC, D), lambda b, i: (b, i, 0)),
        out_shape=jax.ShapeDtypeStruct((NB, TT, D), BF),
        scratch_shapes=[pltpu.VMEM((TC, 128), F32)],
        compiler_params=_cp(("parallel", "arbitrary")),
        name="prenorm",
    )(h, h_ctx, mod, mod, gain.reshape(1, D))


def _mm_kernel(a_ref, b_ref, o_ref):
    o_ref[...] = _dot(a_ref[...], b_ref[...]).astype(o_ref.dtype)


def _mm_split_kernel(a_ref, b_ref, o1_ref, o2_ref):
    r = _dot(a_ref[...], b_ref[...])
    n1 = o1_ref.shape[1]
    o1_ref[...] = r[:, :n1].astype(o1_ref.dtype)
    o2_ref[...] = r[:, n1:]


def _matmul_split(a, w, tm, n1, name):
    m, k = a.shape
    n = w.shape[1]
    return pl.pallas_call(
        _mm_split_kernel,
        grid=(m // tm,),
        in_specs=[pl.BlockSpec((tm, k), lambda i: (i, 0)),
                  pl.BlockSpec((k, n), lambda i: (0, 0))],
        out_specs=[pl.BlockSpec((tm, n1), lambda i: (i, 0)),
                   pl.BlockSpec((tm, n - n1), lambda i: (i, 0))],
        out_shape=[jax.ShapeDtypeStruct((m, n1), BF), jax.ShapeDtypeStruct((m, n - n1), F32)],
        compiler_params=_cp(("parallel",)),
        name=name,
    )(a, w)


def _matmul(a, w, n, tm, tn, out_dtype, name):
    m, k = a.shape
    return pl.pallas_call(
        _mm_kernel,
        grid=(m // tm, n // tn),
        in_specs=[pl.BlockSpec((tm, k), lambda i, j: (i, 0)),
                  pl.BlockSpec((k, tn), lambda i, j: (0, j))],
        out_specs=pl.BlockSpec((tm, tn), lambda i, j: (i, j)),
        out_shape=jax.ShapeDtypeStruct((m, n), out_dtype),
        compiler_params=_cp(("parallel", "arbitrary")),
        name=name,
    )(a, w)


def _conv_kernel(x_ref, w_ref, o_ref, pad_ref):
    j = pl.program_id(1)
    half = CONV_K // 2
    nblk = x_ref.shape[1] // HD
    for c in range(nblk):
        cols = slice(c * HD, (c + 1) * HD)
        for r0, n in ((0, T), (T, TC)):
            pad_ref[0:8, :] = jnp.zeros((8, HD), F32)
            pad_ref[8:8 + n, :] = x_ref[r0:r0 + n, cols].astype(F32)
            pad_ref[8 + n:16 + n, :] = jnp.zeros((8, HD), F32)
            acc = jnp.zeros((n, HD), F32)
            for k in range(CONV_K):
                acc = acc + w_ref[k:k + 1, cols] * pad_ref[8 + k - half:8 + k - half + n, :]
            y = _silu(acc)
            yn = y * lax.rsqrt(jnp.sum(y * y, axis=-1, keepdims=True) + EPS)
            o_ref[r0:r0 + n, cols] = jnp.where(j * nblk + c < 2 * NH, yn, y).astype(o_ref.dtype)


def _gdn_conv(p, conv_w, nblk=6):
    return pl.pallas_call(
        _conv_kernel,
        grid=(NB, 3 * NH // nblk),
        in_specs=[pl.BlockSpec((None, TT, nblk * HD), lambda b, j: (b, 0, j)),
                  pl.BlockSpec((CONV_K, nblk * HD), lambda b, j: (0, j))],
        out_specs=pl.BlockSpec((None, TT, nblk * HD), lambda b, j: (b, 0, j)),
        out_shape=jax.ShapeDtypeStruct((NB, TT, 3 * W_A), BF),
        scratch_shapes=[pltpu.VMEM((T + 16, HD), F32)],
        compiler_params=_cp(("parallel", "parallel")),
        name="gdn_conv",
    )(p, conv_w)


def _split3(x):
    a = x.astype(BF)
    r = x - a.astype(F32)
    b = r.astype(BF)
    return a, b, (r - b.astype(F32)).astype(BF)


def _gdn_kernel(qf_ref, kf_ref, vf_ref, gf_ref, qb_ref, kb_ref, vb_ref, gb_ref, prm_ref, of_ref, ob_ref, s_ref):
    nbs = qf_ref.shape[0]
    @pl.when(pl.program_id(1) == 0)
    def _():
        s_ref[...] = jnp.zeros_like(s_ref)

    ri = lax.broadcasted_iota(jnp.int32, (CS, CS), 0)
    ci = lax.broadcasted_iota(jnp.int32, (CS, CS), 1)
    eye_f = jnp.where(ri == ci, 1.0, 0.0)
    pair = (ri // 2) == (ci // 2)
    levels = [((ri // (2 * bs)) == (ci // (2 * bs))) & ((ri // bs) != (ci // bs)) for bs in (2, 4, 8, 16, 32)]
    prm = prm_ref[...]
    a_neg = -jnp.exp(prm[0:1, :])
    dt_b = prm[1:2, :]

    ch = []
    dir_refs = ((qf_ref, kf_ref, vf_ref, gf_ref, of_ref), (qb_ref, kb_ref, vb_ref, gb_ref, ob_ref))
    for bb, d in [(bb, d) for bb in range(nbs) for d in range(2)]:
        q_ref, k_ref, v_ref, g_ref, o_ref = [r.at[bb] for r in dir_refs[d]]
        dm = (ri - ci) if d == 0 else (ci - ri)
        incl = dm >= 0
        strict = dm > 0
        tri = jnp.where(incl, 1.0, 0.0).astype(BF)
        tri_t = jnp.where(dm <= 0, 1.0, 0.0).astype(BF)
        gts = g_ref[...]
        x = gts + dt_b
        glog = a_neg * (jnp.maximum(x, 0.0) + jnp.log(1.0 + jnp.exp(-jnp.abs(x))))
        beta = jax.nn.sigmoid(gts)
        parts = _split3(glog)
        gc = sum(_dot(tri, g) for g in parts)
        gc_t = sum(_dot_tn(g, tri_t) for g in parts)
        last = CS - 1 if d == 0 else 0
        g_last = gc[last:last + 1, :]
        eg = jnp.exp(gc)
        ek = jnp.exp(g_last - gc)
        g_tot = jnp.exp(g_last)
        for h in range(NH):
            c = d * NH + h
            ch.append(dict(sl=slice(h * HD, (h + 1) * HD), c=bb * 2 * NH + c, incl=incl, strict=strict, o_ref=o_ref,
                           q_ref=q_ref, k_ref=k_ref, v_ref=v_ref,
                           b_c=beta[:, 2 * NH + c:2 * NH + c + 1], gc_c=gc[:, c:c + 1], gc_r=gc_t[c:c + 1, :],
                           eg=eg[:, c:c + 1], ek=ek[:, c:c + 1], g_tot=g_tot[:, c:c + 1]))

    for x in ch:
        kb = x["k_ref"][:, x["sl"]]
        qh = x["q_ref"][:, x["sl"]].astype(F32) * (HD ** -0.5)
        x["kh"], x["qh"] = kb.astype(F32), qh
        kq = _dot_nt(jnp.concatenate([kb, qh.astype(BF)], axis=0), kb)
        x["kk"], x["qk"] = kq[:CS], kq[CS:]
    for x in ch:
        decay = jnp.exp(jnp.where(x["incl"], x["gc_c"] - x["gc_r"], -1e30))
        lower = jnp.where(x["strict"], x["b_c"] * x["kk"] * decay, 0.0)
        x["qk"] = (x["qk"] * decay).astype(BF)
        x["lower"] = lower.astype(BF)
        x["t"] = eye_f - jnp.where(pair, lower, 0.0)
    for off in levels:
        for x in ch:
            x["tb"] = x["t"].astype(BF)
            x["mt"] = _dot(jnp.where(off, x["lower"], jnp.zeros_like(x["lower"])), x["tb"]).astype(BF)
        for x in ch:
            x["t"] = x["t"] - _dot(x["tb"], x["mt"])
    for x in ch:
        vh = x["v_ref"][:, x["sl"]].astype(F32)
        rhs = jnp.concatenate([vh * x["b_c"], x["kh"] * (x["b_c"] * x["eg"])], axis=1).astype(BF)
        x["uw"] = _dot(x["t"].astype(BF), rhs)
        x["q_dec"] = (x["qh"] * x["eg"]).astype(BF)
        x["k_dec"] = (x["kh"] * x["ek"]).astype(BF)
    for x in ch:
        st = s_ref[x["c"]]
        x["st"] = st
        stb = st.astype(BF)
        ws = _dot(jnp.concatenate([x["uw"][:, HD:].astype(BF), x["q_dec"]], axis=0), stb)
        x["vn"] = (x["uw"][:, :HD] - ws[:CS]).astype(BF)
        x["qs"] = ws[CS:]
    for x in ch:
        x["o_ref"][:, x["sl"]] = x["qs"] + _dot(x["qk"], x["vn"])
        s_ref[x["c"]] = x["st"] * x["g_tot"] + _dot_tn(x["k_dec"], x["vn"])


def _gdn_scan(qkv, gates, prm, nbs=4):
    chunk_f = lambda s: (s + NCH_LAT) % NCH
    chunk_b = lambda s: NCH - 1 - s

    def specs(chunk):
        qkv_spec = lambda c: pl.BlockSpec((nbs, CS, W_A), lambda b, s: (b, chunk(s), c))
        return [qkv_spec(0), qkv_spec(1), qkv_spec(2),
                pl.BlockSpec((nbs, CS, 128), lambda b, s: (b, chunk(s), 0))]

    return pl.pallas_call(
        _gdn_kernel,
        grid=(NB // nbs, NCH),
        in_specs=specs(chunk_f) + specs(chunk_b) + [pl.BlockSpec((8, 128), lambda b, s: (0, 0))],
        out_specs=[pl.BlockSpec((nbs, CS, W_A), lambda b, s: (b, chunk_f(s), 0)),
                   pl.BlockSpec((nbs, CS, W_A), lambda b, s: (b, chunk_b(s), 0))],
        out_shape=[jax.ShapeDtypeStruct((NB, TT, W_A), F32)] * 2,
        scratch_shapes=[pltpu.VMEM((nbs * 2 * NH, HD, HD), F32)],
        compiler_params=_cp(("parallel", "arbitrary")),
        name="gdn_scan",
    )(qkv, qkv, qkv, gates, qkv, qkv, qkv, gates, prm)


def _rope_kernel(x_ref, g_ref, cos_ref, sin_ref, o_ref):
    lane = lax.broadcasted_iota(jnp.int32, (TT, HD), 1)
    first = (lane % 64) < 32
    for hd in range(NH + KVH):
        cols = slice(hd * HD, (hd + 1) * HD)
        x = x_ref[:, cols].astype(F32)
        y = x * lax.rsqrt(jnp.mean(x * x, axis=-1, keepdims=True) + EPS) * g_ref[hd]
        partner = jnp.where(first, pltpu.roll(y, 128 - 32, axis=1), pltpu.roll(y, 32, axis=1))
        o_ref[:, cols] = (y * cos_ref[...] + partner * sin_ref[...]).astype(o_ref.dtype)


def _rope_norm(pb, gains, cos_t, sin_t):
    nh = NH + KVH
    return pl.pallas_call(
        _rope_kernel,
        grid=(NB,),
        in_specs=[pl.BlockSpec((None, TT, nh * HD), lambda b: (b, 0, 0)),
                  pl.BlockSpec((nh, 1, HD), lambda b: (0, 0, 0)),
                  pl.BlockSpec((TT, HD), lambda b: (0, 0)),
                  pl.BlockSpec((TT, HD), lambda b: (0, 0))],
        out_specs=pl.BlockSpec((None, TT, nh * HD), lambda b: (b, 0, 0)),
        out_shape=jax.ShapeDtypeStruct((NB, TT, nh * HD), BF),
        compiler_params=_cp(("parallel",)),
        name="rope_norm",
    )(pb, gains, cos_t, sin_t)


def _attn_kernel(q_ref, k_ref, v_ref, o_ref, s_ref, p_ref):
    k = k_ref[...]
    v_aug = jnp.concatenate([v_ref[...], jnp.ones((TT, HD), BF)], axis=1)
    tq = q_ref.shape[0]
    rb = 16
    for g in range(GQ):
        s_ref[g] = _dot_nt(q_ref[:, g * HD:(g + 1) * HD], k)
    for g in range(GQ):
        sl = slice(g * HD, (g + 1) * HD)
        for r in range(tq // rb):
            rows = slice(r * rb, (r + 1) * rb)
            m = s_ref[g, rows, 0:HD]
            for c in range(1, TT // HD):
                m = jnp.maximum(m, s_ref[g, rows, c * HD:(c + 1) * HD])
            mx = jnp.broadcast_to(jnp.max(m, axis=-1, keepdims=True), (rb, HD))
            for c in range(TT // HD):
                cols = slice(c * HD, (c + 1) * HD)
                p_ref[g, rows, cols] = jnp.exp(s_ref[g, rows, cols] - mx).astype(BF)
        o = _dot(p_ref[g], v_aug)
        o_ref[:, sl] = (o[:, :HD] / o[:, HD:]).astype(o_ref.dtype)


def _attention(qk, pb, tq=512):
    return pl.pallas_call(
        _attn_kernel,
        grid=(NB, KVH, T // tq),
        in_specs=[pl.BlockSpec((None, tq, GQ * HD), lambda b, kv, i: (b, i, kv)),
                  pl.BlockSpec((None, TT, HD), lambda b, kv, i: (b, 0, NH + kv)),
                  pl.BlockSpec((None, TT, HD), lambda b, kv, i: (b, 0, COL_VB // HD + kv))],
        out_specs=pl.BlockSpec((None, tq, GQ * HD), lambda b, kv, i: (b, i, kv)),
        out_shape=jax.ShapeDtypeStruct((NB, T, W_A), BF),
        scratch_shapes=[pltpu.VMEM((GQ, tq, TT), F32), pltpu.VMEM((GQ, tq, TT), BF)],
        compiler_params=_cp(("parallel", "parallel", "arbitrary")),
        name="attention",
    )(qk, qk, pb)


def _mixout_kernel(of_ref, ob_ref, z_ref, a_ref, h_ref, gt_ref, gain_ref, w_ref, o_ref):
    parts = []
    for hd in range(NH):
        sl = slice(hd * HD, (hd + 1) * HD)
        o = of_ref[:, sl] + ob_ref[:, sl]
        n = o * lax.rsqrt(jnp.mean(o * o, axis=-1, keepdims=True) + EPS) * gain_ref[...]
        parts.append((n * _silu(z_ref[:, sl].astype(F32))).astype(BF))
    gd = jnp.concatenate(parts, axis=1)
    y = _dot(gd, w_ref[0:W_A, :]) + _dot(a_ref[...], w_ref[W_A:2 * W_A, :])
    o_ref[...] = h_ref[...] + _mrow(gt_ref, pl.program_id(0)) * y


def _mixout(o_f, o_b, pa, attn, h, mod, gain, w_out, tm=512):
    return pl.pallas_call(
        _mixout_kernel,
        grid=(NB, T // tm),
        in_specs=[pl.BlockSpec((None, tm, W_A), lambda b, i: (b, i, 0)),
                  pl.BlockSpec((None, tm, W_A), lambda b, i: (b, i, 0)),
                  pl.BlockSpec((None, tm, W_A), lambda b, i: (b, i, COL_Z // W_A)),
                  pl.BlockSpec((None, tm, W_A), lambda b, i: (b, i, 0)),
                  pl.BlockSpec((None, tm, D), lambda b, i: (b, i, 0)),
                  _mod_spec(5),
                  pl.BlockSpec((1, HD), lambda b, i: (0, 0)),
                  pl.BlockSpec((D, D), lambda b, i: (0, 0), pipeline_mode=pl.Buffered(1))],
        out_specs=pl.BlockSpec((None, tm, D), lambda b, i: (b, i, 0)),
        out_shape=jax.ShapeDtypeStruct((NB, T, D), F32),
        compiler_params=_cp(("parallel", "parallel")),
        name="mix_out",
    )(o_f, o_b, pa, attn, h, mod, gain.reshape(1, HD), w_out)


def _dft_tables():
    def cs(n):
        idx = np.arange(n, dtype=np.int64)
        ang = 2.0 * np.pi * ((idx[:, None] * idx[None, :]) % n).astype(np.float64) / n
        return np.cos(ang) / math.sqrt(n), np.sin(ang) / math.sqrt(n)

    cc, sc = cs(GC)
    ct, st = cs(T)
    ch = np.concatenate([cc, sc], axis=1).astype(np.float32)
    tm = np.concatenate([ct, -st], axis=1).astype(np.float32)
    return jnp.asarray(ch, dtype=BF), jnp.asarray(tm, dtype=BF)


def _fchan_kernel(h_ref, sh_ref, sc_ref, gain_ref, w_ref, a_ref, b_ref, u_ref, rs_ref):
    b = pl.program_id(0)
    _modnorm_tiles(h_ref, u_ref, rs_ref, gain_ref[...], _mrow(sh_ref, b), _mrow(sc_ref, b))
    w = w_ref[...]
    for g in range(FG):
        sl = slice(g * GC, (g + 1) * GC)
        ab = _dot(u_ref[:, sl], w)
        a_ref[:, sl] = ab[:, :GC].astype(a_ref.dtype)
        b_ref[:, sl] = ab[:, GC:].astype(b_ref.dtype)


def _fourier_channels(h, mod, gain, ch_tab, tm=1024):
    return pl.pallas_call(
        _fchan_kernel,
        grid=(NB, T // tm),
        in_specs=[pl.BlockSpec((None, tm, D), lambda b, i: (b, i, 0)),
                  _mod_spec(3), _mod_spec(4),
                  pl.BlockSpec((1, D), lambda b, i: (0, 0)),
                  pl.BlockSpec((GC, 2 * GC), lambda b, i: (0, 0))],
        out_specs=[pl.BlockSpec((None, tm, D), lambda b, i: (b, i, 0))] * 2,
        out_shape=[jax.ShapeDtypeStruct((NB, T, D), BF)] * 2,
        scratch_shapes=[pltpu.VMEM((tm, D), BF), pltpu.VMEM((tm, 128), F32)],
        compiler_params=_cp(("parallel", "parallel")),
        name="fourier_channels",
    )(h, mod, mod, gain.reshape(1, D), ch_tab)


def _ftime_kernel(t_ref, a_ref, b_ref, o_ref):
    y = _dot(t_ref[:, :T], a_ref[...]) + _dot(t_ref[:, T:], b_ref[...])
    o_ref[...] = y.astype(o_ref.dtype)


def _fourier_time(tm_tab, a, b, tm=1024, tn=1024):
    return pl.pallas_call(
        _ftime_kernel,
        grid=(NB, D // tn, T // tm),
        in_specs=[pl.BlockSpec((tm, 2 * T), lambda b, j, i: (i, 0)),
                  pl.BlockSpec((None, T, tn), lambda b, j, i: (b, 0, j)),
                  pl.BlockSpec((None, T, tn), lambda b, j, i: (b, 0, j))],
        out_specs=pl.BlockSpec((None, tm, tn), lambda b, j, i: (b, i, j)),
        out_shape=jax.ShapeDtypeStruct((NB, T, D), BF),
        compiler_params=_cp(("parallel", "parallel", "arbitrary")),
        name="fourier_time",
    )(tm_tab, a, b)


def _resmm_kernel(y_ref, w_ref, h_ref, gt_ref, o_ref):
    o_ref[...] = h_ref[...] + _mrow(gt_ref, pl.program_id(0)) * _dot(y_ref[...], w_ref[...])


def _fourier_out(y, w_f, h, mod, tm=512):
    return pl.pallas_call(
        _resmm_kernel,
        grid=(NB, T // tm),
        in_specs=[pl.BlockSpec((None, tm, D), lambda b, i: (b, i, 0)),
                  pl.BlockSpec((D, D), lambda b, i: (0, 0)),
                  pl.BlockSpec((None, tm, D), lambda b, i: (b, i, 0)),
                  _mod_spec(5)],
        out_specs=pl.BlockSpec((None, tm, D), lambda b, i: (b, i, 0)),
        out_shape=jax.ShapeDtypeStruct((NB, T, D), F32),
        compiler_params=_cp(("parallel", "parallel")),
        name="fourier_out",
    )(y, w_f, h, mod)


def _rope_tables():
    half = HD // 4
    inv = ROPE_THETA ** (-np.arange(half, dtype=np.float64) / half)
    t = np.arange(T)
    ang_r = (t // GRID_W)[:, None] * inv[None, :]
    ang_c = (t % GRID_W)[:, None] * inv[None, :]
    cos = np.concatenate([np.cos(ang_r)] * 2 + [np.cos(ang_c)] * 2, axis=1)
    sin = np.concatenate([-np.sin(ang_r), np.sin(ang_r), -np.sin(ang_c), np.sin(ang_c)], axis=1)
    cos = np.concatenate([cos, np.ones((TC, HD))], axis=0).astype(np.float32)
    sin = np.concatenate([sin, np.zeros((TC, HD))], axis=0).astype(np.float32)
    return jnp.asarray(cos), jnp.asarray(sin)


def kernel(x, c, ctx, c_ctx, w_mod, b_mod, norm_w, ffn_w_gu, ffn_w_down, w_in, conv_w, a_log, dt_bias,
           gdn_norm, q_norm, k_norm, w_out, w_fourier, final_norm):
    cvec = jnp.concatenate([c, c_ctx[None, :], jnp.zeros((8 - NB - 1, D), F32)], axis=0)
    b_mod3 = b_mod.reshape(DEPTH, 1, N_MOD * D)
    mod0 = _modulation(cvec, w_mod, b_mod3, 0)

    h_ctx, w_g, w_u, w_dn = _ffn(ctx.reshape(1, NB * TC, D), mod0, 0, NB, norm_w[0, 0], None, None,
                                 raw=(ffn_w_gu, ffn_w_down, 0, 0))
    h_ctx = h_ctx.reshape(NB, TC, D)
    h, w_gu, w_dn, mod1 = _ffn(x, mod0, 0, 0, norm_w[0, 0], (w_g, w_u), w_dn, nxt=(ffn_w_gu, ffn_w_down, 0, 1),
                               side=(cvec, w_mod, b_mod3, 1))

    wi = w_in[0]
    wi = wi.astype(BF)
    w_b = jnp.concatenate([wi[:, NPA + 4 * NH:], wi[:, NPA:NPA + 4 * NH], jnp.zeros((D, 128 - 4 * NH), BF)], axis=1)
    u = _prenorm(h, h_ctx, mod0, norm_w[0, 1]).reshape(NB * TT, D)
    pa = _matmul(u, wi, NPA, 1024, 2048, BF, "in_proj_a").reshape(NB, TT, NPA)
    pb, gates = _matmul_split(u, w_b, 1024, NPB, "in_proj_b")
    pb = pb.reshape(NB, TT, NPB)

    qkv = _gdn_conv(pa, conv_w[0])
    prm = jnp.zeros((8, 128), F32)
    prm = prm.at[0, :2 * NH].set(a_log[0].reshape(-1)).at[1, :2 * NH].set(dt_bias[0].reshape(-1))
    o_f, o_b = _gdn_scan(qkv, gates.reshape(NB, TT, 128), prm)

    gains = jnp.concatenate([jnp.broadcast_to(q_norm[0] * (HD ** -0.5), (NH, HD)),
                             jnp.broadcast_to(k_norm[0], (KVH, HD))], axis=0).reshape(NH + KVH, 1, HD)
    cos_t, sin_t = _rope_tables()
    qk = _rope_norm(pb, gains, cos_t, sin_t)
    attn = _attention(qk, pb)

    h = _mixout(o_f, o_b, pa, attn, h, mod0, gdn_norm[0], w_out[0].astype(BF))
    h, w_gu, w_dn = _ffn(h, mod0, 6, 0, norm_w[0, 2], w_gu, w_dn, nxt=(ffn_w_gu, ffn_w_down, 1, 0))

    h, w_gu, w_dn = _ffn(h, mod1, 0, 0, norm_w[1, 0], w_gu, w_dn, nxt=(ffn_w_gu, ffn_w_down, 1, 1))
    ch_tab, tm_tab = _dft_tables()
    fa, fb = _fourier_channels(h, mod1, norm_w[1, 1], ch_tab)
    y = _fourier_time(tm_tab, fa, fb)
    h = _fourier_out(y, w_fourier[0].astype(BF), h, mod1)
    return _ffn(h, mod1, 6, 0, norm_w[1, 2], w_gu, w_dn, final_gain=final_norm)
```
